```python
import math
import jax, jax.numpy as jnp
from jax import lax
import numpy as np

D_MODEL = 1024
BATCH = 8
SEQ = 2048
DEPTH = 4

GDN_HEADS = 4
GDN_HEAD_DIM = 128
GDN_CONV = 4
GDN_CHUNK = 64
DSW_HEADS = 4
DSW_HEAD_DIM = 64
DSW_PATTERNS = ((128, 1), (512, 4), (2048, 16))
WIN_BLOCK = 128
DIFF_HEADS = 4
DIFF_QK_DIM = 32
DIFF_V_DIM = 64
ATTN_QBLOCK = 128
D_FF = 2752
FFN_CONV = 3
DEEPNORM_ALPHA = (2 * DEPTH) ** 0.25
DEEPNORM_BETA = (8 * DEPTH) ** -0.25
EPS = 1e-5

GDN_W = GDN_HEADS * GDN_HEAD_DIM
DSW_W = DSW_HEADS * DSW_HEAD_DIM
DIFF_W = DIFF_HEADS * DIFF_V_DIM
MIX_WIDTH = GDN_W + DSW_W + DIFF_W
DIFF_QK_W = DIFF_HEADS * 2 * DIFF_QK_DIM
IN_SPLITS = (3 * GDN_W, GDN_W, GDN_HEADS, GDN_HEADS, 3 * DSW_W, 2 * DIFF_QK_W + DIFF_W)
IN_WIDTH = sum(IN_SPLITS)

kernel_name = "hybrid_gdn_dilated_diff_deepnorm"


def _layer_norm(x, g, b):
    xf = x.astype(jnp.float32)
    mu = jnp.mean(xf, axis=-1, keepdims=True)
    var = jnp.mean(jnp.square(xf - mu), axis=-1, keepdims=True)
    return ((xf - mu) * lax.rsqrt(var + EPS) * g + b).astype(x.dtype)


def _rms_norm(x, w):
    xf = x.astype(jnp.float32)
    return xf * lax.rsqrt(jnp.mean(jnp.square(xf), axis=-1, keepdims=True) + EPS) * w


def _l2norm(x):
    return x * lax.rsqrt(jnp.sum(jnp.square(x), axis=-1, keepdims=True) + 1e-6)


def _heads(a, n_heads):
    B, T, _ = a.shape
    return a.reshape(B, T, n_heads, -1).transpose(0, 2, 1, 3)


def _merge_heads(a):
    B, H, T, D = a.shape
    return a.transpose(0, 2, 1, 3).reshape(B, T, H * D)


def _causal_depthwise_conv(x, w):
    K = w.shape[0]
    T = x.shape[1]
    xp = jnp.pad(x, ((0, 0), (K - 1, 0), (0, 0)))
    y = xp[:, 0:T] * w[0]
    for j in range(1, K):
        y = y + xp[:, j:j + T] * w[j]
    return y


def _chunked_gated_delta_rule(q, k, v, g, beta):
    B, H, T, Dk = q.shape
    Dv = v.shape[-1]
    n = T // GDN_CHUNK
    C = GDN_CHUNK
    rs = lambda a: a.reshape(B, H, n, C, *a.shape[3:])
    q, k, v, g, beta = rs(q), rs(k), rs(v), rs(g), rs(beta)
    g = jnp.cumsum(g, axis=-1)
    idx = jnp.arange(C)
    lower_incl = idx[:, None] >= idx[None, :]
    strict = idx[:, None] > idx[None, :]
    decay = jnp.exp(jnp.where(lower_incl, g[..., :, None] - g[..., None, :], -jnp.inf))
    k_beta = k * beta[..., None]
    m = jnp.where(strict, jnp.einsum('bhncd,bhnsd->bhncs', k_beta, k) * decay, 0.0)
    a_mat = jnp.eye(C, dtype=jnp.float32) + m
    u = lax.linalg.triangular_solve(a_mat, v * beta[..., None], left_side=True, lower=True, unit_diagonal=True)
    w = lax.linalg.triangular_solve(a_mat, k_beta * jnp.exp(g)[..., None], left_side=True, lower=True, unit_diagonal=True)
    qk = jnp.where(lower_incl, jnp.einsum('bhncd,bhnsd->bhncs', q, k) * decay, 0.0)
    g_last = g[..., -1]
    k_tail = k * jnp.exp(g_last[..., None] - g)[..., None]
    q_dec = q * jnp.exp(g)[..., None]

    def step(S, inp):
        q_i, w_i, u_i, qk_i, kt_i, gl_i = inp
        v_new = u_i - jnp.einsum('bhcd,bhde->bhce', w_i, S)
        o = jnp.einsum('bhcd,bhde->bhce', q_i, S) + jnp.einsum('bhcs,bhse->bhce', qk_i, v_new)
        S = S * jnp.exp(gl_i)[..., None, None] + jnp.einsum('bhcd,bhce->bhde', kt_i, v_new)
        return S, o

    xs = tuple(jnp.moveaxis(t, 2, 0) for t in (q_dec, w, u, qk, k_tail, g_last))
    S0 = jnp.zeros((B, H, Dk, Dv), jnp.float32)
    _, o = lax.scan(step, S0, xs)
    return jnp.moveaxis(o, 0, 2).reshape(B, H, T, Dv)


def _gated_deltanet(qkv, z, b, a, conv_w, a_log, dt_bias, norm_w):
    f32 = jnp.float32
    B, T, _ = z.shape
    qkv = jax.nn.silu(_causal_depthwise_conv(qkv, conv_w)).astype(f32)
    q, k, v = (_heads(t, GDN_HEADS) for t in jnp.split(qkv, 3, axis=-1))
    q = _l2norm(q) * GDN_HEAD_DIM ** -0.5
    k = _l2norm(k)
    beta = jax.nn.sigmoid(b.astype(f32)).transpose(0, 2, 1)
    g = (-jnp.exp(a_log.astype(f32)) * jax.nn.softplus(a.astype(f32) + dt_bias.astype(f32))).transpose(0, 2, 1)
    o = _chunked_gated_delta_rule(q, k, v, g, beta).transpose(0, 2, 1, 3)
    zz = z.reshape(B, T, GDN_HEADS, GDN_HEAD_DIM).astype(f32)
    y = _rms_norm(o, norm_w) * jax.nn.silu(zz)
    return y.reshape(B, T, GDN_W).astype(z.dtype)


def _banded_window_attn(q, k, v, window):
    *lead, L, D = q.shape
    nb = -(-L // WIN_BLOCK)
    pad = nb * WIN_BLOCK - L
    lp = [(0, 0)] * len(lead)
    qb = jnp.pad(q, lp + [(0, pad), (0, 0)]).reshape(*lead, nb, WIN_BLOCK, D)

    def kv_blocks(t):
        t = jnp.pad(t, lp + [(WIN_BLOCK, pad), (0, 0)]).reshape(*lead, nb + 1, WIN_BLOCK, D)
        return jnp.concatenate([t[..., :-1, :, :], t[..., 1:, :, :]], axis=-2)

    kb, vb = kv_blocks(k), kv_blocks(v)
    i = jnp.arange(WIN_BLOCK)[:, None]
    j = jnp.arange(2 * WIN_BLOCK)[None, :]
    dist = WIN_BLOCK + i - j
    blk = jnp.arange(nb)[:, None, None]
    valid = (dist >= 0) & (dist <= window) & ((blk > 0) | (j >= WIN_BLOCK))
    s = jnp.einsum('...nqd,...nkd->...nqk', qb, kb).astype(jnp.float32) * D ** -0.5
    s = jnp.where(valid, s, -jnp.inf)
    mx = jnp.max(s, axis=-1, keepdims=True)
    p = jnp.exp(s - mx)
    den = jnp.sum(p, axis=-1, keepdims=True)
    o = jnp.einsum('...nqk,...nkd->...nqd', p, vb.astype(jnp.float32)) / den
    lse = (mx + jnp.log(den))[..., 0]
    o = o.reshape(*lead, nb * WIN_BLOCK, D)[..., :L, :]
    lse = lse.reshape(*lead, nb * WIN_BLOCK)[..., :L]
    return o, lse


def _dilated_window_group(qkv):
    q, k, v = (_heads(t, DSW_HEADS) for t in jnp.split(qkv, 3, axis=-1))
    B, H, T, D = q.shape
    outs, lses = [], []
    for window, dilation in DSW_PATTERNS:
        L = T // dilation
        regroup = lambda t: jnp.swapaxes(t.reshape(B, H, L, dilation, D), 2, 3)
        o, lse = _banded_window_attn(regroup(q), regroup(k), regroup(v), window // dilation)
        outs.append(jnp.swapaxes(o, 2, 3).reshape(B, H, T, D))
        lses.append(jnp.swapaxes(lse, 2, 3).reshape(B, H, T))
    wts = jax.nn.softmax(jnp.stack(lses, axis=0), axis=0)
    o = jnp.sum(wts[..., None] * jnp.stack(outs, axis=0), axis=0)
    return _merge_heads(o).astype(qkv.dtype)


def _diff_attention_group(qkv, lam_vecs, norm_w, lam_init):
    f32 = jnp.float32
    B, T, _ = qkv.shape
    q, k, v = jnp.split(qkv, [DIFF_QK_W, 2 * DIFF_QK_W], axis=-1)
    two_maps = lambda t: t.reshape(B, T, DIFF_HEADS, 2, DIFF_QK_DIM).transpose(0, 2, 3, 1, 4)
    q, k = two_maps(q), two_maps(k)
    v = _heads(v, DIFF_HEADS).astype(f32)
    lv = lam_vecs.astype(f32)
    lam = jnp.exp(jnp.sum(lv[0] * lv[1])) - jnp.exp(jnp.sum(lv[2] * lv[3])) + lam_init
    nq = T // ATTN_QBLOCK
    q_blocks = jnp.moveaxis(q.reshape(B, DIFF_HEADS, 2, nq, ATTN_QBLOCK, DIFF_QK_DIM), 3, 0)
    kpos = jnp.arange(T)
    scale = DIFF_QK_DIM ** -0.5

    def one_block(args):
        q_blk, bi = args
        s = jnp.einsum('bhmqd,bhmkd->bhmqk', q_blk, k).astype(f32) * scale
        qpos = bi * ATTN_QBLOCK + jnp.arange(ATTN_QBLOCK)
        s = jnp.where(kpos[None, :] <= qpos[:, None], s, -jnp.inf)
        p = jax.nn.softmax(s, axis=-1)
        return jnp.einsum('bhqk,bhkd->bhqd', p[:, :, 0] - lam * p[:, :, 1], v)

    o = lax.map(one_block, (q_blocks, jnp.arange(nq)))
    o = jnp.moveaxis(o, 0, 2).reshape(B, DIFF_HEADS, T, DIFF_V_DIM)
    o = _rms_norm(o, norm_w) * (1.0 - lam_init)
    return _merge_heads(o).astype(qkv.dtype)


def _token_mixer(h, w_in, conv_w, a_log, dt_bias, gdn_norm_w, lam_vecs, diff_norm_w, w_out, lam_init):
    proj = h @ w_in
    a_qkv, a_z, a_b, a_dec, b_qkv, c_qkv = jnp.split(proj, np.cumsum(IN_SPLITS)[:-1], axis=-1)
    y_a = _gated_deltanet(a_qkv, a_z, a_b, a_dec, conv_w, a_log, dt_bias, gdn_norm_w)
    y_b = _dilated_window_group(b_qkv)
    y_c = _diff_attention_group(c_qkv, lam_vecs, diff_norm_w, lam_init)
    return jnp.concatenate([y_a, y_b, y_c], axis=-1) @ w_out


def _conv_glu_ffn(h, w_up, conv_w, w_down):
    u = _causal_depthwise_conv(h @ w_up, conv_w)
    gate, val = jnp.split(u, 2, axis=-1)
    return (jax.nn.silu(gate) * val) @ w_down


def setup_inputs(seed: int = 0) -> dict:
    key = jax.random.key(seed)
    ks = jax.random.split(key, 16)
    f32 = jnp.float32
    nrm = lambda k, s: jax.random.normal(k, s, f32)
    x = nrm(ks[0], (BATCH, SEQ, D_MODEL))
    w_in = nrm(ks[1], (DEPTH, D_MODEL, IN_WIDTH)) * D_MODEL ** -0.5
    gdn_conv = nrm(ks[2], (DEPTH, GDN_CONV, 3 * GDN_W)) * GDN_CONV ** -0.5
    gdn_a_log = jnp.log(jax.random.uniform(ks[3], (DEPTH, GDN_HEADS), f32, 1.0, 16.0))
    dt = jnp.exp(jax.random.uniform(ks[4], (DEPTH, GDN_HEADS), f32, math.log(1e-3), math.log(1e-1)))
    gdn_dt_bias = dt + jnp.log(-jnp.expm1(-dt))
    gdn_norm = 1.0 + 0.02 * nrm(ks[5], (DEPTH, GDN_HEAD_DIM))
    diff_lambda = 0.1 * nrm(ks[6], (DEPTH, 4, DIFF_QK_DIM))
    diff_norm = 1.0 + 0.02 * nrm(ks[7], (DEPTH, DIFF_V_DIM))
    w_out = nrm(ks[8], (DEPTH, MIX_WIDTH, D_MODEL)) * MIX_WIDTH ** -0.5 * DEEPNORM_BETA
    ln1_g = 1.0 + 0.02 * nrm(ks[9], (DEPTH, D_MODEL))
    ln1_b = 0.02 * nrm(ks[10], (DEPTH, D_MODEL))
    w_up = nrm(ks[11], (DEPTH, D_MODEL, 2 * D_FF)) * D_MODEL ** -0.5
    ffn_conv = nrm(ks[12], (DEPTH, FFN_CONV, 2 * D_FF)) * FFN_CONV ** -0.5
    w_down = nrm(ks[13], (DEPTH, D_FF, D_MODEL)) * D_FF ** -0.5 * DEEPNORM_BETA
    ln2_g = 1.0 + 0.02 * nrm(ks[14], (DEPTH, D_MODEL))
    ln2_b = 0.02 * nrm(ks[15], (DEPTH, D_MODEL))
    return {"x": x, "w_in": w_in, "gdn_conv": gdn_conv, "gdn_a_log": gdn_a_log,
            "gdn_dt_bias": gdn_dt_bias, "gdn_norm": gdn_norm, "diff_lambda": diff_lambda,
            "diff_norm": diff_norm, "w_out": w_out, "ln1_g": ln1_g, "ln1_b": ln1_b,
            "w_up": w_up, "ffn_conv": ffn_conv, "w_down": w_down, "ln2_g": ln2_g, "ln2_b": ln2_b}


def reference(x, w_in, gdn_conv, gdn_a_log, gdn_dt_bias, gdn_norm, diff_lambda, diff_norm,
              w_out, ln1_g, ln1_b, w_up, ffn_conv, w_down, ln2_g, ln2_b):
    for l in range(DEPTH):
        lam_init = 0.8 - 0.6 * math.exp(-0.3 * l)
        y = _token_mixer(x, w_in[l], gdn_conv[l], gdn_a_log[l], gdn_dt_bias[l], gdn_norm[l],
                         diff_lambda[l], diff_norm[l], w_out[l], lam_init)
        x = _layer_norm(DEEPNORM_ALPHA * x + y, ln1_g[l], ln1_b[l])
        f = _conv_glu_ffn(x, w_up[l], ffn_conv[l], w_down[l])
        x = _layer_norm(DEEPNORM_ALPHA * x + f, ln2_g[l], ln2_b[l])
    return x
```

```python
import functools
import math

import jax
import jax.numpy as jnp
from jax import lax
from jax.experimental import pallas as pl
from jax.experimental.pallas import tpu as pltpu

F32 = jnp.float32
BF16 = jnp.bfloat16

LANES = 128
SUBLANES = 8
VMEM_LIMIT = 48 * 1024 * 1024

GDN_HEADS = 4
GDN_HEAD_DIM = 128
GDN_CONV = 4
DSW_HEADS = 4
DSW_HEAD_DIM = 64
DIFF_HEADS = 4
DIFF_QK_DIM = 32
DIFF_V_DIM = 64
FFN_CONV = 3
EPS = 1e-5

GDN_W = GDN_HEADS * GDN_HEAD_DIM
DSW_W = DSW_HEADS * DSW_HEAD_DIM
DIFF_W = DIFF_HEADS * DIFF_V_DIM
DIFF_QK_W = DIFF_HEADS * 2 * DIFF_QK_DIM
A_W = 4 * GDN_W
BC_W = 3 * DSW_W + 2 * DIFF_QK_W + DIFF_W

ROW_TILE = 512
ATT_BLOCK = 128
GDN_CHUNK = 128
N_COUNT_TILES = 6


def _dot(a, b):
    return jnp.dot(a.astype(BF16), b.astype(BF16), preferred_element_type=F32)


def _dot_nt(a, b):
    return lax.dot_general(a.astype(BF16), b.astype(BF16), (((1,), (1,)), ((), ())),
                           preferred_element_type=F32)


def _dot_exact(a, b):
    return jnp.dot(a, b, preferred_element_type=F32, precision=lax.Precision.HIGHEST)


def _split_bf16(a):
    hi = a.astype(BF16)
    return hi, (a - hi.astype(F32)).astype(BF16)


def _dot_split(a, b):
    ah, al = _split_bf16(a)
    bh, bl = _split_bf16(b)
    d = lambda p, q: jnp.dot(p, q, preferred_element_type=F32)
    return d(ah, bh) + (d(ah, bl) + d(al, bh))


def _sigmoid(x):
    return 1.0 / (1.0 + jnp.exp(-x))


def _params(*sem):
    return pltpu.CompilerParams(dimension_semantics=sem, vmem_limit_bytes=VMEM_LIMIT)


def _in_proj_kernel(x_ref, wa_ref, wbc_ref, wbd_ref, scale_ref, a_ref, bc_ref, bd_ref, *, col_chunk):
    x = x_ref[...].astype(BF16)
    for c in range(0, A_W, col_chunk):
        a_ref[:, c:c + col_chunk] = jnp.dot(x, wa_ref[:, c:c + col_chunk], preferred_element_type=F32)
    for c in range(0, BC_W, col_chunk):
        r = jnp.dot(x, wbc_ref[:, c:c + col_chunk], preferred_element_type=F32)
        bc_ref[:, c:c + col_chunk] = (r * scale_ref[:, c:c + col_chunk]).astype(BF16)
    bd_ref[...] = jnp.dot(x, wbd_ref[...], preferred_element_type=F32)


def _in_proj(x, wa, wbc, wbd, scale, layer):
    n, d = x.shape
    tm = min(ROW_TILE, n)
    return pl.pallas_call(
        functools.partial(_in_proj_kernel, col_chunk=512),
        grid=(n // tm,),
        in_specs=[
            pl.BlockSpec((tm, d), lambda i: (i, 0)),
            pl.BlockSpec((None, d, A_W), lambda i: (layer, 0, 0)),
            pl.BlockSpec((None, d, BC_W), lambda i: (layer, 0, 0)),
            pl.BlockSpec((None, d, LANES), lambda i: (layer, 0, 0)),
            pl.BlockSpec((1, BC_W), lambda i: (0, 0)),
        ],
        out_specs=[
            pl.BlockSpec((tm, A_W), lambda i: (i, 0)),
            pl.BlockSpec((tm, BC_W), lambda i: (i, 0)),
            pl.BlockSpec((tm, LANES), lambda i: (i, 0)),
        ],
        out_shape=[
            jax.ShapeDtypeStruct((n, A_W), F32),
            jax.ShapeDtypeStruct((n, BC_W), BF16),
            jax.ShapeDtypeStruct((n, LANES), F32),
        ],
        compiler_params=_params("arbitrary"),
        name="in_proj",
    )(x, wa, wbc, wbd, scale)


def _gdn_kernel(q_ref, k_ref, v_ref, z_ref, bd_ref, cq_ref, ck_ref, cv_ref, alog_ref, dtb_ref, nw_ref,
                y_ref,
                pad_s, q_s, k_s, v_s, beta_s, g_s, u_s, w_s, qk_s, qd_s, ktt_s, gl_s, *, seq):
    C = GDN_CHUNK
    n_chunks = seq // C
    head = pl.program_id(1)

    def conv_silu(x_ref, cw_ref):
        pad_s[0:SUBLANES, :] = jnp.zeros((SUBLANES, LANES), F32)
        pad_s[SUBLANES:SUBLANES + seq, :] = x_ref[...]
        cw = cw_ref[...]
        first = SUBLANES - (GDN_CONV - 1)
        acc = pad_s[first:first + seq, :] * cw[0:1, :]
        for j in range(1, GDN_CONV):
            acc = acc + pad_s[first + j:first + j + seq, :] * cw[j:j + 1, :]
        return acc * _sigmoid(acc)

    q = conv_silu(q_ref, cq_ref)
    q_s[...] = q * lax.rsqrt(jnp.sum(q * q, axis=-1, keepdims=True) + 1e-6) * (GDN_HEAD_DIM ** -0.5)
    k = conv_silu(k_ref, ck_ref)
    k_s[...] = k * lax.rsqrt(jnp.sum(k * k, axis=-1, keepdims=True) + 1e-6)
    v_s[...] = conv_silu(v_ref, cv_ref)

    bd = bd_ref[...]
    lane = lax.broadcasted_iota(jnp.int32, (seq, LANES), 1)
    beta_all = _sigmoid(bd)
    xg = bd + dtb_ref[...]
    softplus = jnp.maximum(xg, 0.0) + jnp.log(1.0 + jnp.exp(-jnp.abs(xg)))
    g_all = -jnp.exp(alog_ref[...]) * softplus
    beta = jnp.sum(jnp.where(lane == head, beta_all, 0.0), axis=-1, keepdims=True)
    g = jnp.sum(jnp.where(lane == head + GDN_HEADS, g_all, 0.0), axis=-1, keepdims=True)
    beta_s[...] = jnp.broadcast_to(beta, (seq, LANES))
    g_s[...] = jnp.broadcast_to(g, (seq, LANES))

    ri = lax.broadcasted_iota(jnp.int32, (C, C), 0)
    ci = lax.broadcasted_iota(jnp.int32, (C, C), 1)
    lower_incl = ri >= ci
    strict = ri > ci
    tri = jnp.where(lower_incl, 1.0, 0.0).astype(F32)

    def chunk_local(c, carry):
        rows = pl.ds(pl.multiple_of(c * C, C), C)
        qc, kc, vc, bb = q_s[rows, :], k_s[rows, :], v_s[rows, :], beta_s[rows, :]
        cum = _dot_exact(tri, g_s[rows, :])
        diff = cum - cum.T
        decay = jnp.exp(jnp.where(lower_incl, diff, -jnp.inf))
        kb = kc * bb
        m = jnp.where(strict, _dot_nt(kb, kc) * decay, 0.0)
        nmat = -m
        p = _dot_split(m, m)
        span = 2
        while True:
            nmat = nmat + p + _dot_split(nmat, p)
            span *= 2
            if span >= C:
                break
            p = _dot_split(p, p)
        ecum = jnp.exp(cum)
        vb = vc * bb
        kbe = kb * ecum
        u_s[rows, :] = vb + _dot(nmat, vb)
        w_s[rows, :] = kbe + _dot(nmat, kbe)
        qk_s[rows, :] = jnp.where(lower_incl, _dot_nt(qc, kc) * decay, 0.0)
        qd_s[rows, :] = qc * ecum
        cum_last = cum[C - 1:C, :]
        ktt_s[rows, :] = (kc * jnp.exp(cum_last - cum)).T
        gl_s[pl.ds(pl.multiple_of(c * SUBLANES, SUBLANES), SUBLANES), :] = jnp.broadcast_to(
            jnp.exp(cum_last), (SUBLANES, LANES))
        return carry

    lax.fori_loop(0, n_chunks, chunk_local, 0)

    nw = nw_ref[...]

    def scan_step(c, s):
        rows = pl.ds(pl.multiple_of(c * C, C), C)
        v_new = u_s[rows, :] - _dot(w_s[rows, :], s)
        o = _dot(qd_s[rows, :], s) + _dot(qk_s[rows, :], v_new)
        gl = gl_s[pl.ds(pl.multiple_of(c * SUBLANES, SUBLANES), 1), :]
        s = s * gl + _dot(ktt_s[rows, :], v_new)
        z = z_ref[rows, :]
        o = o * lax.rsqrt(jnp.mean(o * o, axis=-1, keepdims=True) + EPS) * nw
        y_ref[rows, :] = (o * (z * _sigmoid(z))).astype(y_ref.dtype)
        return s

    lax.fori_loop(0, n_chunks, scan_step, jnp.zeros((GDN_HEAD_DIM, GDN_HEAD_DIM), F32))


def _gdn(a, bd, conv_w, alog_row, dtb_row, norm_row, layer):
    b, t, _ = a.shape
    tok = lambda off: pl.BlockSpec((None, t, LANES), lambda i, h: (i, 0, off + h))
    cw = lambda off: pl.BlockSpec((None, GDN_CONV, LANES), lambda i, h: (layer, 0, off + h))
    row = pl.BlockSpec((None, 1, LANES), lambda i, h: (layer, 0, 0))
    seq_buf = pltpu.VMEM((t, LANES), F32)
    return pl.pallas_call(
        functools.partial(_gdn_kernel, seq=t),
        grid=(b, GDN_HEADS),
        in_specs=[tok(0), tok(GDN_HEADS), tok(2 * GDN_HEADS), tok(3 * GDN_HEADS),
                  pl.BlockSpec((None, t, LANES), lambda i, h: (i, 0, 0)),
                  cw(0), cw(GDN_HEADS), cw(2 * GDN_HEADS), row, row, row],
        out_specs=pl.BlockSpec((None, t, LANES), lambda i, h: (i, 0, h)),
        out_shape=jax.ShapeDtypeStruct((b, t, GDN_W), BF16),
        scratch_shapes=[pltpu.VMEM((t + SUBLANES, LANES), F32)] + [seq_buf] * 10
        + [pltpu.VMEM((t // GDN_CHUNK * SUBLANES, LANES), F32)],
        compiler_params=_params("arbitrary", "arbitrary"),
        name="gdn",
    )(a, a, a, a, bd, conv_w, conv_w, conv_w, alog_row, dtb_row, norm_row)


def _dsw_kernel(q_ref, k_ref, v_ref, o_ref, cnt_s, *, seq):
    blk = ATT_BLOCK
    ri = lax.broadcasted_iota(jnp.int32, (blk, blk), 0)
    ci = lax.broadcasted_iota(jnp.int32, (blk, blk), 1)
    for d in range(N_COUNT_TILES):
        delta = d * blk + ri - ci
        causal = delta >= 0
        cnt = (jnp.where(causal & (delta <= 128), 1.0, 0.0)
               + jnp.where(causal & (delta <= 512) & ((delta & 3) == 0), 1.0, 0.0)
               + jnp.where(causal & ((delta & 15) == 0), 1.0, 0.0))
        cnt_s[d] = cnt.astype(F32)
    left = ci < DSW_HEAD_DIM

    def q_block(qi, carry):
        qrows = pl.ds(pl.multiple_of(qi * blk, blk), blk)
        q = q_ref[qrows, :].astype(F32)
        qh = (jnp.where(left, q, 0.0).astype(BF16), jnp.where(left, 0.0, q).astype(BF16))

        def k_block(kj, st):
            krows = pl.ds(pl.multiple_of(kj * blk, blk), blk)
            kk = k_ref[krows, :]
            vv = v_ref[krows, :]
            cnt = cnt_s[jnp.minimum(qi - kj, N_COUNT_TILES - 1)]
            new = []
            for h in range(2):
                m_old, l_old, acc = st[h]
                s = jnp.where(cnt > 0.0, _dot_nt(qh[h], kk), -jnp.inf)
                m_new = jnp.maximum(m_old, jnp.max(s, axis=-1, keepdims=True))
                p = jnp.exp(s - m_new) * cnt
                alpha = jnp.exp(m_old - m_new)
                l_new = alpha * l_old + jnp.sum(p, axis=-1, keepdims=True)
                acc = alpha * acc + _dot(p, vv)
                new.append((m_new, l_new, acc))
            return tuple(new)

        init = tuple((jnp.full((blk, 1), -jnp.inf, F32), jnp.zeros((blk, 1), F32),
                      jnp.zeros((blk, LANES), F32)) for _ in range(2))
        st = lax.fori_loop(0, qi + 1, k_block, init)
        o = jnp.where(left, st[0][2] / st[0][1], st[1][2] / st[1][1])
        o_ref[qrows, :] = o.astype(o_ref.dtype)
        return carry

    lax.fori_loop(0, seq // blk, q_block, 0)


def _dsw(bc):
    b, t, _ = bc.shape
    pairs = DSW_W // LANES
    spec = lambda off: pl.BlockSpec((None, t, LANES), lambda i, p: (i, 0, off + p))
    return pl.pallas_call(
        functools.partial(_dsw_kernel, seq=t),
        grid=(b, pairs),
        in_specs=[spec(0), spec(pairs), spec(2 * pairs)],
        out_specs=pl.BlockSpec((None, t, LANES), lambda i, p: (i, 0, p)),
        out_shape=jax.ShapeDtypeStruct((b, t, DSW_W), BF16),
        scratch_shapes=[pltpu.VMEM((N_COUNT_TILES, ATT_BLOCK, ATT_BLOCK), F32)],
        compiler_params=_params("arbitrary", "arbitrary"),
        name="dsw",
    )(bc, bc, bc)


def _diff_kernel(q_ref, k_ref, v_ref, lam_ref, laminit_ref, nw_ref, o_ref, *, seq):
    blk = ATT_BLOCK
    ri = lax.broadcasted_iota(jnp.int32, (blk, blk), 0)
    ci = lax.broadcasted_iota(jnp.int32, (blk, blk), 1)
    left = ci < DIFF_V_DIM
    lv = lam_ref[...]
    lam_init = laminit_ref[...]
    lam = (jnp.exp(jnp.sum(lv[0:1, :] * lv[1:2, :], keepdims=True))
           - jnp.exp(jnp.sum(lv[2:3, :] * lv[3:4, :], keepdims=True)) + lam_init)
    nw = nw_ref[...]

    def q_block(qi, carry):
        qrows = pl.ds(pl.multiple_of(qi * blk, blk), blk)
        q = q_ref[qrows, :].astype(F32)
        qm = [jnp.where((ci >= j * DIFF_QK_DIM) & (ci < (j + 1) * DIFF_QK_DIM), q, 0.0).astype(BF16)
              for j in range(4)]

        def k_block(kj, st):
            krows = pl.ds(pl.multiple_of(kj * blk, blk), blk)
            kk = k_ref[krows, :]
            vv = v_ref[krows, :]
            valid = (qi - kj) * blk + ri >= ci
            new = []
            for j in range(4):
                m_old, l_old, acc = st[j]
                s = jnp.where(valid, _dot_nt(qm[j], kk), -jnp.inf)
                m_new = jnp.maximum(m_old, jnp.max(s, axis=-1, keepdims=True))
                p = jnp.exp(s - m_new)
                alpha = jnp.exp(m_old - m_new)
                l_new = alpha * l_old + jnp.sum(p, axis=-1, keepdims=True)
                acc = alpha * acc + _dot(p, vv)
                new.append((m_new, l_new, acc))
            return tuple(new)

        init = tuple((jnp.full((blk, 1), -jnp.inf, F32), jnp.zeros((blk, 1), F32),
                      jnp.zeros((blk, LANES), F32)) for _ in range(4))
        st = lax.fori_loop(0, qi + 1, k_block, init)
        att = [st[j][2] / st[j][1] for j in range(4)]
        o = jnp.where(left, att[0] - lam * att[1], att[2] - lam * att[3])
        sq = o * o
        ss_left = jnp.sum(jnp.where(left, sq, 0.0), axis=-1, keepdims=True)
        ss_right = jnp.sum(jnp.where(left, 0.0, sq), axis=-1, keepdims=True)
        ms = jnp.where(left, ss_left, ss_right) * (1.0 / DIFF_V_DIM)
        o = o * lax.rsqrt(ms + EPS) * nw * (1.0 - lam_init)
        o_ref[qrows, :] = o.astype(o_ref.dtype)
        return carry

    lax.fori_loop(0, seq // blk, q_block, 0)


def _diff(bc, lam_vecs, lam_init, norm_row, layer):
    b, t, _ = bc.shape
    pairs = DIFF_W // LANES
    base = 3 * DSW_W // LANES
    spec = lambda off: pl.BlockSpec((None, t, LANES), lambda i, p: (i, 0, base + off + p))
    return pl.pallas_call(
        functools.partial(_diff_kernel, seq=t),
        grid=(b, pairs),
        in_specs=[spec(0), spec(pairs), spec(2 * pairs),
                  pl.BlockSpec((None, 4, DIFF_QK_DIM), lambda i, p: (layer, 0, 0)),
                  pl.BlockSpec((None, 1, 1), lambda i, p: (layer, 0, 0)),
                  pl.BlockSpec((None, 1, LANES), lambda i, p: (layer, 0, 0))],
        out_specs=pl.BlockSpec((None, t, LANES), lambda i, p: (i, 0, p)),
        out_shape=jax.ShapeDtypeStruct((b, t, DIFF_W), BF16),
        compiler_params=_params("arbitrary", "arbitrary"),
        name="diff",
    )(bc, bc, bc, lam_vecs, lam_init, norm_row)


def _layer_norm(h, g, b):
    mu = jnp.mean(h, axis=-1, keepdims=True)
    hc = h - mu
    var = jnp.mean(hc * hc, axis=-1, keepdims=True)
    return hc * lax.rsqrt(var + EPS) * g + b


def _out_ln_kernel(x_ref, ya_ref, yb_ref, yc_ref, w_ref, g_ref, b_ref, o_ref, *, alpha):
    y = jnp.dot(ya_ref[...], w_ref[0:GDN_W, :], preferred_element_type=F32)
    y = y + jnp.dot(yb_ref[...], w_ref[GDN_W:GDN_W + DSW_W, :], preferred_element_type=F32)
    y = y + jnp.dot(yc_ref[...], w_ref[GDN_W + DSW_W:, :], preferred_element_type=F32)
    o_ref[...] = _layer_norm(alpha * x_ref[...] + y, g_ref[...], b_ref[...])


def _out_ln(x, ya, yb, yc, w, g, b, layer, alpha):
    n, d = x.shape
    tm = min(ROW_TILE, n)
    rows = lambda width: pl.BlockSpec((tm, width), lambda i: (i, 0))
    vec = pl.BlockSpec((None, 1, d), lambda i: (layer, 0, 0))
    return pl.pallas_call(
        functools.partial(_out_ln_kernel, alpha=alpha),
        grid=(n // tm,),
        in_specs=[rows(d), rows(GDN_W), rows(DSW_W), rows(DIFF_W),
                  pl.BlockSpec((None, d, d), lambda i: (layer, 0, 0)), vec, vec],
        out_specs=rows(d),
        out_shape=jax.ShapeDtypeStruct((n, d), F32),
        compiler_params=_params("arbitrary"),
        name="out_ln",
    )(x, ya, yb, yc, w, g, b)


def _ffn_up_kernel(x_ref, wg_ref, wv_ref, cg_ref, cv_ref, h_ref, work_g, work_v, carry_g, carry_v,
                   *, tiles_per_seq, col_chunk):
    tm = x_ref.shape[0]
    width = h_ref.shape[1]

    @pl.when(pl.program_id(0) % tiles_per_seq == 0)
    def _():
        carry_g[...] = jnp.zeros(carry_g.shape, F32)
        carry_v[...] = jnp.zeros(carry_v.shape, F32)

    x = x_ref[...].astype(BF16)
    first = SUBLANES - (FFN_CONV - 1)

    def conv(u, w_ref, cw_ref, work, carry, cols):
        work[0:SUBLANES, :] = carry[:, cols]
        work[SUBLANES:SUBLANES + tm, :] = u
        carry[:, cols] = work[tm:tm + SUBLANES, :]
        cw = cw_ref[:, cols]
        acc = u * cw[FFN_CONV - 1:FFN_CONV, :]
        for j in range(FFN_CONV - 1):
            acc = acc + work[first + j:first + j + tm, :] * cw[j:j + 1, :]
        return acc

    for c in range(0, width, col_chunk):
        cols = slice(c, c + col_chunk)
        gate = conv(jnp.dot(x, wg_ref[:, cols], preferred_element_type=F32), wg_ref, cg_ref,
                    work_g, carry_g, cols)
        val = conv(jnp.dot(x, wv_ref[:, cols], preferred_element_type=F32), wv_ref, cv_ref,
                   work_v, carry_v, cols)
        h_ref[:, cols] = (gate * _sigmoid(gate) * val).astype(h_ref.dtype)


def _ffn_up(x, wg, wv, cg, cv, layer, seq):
    n, d = x.shape
    width = wg.shape[-1]
    tm = min(ROW_TILE, seq)
    col_chunk = 256
    wspec = pl.BlockSpec((None, d, width), lambda i: (layer, 0, 0))
    cspec = pl.BlockSpec((None, FFN_CONV, width), lambda i: (layer, 0, 0))
    return pl.pallas_call(
        functools.partial(_ffn_up_kernel, tiles_per_seq=seq // tm, col_chunk=col_chunk),
        grid=(n // tm,),
        in_specs=[pl.BlockSpec((tm, d), lambda i: (i, 0)), wspec, wspec, cspec, cspec],
        out_specs=pl.BlockSpec((tm, width), lambda i: (i, 0)),
        out_shape=jax.ShapeDtypeStruct((n, width), BF16),
        scratch_shapes=[pltpu.VMEM((tm + SUBLANES, col_chunk), F32)] * 2
        + [pltpu.VMEM((SUBLANES, width), F32)] * 2,
        compiler_params=_params("arbitrary"),
        name="ffn_up",
    )(x, wg, wv, cg, cv)


def _ffn_down_kernel(x_ref, h_ref, w_ref, g_ref, b_ref, o_ref, *, alpha):
    f = jnp.dot(h_ref[...], w_ref[...], preferred_element_type=F32)
    o_ref[...] = _layer_norm(alpha * x_ref[...] + f, g_ref[...], b_ref[...])


def _ffn_down(x, h, w, g, b, layer, alpha):
    n, d = x.shape
    width = h.shape[1]
    tm = min(ROW_TILE, n)
    vec = pl.BlockSpec((None, 1, d), lambda i: (layer, 0, 0))
    return pl.pallas_call(
        functools.partial(_ffn_down_kernel, alpha=alpha),
        grid=(n // tm,),
        in_specs=[pl.BlockSpec((tm, d), lambda i: (i, 0)), pl.BlockSpec((tm, width), lambda i: (i, 0)),
                  pl.BlockSpec((None, width, d), lambda i: (layer, 0, 0)), vec, vec],
        out_specs=pl.BlockSpec((tm, d), lambda i: (i, 0)),
        out_shape=jax.ShapeDtypeStruct((n, d), F32),
        compiler_params=_params("arbitrary"),
        name="ffn_down",
    )(x, h, w, g, b)


def _pad_last(a, width):
    return jnp.pad(a, [(0, 0)] * (a.ndim - 1) + [(0, width - a.shape[-1])])


def kernel(x, w_in, gdn_conv, gdn_a_log, gdn_dt_bias, gdn_norm, diff_lambda, diff_norm, w_out,
           ln1_g, ln1_b, w_up, ffn_conv, w_down, ln2_g, ln2_b):
    batch, seq, d = x.shape
    depth = w_in.shape[0]
    d_ff = w_down.shape[1]
    ff_pad = -(-d_ff // LANES) * LANES
    alpha = (2 * depth) ** 0.25

    bd0 = A_W
    bc0 = A_W + 2 * GDN_HEADS
    wa = w_in[:, :, :A_W].astype(BF16)
    wbd = _pad_last(w_in[:, :, bd0:bc0], LANES).astype(BF16)
    wbc = w_in[:, :, bc0:].astype(BF16)
    scale = jnp.concatenate([
        jnp.full((DSW_W,), DSW_HEAD_DIM ** -0.5, F32), jnp.ones((2 * DSW_W,), F32),
        jnp.full((DIFF_QK_W,), DIFF_QK_DIM ** -0.5, F32), jnp.ones((DIFF_QK_W + DIFF_W,), F32)])[None, :]
    lane_row = lambda v, off: jnp.pad(v, ((0, 0), (off, LANES - off - v.shape[1])))[:, None, :]
    alog_row = lane_row(gdn_a_log, GDN_HEADS)
    dtb_row = lane_row(gdn_dt_bias, GDN_HEADS)
    gdn_norm_row = gdn_norm[:, None, :]
    diff_norm_row = jnp.tile(diff_norm, (1, LANES // DIFF_V_DIM))[:, None, :]
    lam_init = jnp.asarray([0.8 - 0.6 * math.exp(-0.3 * l) for l in range(depth)], F32)[:, None, None]
    w_out_b = w_out.astype(BF16)
    wg = _pad_last(w_up[:, :, :d_ff], ff_pad).astype(BF16)
    wv = _pad_last(w_up[:, :, d_ff:], ff_pad).astype(BF16)
    cg = _pad_last(ffn_conv[:, :, :d_ff], ff_pad)
    cv = _pad_last(ffn_conv[:, :, d_ff:], ff_pad)
    w_down_b = jnp.pad(w_down, ((0, 0), (0, ff_pad - d_ff), (0, 0))).astype(BF16)
    vec3 = lambda v: v[:, None, :]

    xf = x.reshape(batch * seq, d)
    for l in range(depth):
        a, bc, bd = _in_proj(xf, wa, wbc, wbd, scale, l)
        bc3 = bc.reshape(batch, seq, BC_W)
        ya = _gdn(a.reshape(batch, seq, A_W), bd.reshape(batch, seq, LANES), gdn_conv,
                  alog_row, dtb_row, gdn_norm_row, l)
        yb = _dsw(bc3)
        yc = _diff(bc3, diff_lambda, lam_init, diff_norm_row, l)
        n = batch * seq
        x1 = _out_ln(xf, ya.reshape(n, GDN_W), yb.reshape(n, DSW_W), yc.reshape(n, DIFF_W),
                     w_out_b, vec3(ln1_g), vec3(ln1_b), l, alpha)
        h = _ffn_up(x1, wg, wv, cg, cv, l, seq)
        xf = _ffn_down(x1, h, w_down_b, vec3(ln2_g), vec3(ln2_b), l, alpha)
    return xf.reshape(batch, seq, d)
```

```python
import functools
import math

import jax
import jax.numpy as jnp
from jax import lax
from jax.experimental import pallas as pl
from jax.experimental.pallas import tpu as pltpu

F32 = jnp.float32
BF16 = jnp.bfloat16

LANES = 128
SUBLANES = 8
VMEM_LIMIT = 48 * 1024 * 1024

GDN_HEADS = 4
GDN_HEAD_DIM = 128
GDN_CONV = 4
DSW_HEADS = 4
DSW_HEAD_DIM = 64
DIFF_HEADS = 4
DIFF_QK_DIM = 32
DIFF_V_DIM = 64
FFN_CONV = 3
EPS = 1e-5

GDN_W = GDN_HEADS * GDN_HEAD_DIM
DSW_W = DSW_HEADS * DSW_HEAD_DIM
DIFF_W = DIFF_HEADS * DIFF_V_DIM
DIFF_QK_W = DIFF_HEADS * 2 * DIFF_QK_DIM
A_W = 4 * GDN_W
BC_W = 3 * DSW_W + 2 * DIFF_QK_W + DIFF_W

ROW_TILE = 512
ATT_Q_TILE = 256
ATT_K_TILE = 512
GDN_CHUNK = 128
GDN_GROUP = 4
DSW_MAX_BAND = 512
DSW_FAR_TILE = -(-(DSW_MAX_BAND + ATT_K_TILE) // ATT_Q_TILE)


def _dot(a, b):
    return jnp.dot(a.astype(BF16), b.astype(BF16), preferred_element_type=F32)


def _dot_nt(a, b):
    return lax.dot_general(a.astype(BF16), b.astype(BF16), (((1,), (1,)), ((), ())),
                           preferred_element_type=F32)


def _dot_exact(a, b):
    return jnp.dot(a, b, preferred_element_type=F32, precision=lax.Precision.HIGHEST)


def _split_bf16(a):
    hi = a.astype(BF16)
    return hi, (a - hi.astype(F32)).astype(BF16)


def _dot_split(a, b):
    ah, al = _split_bf16(a)
    bh, bl = _split_bf16(b)
    return jnp.dot(jnp.concatenate([ah, al, ah], axis=1), jnp.concatenate([bh, bh, bl], axis=0),
                   preferred_element_type=F32)


def _sigmoid(x):
    return 1.0 / (1.0 + jnp.exp(-x))


def _params(*sem):
    return pltpu.CompilerParams(dimension_semantics=sem, vmem_limit_bytes=VMEM_LIMIT)


def _in_proj_kernel(x_ref, wa_ref, wbc_ref, wbd_ref, scale_ref, a_ref, bc_ref, bd_ref, *, col_chunk):
    x = x_ref[...].astype(BF16)
    for c in range(0, A_W, col_chunk):
        a_ref[:, c:c + col_chunk] = jnp.dot(x, wa_ref[:, c:c + col_chunk], preferred_element_type=F32)
    for c in range(0, BC_W, col_chunk):
        r = jnp.dot(x, wbc_ref[:, c:c + col_chunk], preferred_element_type=F32)
        bc_ref[:, c:c + col_chunk] = (r * scale_ref[:, c:c + col_chunk]).astype(BF16)
    bd_ref[...] = jnp.dot(x, wbd_ref[...], preferred_element_type=F32)


def _in_proj(x, wa, wbc, wbd, scale, layer):
    n, d = x.shape
    tm = min(ROW_TILE, n)
    return pl.pallas_call(
        functools.partial(_in_proj_kernel, col_chunk=512),
        grid=(n // tm,),
        in_specs=[
            pl.BlockSpec((tm, d), lambda i: (i, 0)),
            pl.BlockSpec((None, d, A_W), lambda i: (layer, 0, 0)),
            pl.BlockSpec((None, d, BC_W), lambda i: (layer, 0, 0)),
            pl.BlockSpec((None, d, LANES), lambda i: (layer, 0, 0)),
            pl.BlockSpec((1, BC_W), lambda i: (0, 0)),
        ],
        out_specs=[
            pl.BlockSpec((tm, A_W), lambda i: (i, 0)),
            pl.BlockSpec((tm, BC_W), lambda i: (i, 0)),
            pl.BlockSpec((tm, LANES), lambda i: (i, 0)),
        ],
        out_shape=[
            jax.ShapeDtypeStruct((n, A_W), F32),
            jax.ShapeDtypeStruct((n, BC_W), BF16),
            jax.ShapeDtypeStruct((n, LANES), F32),
        ],
        compiler_params=_params("arbitrary"),
        name="in_proj",
    )(x, wa, wbc, wbd, scale)


def _gdn_kernel(q_ref, k_ref, v_ref, z_ref, bd_ref, cq_ref, ck_ref, cv_ref, alog_ref, dtb_ref, nw_ref,
                y_ref,
                pad_s, q_s, k_s, v_s, beta_s, g_s, u_s, w_s, qk_s, qd_s, ktt_s, gl_s, *, seq):
    C = GDN_CHUNK
    n_chunks = seq // C
    head = pl.program_id(1)

    def conv_silu(x_ref, cw_ref):
        pad_s[0:SUBLANES, :] = jnp.zeros((SUBLANES, LANES), F32)
        pad_s[SUBLANES:SUBLANES + seq, :] = x_ref[...]
        cw = cw_ref[...]
        first = SUBLANES - (GDN_CONV - 1)
        acc = pad_s[first:first + seq, :] * cw[0:1, :]
        for j in range(1, GDN_CONV):
            acc = acc + pad_s[first + j:first + j + seq, :] * cw[j:j + 1, :]
        return acc * _sigmoid(acc)

    q = conv_silu(q_ref, cq_ref)
    q_s[...] = q * lax.rsqrt(jnp.sum(q * q, axis=-1, keepdims=True) + 1e-6) * (GDN_HEAD_DIM ** -0.5)
    k = conv_silu(k_ref, ck_ref)
    k_s[...] = k * lax.rsqrt(jnp.sum(k * k, axis=-1, keepdims=True) + 1e-6)
    v_s[...] = conv_silu(v_ref, cv_ref)

    bd = bd_ref[...]
    lane = lax.broadcasted_iota(jnp.int32, (seq, LANES), 1)
    beta_all = _sigmoid(bd)
    xg = bd + dtb_ref[...]
    softplus = jnp.maximum(xg, 0.0) + jnp.log(1.0 + jnp.exp(-jnp.abs(xg)))
    g_all = -jnp.exp(alog_ref[...]) * softplus
    beta = jnp.sum(jnp.where(lane == head, beta_all, 0.0), axis=-1, keepdims=True)
    g = jnp.sum(jnp.where(lane == head + GDN_HEADS, g_all, 0.0), axis=-1, keepdims=True)
    beta_s[...] = jnp.broadcast_to(beta, (seq, LANES))
    g_s[...] = jnp.broadcast_to(g, (seq, LANES))

    ri = lax.broadcasted_iota(jnp.int32, (C, C), 0)
    ci = lax.broadcasted_iota(jnp.int32, (C, C), 1)
    lower_incl = ri >= ci
    strict = ri > ci
    tri = jnp.where(lower_incl, 1.0, 0.0).astype(F32)

    def chunk_group(gi, carry):
        grp = range(GDN_GROUP)
        rows = [pl.ds(pl.multiple_of((gi * GDN_GROUP + j) * C, C), C) for j in grp]
        qc = [q_s[r, :] for r in rows]
        kc = [k_s[r, :] for r in rows]
        vc = [v_s[r, :] for r in rows]
        bb = [beta_s[r, :] for r in rows]
        gg = [g_s[r, :] for r in rows]
        cum = [_dot_exact(tri, g) for g in gg]
        decay = [jnp.exp(jnp.where(lower_incl, c - c.T, -jnp.inf)) for c in cum]
        kb = [k * b for k, b in zip(kc, bb)]
        m = [jnp.where(strict, _dot_nt(x, k) * d, 0.0) for x, k, d in zip(kb, kc, decay)]
        nmat = [-x for x in m]
        p = [_dot_split(x, x) for x in m]
        span = 2
        while True:
            nmat = [n + x + _dot_split(n, x) for n, x in zip(nmat, p)]
            span *= 2
            if span >= C:
                break
            p = [_dot_split(x, x) for x in p]
        ecum = [jnp.exp(c) for c in cum]
        vb = [v * b for v, b in zip(vc, bb)]
        kbe = [x * e for x, e in zip(kb, ecum)]
        u = [x + _dot(n, x) for n, x in zip(nmat, vb)]
        w = [x + _dot(n, x) for n, x in zip(nmat, kbe)]
        qk = [jnp.where(lower_incl, _dot_nt(q, k) * d, 0.0) for q, k, d in zip(qc, kc, decay)]
        qd = [q * e for q, e in zip(qc, ecum)]
        cum_last = [c[C - 1:C, :] for c in cum]
        ktt = [(k * jnp.exp(cl - c)).T for k, cl, c in zip(kc, cum_last, cum)]
        for j in grp:
            u_s[rows[j], :] = u[j]
            w_s[rows[j], :] = w[j]
            qk_s[rows[j], :] = qk[j]
            qd_s[rows[j], :] = qd[j]
            ktt_s[rows[j], :] = ktt[j]
            gl_s[pl.ds(pl.multiple_of((gi * GDN_GROUP + j) * SUBLANES, SUBLANES), SUBLANES), :] = (
                jnp.broadcast_to(jnp.exp(cum_last[j]), (SUBLANES, LANES)))
        return carry

    lax.fori_loop(0, n_chunks // GDN_GROUP, chunk_group, 0)

    nw = nw_ref[...]

    def scan_step(c, s):
        rows = pl.ds(pl.multiple_of(c * C, C), C)
        v_new = u_s[rows, :] - _dot(w_s[rows, :], s)
        o = _dot(qd_s[rows, :], s) + _dot(qk_s[rows, :], v_new)
        gl = gl_s[pl.ds(pl.multiple_of(c * SUBLANES, SUBLANES), 1), :]
        s = s * gl + _dot(ktt_s[rows, :], v_new)
        z = z_ref[rows, :]
        o = o * lax.rsqrt(jnp.mean(o * o, axis=-1, keepdims=True) + EPS) * nw
        y_ref[rows, :] = (o * (z * _sigmoid(z))).astype(y_ref.dtype)
        return s

    lax.fori_loop(0, n_chunks, scan_step, jnp.zeros((GDN_HEAD_DIM, GDN_HEAD_DIM), F32))


def _gdn(a, bd, conv_w, alog_row, dtb_row, norm_row, layer):
    b, t, _ = a.shape
    tok = lambda off: pl.BlockSpec((None, t, LANES), lambda i, h: (i, 0, off + h))
    cw = lambda off: pl.BlockSpec((None, GDN_CONV, LANES), lambda i, h: (layer, 0, off + h))
    row = pl.BlockSpec((None, 1, LANES), lambda i, h: (layer, 0, 0))
    seq_buf = pltpu.VMEM((t, LANES), F32)
    return pl.pallas_call(
        functools.partial(_gdn_kernel, seq=t),
        grid=(b, GDN_HEADS),
        in_specs=[tok(0), tok(GDN_HEADS), tok(2 * GDN_HEADS), tok(3 * GDN_HEADS),
                  pl.BlockSpec((None, t, LANES), lambda i, h: (i, 0, 0)),
                  cw(0), cw(GDN_HEADS), cw(2 * GDN_HEADS), row, row, row],
        out_specs=pl.BlockSpec((None, t, LANES), lambda i, h: (i, 0, h)),
        out_shape=jax.ShapeDtypeStruct((b, t, GDN_W), BF16),
        scratch_shapes=[pltpu.VMEM((t + SUBLANES, LANES), F32)] + [seq_buf] * 10
        + [pltpu.VMEM((t // GDN_CHUNK * SUBLANES, LANES), F32)],
        compiler_params=_params("arbitrary", "arbitrary"),
        name="gdn",
    )(a, a, a, a, bd, conv_w, conv_w, conv_w, alog_row, dtb_row, norm_row)


def _transpose_values(v_ref, vt_s, seq):
    for c in range(seq // LANES):
        t, off = divmod(c * LANES, ATT_K_TILE)
        vt_s[t, :, off:off + LANES] = v_ref[c * LANES:(c + 1) * LANES, :].astype(F32).T.astype(BF16)


def _softmax_init(dv, width):
    return (jnp.full((1, width), -jnp.inf, F32), jnp.zeros((1, width), F32), jnp.zeros((dv, width), F32))


def _softmax_weights(st, s, cnt):
    m_old, l_old, _ = st
    m_new = jnp.maximum(m_old, jnp.max(s, axis=0, keepdims=True))
    p = jnp.exp(s - m_new)
    if cnt is not None:
        p = p * cnt
    alpha = jnp.exp(m_old - m_new)
    l_new = alpha * l_old + jnp.sum(p, axis=0, keepdims=True)
    return m_new, l_new, alpha, p.astype(BF16)


def _dsw_kernel(q_ref, k_ref, v_ref, o_ref, vt_s, cnt_s, bias_s, *, seq):
    tq, tk = ATT_Q_TILE, ATT_K_TILE
    heads = LANES // DSW_HEAD_DIM
    _transpose_values(v_ref, vt_s, seq)

    @pl.when((pl.program_id(0) == 0) & (pl.program_id(1) == 0))
    def _():
        kr = lax.broadcasted_iota(jnp.int32, (tk, tq), 0)
        qc = lax.broadcasted_iota(jnp.int32, (tk, tq), 1)
        for d in range(DSW_FAR_TILE + 1):
            delta = d * tq + qc - kr
            causal = delta >= 0
            cnt = (jnp.where(causal & (delta <= 128), 1.0, 0.0)
                   + jnp.where(causal & (delta <= DSW_MAX_BAND) & ((delta & 3) == 0), 1.0, 0.0)
                   + jnp.where(causal & ((delta & 15) == 0), 1.0, 0.0))
            cnt_s[d] = cnt.astype(F32)
            bias_s[d] = jnp.where(cnt > 0.0, 0.0, -jnp.inf).astype(F32)

    lane = lax.broadcasted_iota(jnp.int32, (tq, LANES), 1)

    def q_tile(qi, carry):
        qrows = pl.ds(pl.multiple_of(qi * tq, tq), tq)
        q = q_ref[qrows, :].astype(F32)
        qstack = jnp.concatenate(
            [jnp.where((lane >= h * DSW_HEAD_DIM) & (lane < (h + 1) * DSW_HEAD_DIM), q, 0.0).astype(BF16)
             for h in range(heads)], axis=0)

        def k_tile(kj, st):
            kk = k_ref[pl.ds(pl.multiple_of(kj * tk, tk), tk), :]
            far = jnp.minimum(qi - kj * (tk // tq), DSW_FAR_TILE)
            cnt = cnt_s[far]
            bias = bias_s[far]
            s = _dot_nt(kk, qstack)
            new = []
            for h in range(heads):
                m, l, alpha, p = _softmax_weights(st[h], s[:, h * tq:(h + 1) * tq] + bias, cnt)
                pv = jnp.dot(vt_s[kj, h * DSW_HEAD_DIM:(h + 1) * DSW_HEAD_DIM, :], p,
                             preferred_element_type=F32)
                new.append((m, l, alpha * st[h][2] + pv))
            return tuple(new)

        n_k = (qi * tq) // tk + 1
        st = lax.fori_loop(0, n_k, k_tile, tuple(_softmax_init(DSW_HEAD_DIM, tq) for _ in range(heads)))
        o = jnp.concatenate([st[h][2] / st[h][1] for h in range(heads)], axis=0)
        o_ref[qrows, :] = o.T.astype(o_ref.dtype)
        return carry

    lax.fori_loop(0, seq // tq, q_tile, 0)


def _dsw(bc):
    b, t, _ = bc.shape
    pairs = DSW_W // LANES
    spec = lambda off: pl.BlockSpec((None, t, LANES), lambda i, p: (i, 0, off + p))
    return pl.pallas_call(
        functools.partial(_dsw_kernel, seq=t),
        grid=(b, pairs),
        in_specs=[spec(0), spec(pairs), spec(2 * pairs)],
        out_specs=pl.BlockSpec((None, t, LANES), lambda i, p: (i, 0, p)),
        out_shape=jax.ShapeDtypeStruct((b, t, DSW_W), BF16),
        scratch_shapes=[pltpu.VMEM((t // ATT_K_TILE, LANES, ATT_K_TILE), BF16)]
        + [pltpu.VMEM((DSW_FAR_TILE + 1, ATT_K_TILE, ATT_Q_TILE), F32)] * 2,
        compiler_params=_params("arbitrary", "arbitrary"),
        name="dsw",
    )(bc, bc, bc)


def _diff_kernel(q_ref, k_ref, v_ref, lam_ref, laminit_ref, nw_ref, o_ref, vt_s, *, seq):
    tq, tk = ATT_Q_TILE, ATT_K_TILE
    heads = LANES // DIFF_V_DIM
    _transpose_values(v_ref, vt_s, seq)
    kr = lax.broadcasted_iota(jnp.int32, (tk, tq), 0)
    qc = lax.broadcasted_iota(jnp.int32, (tk, tq), 1)
    lane = lax.broadcasted_iota(jnp.int32, (tq, LANES), 1)
    lv = lam_ref[...]
    lam_init = laminit_ref[...]
    lam = (jnp.exp(jnp.sum(lv[0:1, :] * lv[1:2, :], keepdims=True))
           - jnp.exp(jnp.sum(lv[2:3, :] * lv[3:4, :], keepdims=True)) + lam_init)
    nw = nw_ref[...]

    def q_tile(qi, carry):
        qrows = pl.ds(pl.multiple_of(qi * tq, tq), tq)
        q = q_ref[qrows, :].astype(F32)
        qstack = jnp.concatenate(
            [jnp.where((lane >= j * DIFF_QK_DIM) & (lane < (j + 1) * DIFF_QK_DIM), q, 0.0).astype(BF16)
             for j in range(2 * heads)], axis=0)

        def k_tile(kj, st, valid):
            kk = k_ref[pl.ds(pl.multiple_of(kj * tk, tk), tk), :]
            s = _dot_nt(kk, qstack)
            new = []
            for h in range(heads):
                ml, ps, alphas = [], [], []
                for mp in range(2):
                    j = 2 * h + mp
                    sj = s[:, j * tq:(j + 1) * tq]
                    if valid is not None:
                        sj = jnp.where(valid, sj, -jnp.inf)
                    m, l, alpha, p = _softmax_weights(st[j], sj, None)
                    ml.append((m, l))
                    alphas.append(alpha)
                    ps.append(p)
                pv = jnp.dot(vt_s[kj, h * DIFF_V_DIM:(h + 1) * DIFF_V_DIM, :],
                             jnp.concatenate(ps, axis=1), preferred_element_type=F32)
                for mp in range(2):
                    j = 2 * h + mp
                    new.append(ml[mp] + (alphas[mp] * st[j][2] + pv[:, mp * tq:(mp + 1) * tq],))
            return tuple(new)

        st = tuple(_softmax_init(DIFF_V_DIM, tq) for _ in range(2 * heads))
        n_full = (qi * tq) // tk
        st = lax.fori_loop(0, n_full, functools.partial(k_tile, valid=None), st)
        st = k_tile(n_full, st, kr + (n_full * tk - qi * tq) <= qc)
        halves = []
        for h in range(heads):
            o = st[2 * h][2] / st[2 * h][1] - lam * (st[2 * h + 1][2] / st[2 * h + 1][1])
            halves.append(o * lax.rsqrt(jnp.mean(o * o, axis=0, keepdims=True) + EPS))
        o = jnp.concatenate(halves, axis=0).T
        o_ref[qrows, :] = (o * nw * (1.0 - lam_init)).astype(o_ref.dtype)
        return carry

    lax.fori_loop(0, seq // tq, q_tile, 0)


def _diff(bc, lam_vecs, lam_init, norm_row, layer):
    b, t, _ = bc.shape
    pairs = DIFF_W // LANES
    base = 3 * DSW_W // LANES
    spec = lambda off: pl.BlockSpec((None, t, LANES), lambda i, p: (i, 0, base + off + p))
    return pl.pallas_call(
        functools.partial(_diff_kernel, seq=t),
        grid=(b, pairs),
        in_specs=[spec(0), spec(pairs), spec(2 * pairs),
                  pl.BlockSpec((None, 4, DIFF_QK_DIM), lambda i, p: (layer, 0, 0)),
                  pl.BlockSpec((None, 1, 1), lambda i, p: (layer, 0, 0)),
                  pl.BlockSpec((None, 1, LANES), lambda i, p: (layer, 0, 0))],
        out_specs=pl.BlockSpec((None, t, LANES), lambda i, p: (i, 0, p)),
        out_shape=jax.ShapeDtypeStruct((b, t, DIFF_W), BF16),
        scratch_shapes=[pltpu.VMEM((t // ATT_K_TILE, LANES, ATT_K_TILE), BF16)],
        compiler_params=_params("arbitrary", "arbitrary"),
        name="diff",
    )(bc, bc, bc, lam_vecs, lam_init, norm_row)


def _layer_norm(h, g, b):
    mu = jnp.mean(h, axis=-1, keepdims=True)
    hc = h - mu
    var = jnp.mean(hc * hc, axis=-1, keepdims=True)
    return hc * lax.rsqrt(var + EPS) * g + b


def _out_ln_kernel(x_ref, ya_ref, yb_ref, yc_ref, w_ref, g_ref, b_ref, o_ref, *, alpha):
    y = jnp.dot(ya_ref[...], w_ref[0:GDN_W, :], preferred_element_type=F32)
    y = y + jnp.dot(yb_ref[...], w_ref[GDN_W:GDN_W + DSW_W, :], preferred_element_type=F32)
    y = y + jnp.dot(yc_ref[...], w_ref[GDN_W + DSW_W:, :], preferred_element_type=F32)
    o_ref[...] = _layer_norm(alpha * x_ref[...] + y, g_ref[...], b_ref[...])


def _out_ln(x, ya, yb, yc, w, g, b, layer, alpha):
    n, d = x.shape
    tm = min(ROW_TILE, n)
    rows = lambda width: pl.BlockSpec((tm, width), lambda i: (i, 0))
    vec = pl.BlockSpec((None, 1, d), lambda i: (layer, 0, 0))
    return pl.pallas_call(
        functools.partial(_out_ln_kernel, alpha=alpha),
        grid=(n // tm,),
        in_specs=[rows(d), rows(GDN_W), rows(DSW_W), rows(DIFF_W),
                  pl.BlockSpec((None, d, d), lambda i: (layer, 0, 0)), vec, vec],
        out_specs=rows(d),
        out_shape=jax.ShapeDtypeStruct((n, d), F32),
        compiler_params=_params("arbitrary"),
        name="out_ln",
    )(x, ya, yb, yc, w, g, b)


def _ffn_up_kernel(x_ref, wg_ref, wv_ref, cg_ref, cv_ref, h_ref, work_g, work_v, carry_g, carry_v,
                   *, tiles_per_seq, col_chunk):
    tm = x_ref.shape[0]
    width = h_ref.shape[1]

    @pl.when(pl.program_id(0) % tiles_per_seq == 0)
    def _():
        carry_g[...] = jnp.zeros(carry_g.shape, F32)
        carry_v[...] = jnp.zeros(carry_v.shape, F32)

    x = x_ref[...].astype(BF16)
    first = SUBLANES - (FFN_CONV - 1)

    def conv(u, cw_ref, work, carry, cols):
        work[0:SUBLANES, :] = carry[:, cols]
        work[SUBLANES:SUBLANES + tm, :] = u
        carry[:, cols] = work[tm:tm + SUBLANES, :]
        cw = cw_ref[:, cols]
        acc = u * cw[FFN_CONV - 1:FFN_CONV, :]
        for j in range(FFN_CONV - 1):
            acc = acc + work[first + j:first + j + tm, :] * cw[j:j + 1, :]
        return acc

    for c in range(0, width, col_chunk):
        cols = slice(c, c + col_chunk)
        gate = conv(jnp.dot(x, wg_ref[:, cols], preferred_element_type=F32), cg_ref, work_g, carry_g, cols)
        val = conv(jnp.dot(x, wv_ref[:, cols], preferred_element_type=F32), cv_ref, work_v, carry_v, cols)
        h_ref[:, cols] = (gate * _sigmoid(gate) * val).astype(h_ref.dtype)


def _ffn_up(x, wg, wv, cg, cv, layer, seq):
    n, d = x.shape
    width = wg.shape[-1]
    tm = min(ROW_TILE, seq)
    col_chunk = 256
    wspec = pl.BlockSpec((None, d, width), lambda i: (layer, 0, 0))
    cspec = pl.BlockSpec((None, FFN_CONV, width), lambda i: (layer, 0, 0))
    return pl.pallas_call(
        functools.partial(_ffn_up_kernel, tiles_per_seq=seq // tm, col_chunk=col_chunk),
        grid=(n // tm,),
        in_specs=[pl.BlockSpec((tm, d), lambda i: (i, 0)), wspec, wspec, cspec, cspec],
        out_specs=pl.BlockSpec((tm, width), lambda i: (i, 0)),
        out_shape=jax.ShapeDtypeStruct((n, width), BF16),
        scratch_shapes=[pltpu.VMEM((tm + SUBLANES, col_chunk), F32)] * 2
        + [pltpu.VMEM((SUBLANES, width), F32)] * 2,
        compiler_params=_params("arbitrary"),
        name="ffn_up",
    )(x, wg, wv, cg, cv)


def _ffn_down_kernel(x_ref, h_ref, w_ref, g_ref, b_ref, o_ref, *, alpha):
    f = jnp.dot(h_ref[...], w_ref[...], preferred_element_type=F32)
    o_ref[...] = _layer_norm(alpha * x_ref[...] + f, g_ref[...], b_ref[...])


def _ffn_down(x, h, w, g, b, layer, alpha):
    n, d = x.shape
    width = h.shape[1]
    tm = min(ROW_TILE, n)
    vec = pl.BlockSpec((None, 1, d), lambda i: (layer, 0, 0))
    return pl.pallas_call(
        functools.partial(_ffn_down_kernel, alpha=alpha),
        grid=(n // tm,),
        in_specs=[pl.BlockSpec((tm, d), lambda i: (i, 0)), pl.BlockSpec((tm, width), lambda i: (i, 0)),
                  pl.BlockSpec((None, width, d), lambda i: (layer, 0, 0)), vec, vec],
        out_specs=pl.BlockSpec((tm, d), lambda i: (i, 0)),
        out_shape=jax.ShapeDtypeStruct((n, d), F32),
        compiler_params=_params("arbitrary"),
        name="ffn_down",
    )(x, h, w, g, b)


def _pad_last(a, width):
    return jnp.pad(a, [(0, 0)] * (a.ndim - 1) + [(0, width - a.shape[-1])])


def kernel(x, w_in, gdn_conv, gdn_a_log, gdn_dt_bias, gdn_norm, diff_lambda, diff_norm, w_out,
           ln1_g, ln1_b, w_up, ffn_conv, w_down, ln2_g, ln2_b):
    batch, seq, d = x.shape
    depth = w_in.shape[0]
    d_ff = w_down.shape[1]
    ff_pad = -(-d_ff // LANES) * LANES
    alpha = (2 * depth) ** 0.25

    bd0 = A_W
    bc0 = A_W + 2 * GDN_HEADS
    wa = w_in[:, :, :A_W].astype(BF16)
    wbd = _pad_last(w_in[:, :, bd0:bc0], LANES).astype(BF16)
    wbc = w_in[:, :, bc0:].astype(BF16)
    scale = jnp.concatenate([
        jnp.full((DSW_W,), DSW_HEAD_DIM ** -0.5, F32), jnp.ones((2 * DSW_W,), F32),
        jnp.full((DIFF_QK_W,), DIFF_QK_DIM ** -0.5, F32), jnp.ones((DIFF_QK_W + DIFF_W,), F32)])[None, :]
    lane_row = lambda v, off: jnp.pad(v, ((0, 0), (off, LANES - off - v.shape[1])))[:, None, :]
    alog_row = lane_row(gdn_a_log, GDN_HEADS)
    dtb_row = lane_row(gdn_dt_bias, GDN_HEADS)
    gdn_norm_row = gdn_norm[:, None, :]
    diff_norm_row = jnp.tile(diff_norm, (1, LANES // DIFF_V_DIM))[:, None, :]
    lam_init = jnp.asarray([0.8 - 0.6 * math.exp(-0.3 * l) for l in range(depth)], F32)[:, None, None]
    w_out_b = w_out.astype(BF16)
    wg = _pad_last(w_up[:, :, :d_ff], ff_pad).astype(BF16)
    wv = _pad_last(w_up[:, :, d_ff:], ff_pad).astype(BF16)
    cg = _pad_last(ffn_conv[:, :, :d_ff], ff_pad)
    cv = _pad_last(ffn_conv[:, :, d_ff:], ff_pad)
    w_down_b = jnp.pad(w_down, ((0, 0), (0, ff_pad - d_ff), (0, 0))).astype(BF16)
    vec3 = lambda v: v[:, None, :]

    xf = x.reshape(batch * seq, d)
    for l in range(depth):
        a, bc, bd = _in_proj(xf, wa, wbc, wbd, scale, l)
        bc3 = bc.reshape(batch, seq, BC_W)
        ya = _gdn(a.reshape(batch, seq, A_W), bd.reshape(batch, seq, LANES), gdn_conv,
                  alog_row, dtb_row, gdn_norm_row, l)
        yb = _dsw(bc3)
        yc = _diff(bc3, diff_lambda, lam_init, diff_norm_row, l)
        n = batch * seq
        x1 = _out_ln(xf, ya.reshape(n, GDN_W), yb.reshape(n, DSW_W), yc.reshape(n, DIFF_W),
                     w_out_b, vec3(ln1_g), vec3(ln1_b), l, alpha)
        h = _ffn_up(x1, wg, wv, cg, cv, l, seq)
        xf = _ffn_down(x1, h, w_down_b, vec3(ln2_g), vec3(ln2_b), l, alpha)
    return xf.reshape(batch, seq, d)
```

```python
import functools
import math

import jax
import jax.numpy as jnp
from jax import lax
from jax.experimental import pallas as pl
from jax.experimental.pallas import tpu as pltpu

F32 = jnp.float32
BF16 = jnp.bfloat16

LANES = 128
SUBLANES = 8
VMEM_LIMIT = 48 * 1024 * 1024

GDN_HEADS = 4
GDN_HEAD_DIM = 128
GDN_CONV = 4
DSW_HEADS = 4
DSW_HEAD_DIM = 64
DIFF_HEADS = 4
DIFF_QK_DIM = 32
DIFF_V_DIM = 64
FFN_CONV = 3
EPS = 1e-5

GDN_W = GDN_HEADS * GDN_HEAD_DIM
DSW_W = DSW_HEADS * DSW_HEAD_DIM
DIFF_W = DIFF_HEADS * DIFF_V_DIM
DIFF_QK_W = DIFF_HEADS * 2 * DIFF_QK_DIM
A_W = 4 * GDN_W
BC_W = 3 * DSW_W + 2 * DIFF_QK_W + DIFF_W

ROW_TILE = 512
ATT_Q_TILE = 512
ATT_K_TILE = 512
GDN_CHUNK = 128
GDN_GROUP = 8
GDN_SPLIT_SPAN = 16
DSW_MAX_BAND = 512
DSW_FAR_TILE = -(-(DSW_MAX_BAND + ATT_K_TILE) // ATT_Q_TILE)


def _dot(a, b):
    return jnp.dot(a.astype(BF16), b.astype(BF16), preferred_element_type=F32)


def _dot_nt(a, b):
    return lax.dot_general(a.astype(BF16), b.astype(BF16), (((1,), (1,)), ((), ())),
                           preferred_element_type=F32)


def _split_bf16(a):
    hi = a.astype(BF16)
    return hi, (a - hi.astype(F32)).astype(BF16)


def _split3_bf16(a):
    hi, mid = _split_bf16(a)
    return hi, mid, (a - hi.astype(F32) - mid.astype(F32)).astype(BF16)


def _dot_split(a, b):
    ah, al = _split_bf16(a)
    bh, bl = _split_bf16(b)
    return jnp.dot(jnp.concatenate([ah, al, ah], axis=1), jnp.concatenate([bh, bh, bl], axis=0),
                   preferred_element_type=F32)


def _sigmoid(x):
    return 1.0 / (1.0 + jnp.exp(-x))


def _params(*sem):
    return pltpu.CompilerParams(dimension_semantics=sem, vmem_limit_bytes=VMEM_LIMIT)


def _in_proj_kernel(x_ref, wa_ref, wbc_ref, wbd_ref, scale_ref, a_ref, bc_ref, bd_ref, *, col_chunk):
    x = x_ref[...].astype(BF16)
    for c in range(0, A_W, col_chunk):
        a_ref[:, c:c + col_chunk] = jnp.dot(x, wa_ref[:, c:c + col_chunk], preferred_element_type=F32)
    for c in range(0, BC_W, col_chunk):
        r = jnp.dot(x, wbc_ref[:, c:c + col_chunk], preferred_element_type=F32)
        bc_ref[:, c:c + col_chunk] = (r * scale_ref[:, c:c + col_chunk]).astype(BF16)
    bd_ref[...] = jnp.dot(x, wbd_ref[...], preferred_element_type=F32)


def _in_proj(x, wa, wbc, wbd, scale, layer):
    n, d = x.shape
    tm = min(ROW_TILE, n)
    return pl.pallas_call(
        functools.partial(_in_proj_kernel, col_chunk=512),
        grid=(n // tm,),
        in_specs=[
            pl.BlockSpec((tm, d), lambda i: (i, 0)),
            pl.BlockSpec((None, d, A_W), lambda i: (layer, 0, 0)),
            pl.BlockSpec((None, d, BC_W), lambda i: (layer, 0, 0)),
            pl.BlockSpec((None, d, LANES), lambda i: (layer, 0, 0)),
            pl.BlockSpec((1, BC_W), lambda i: (0, 0)),
        ],
        out_specs=[
            pl.BlockSpec((tm, A_W), lambda i: (i, 0)),
            pl.BlockSpec((tm, BC_W), lambda i: (i, 0)),
            pl.BlockSpec((tm, LANES), lambda i: (i, 0)),
        ],
        out_shape=[
            jax.ShapeDtypeStruct((n, A_W), F32),
            jax.ShapeDtypeStruct((n, BC_W), BF16),
            jax.ShapeDtypeStruct((n, LANES), F32),
        ],
        compiler_params=_params("arbitrary"),
        name="in_proj",
    )(x, wa, wbc, wbd, scale)


def _gdn_kernel(q_ref, k_ref, v_ref, z_ref, bd_ref, cq_ref, ck_ref, cv_ref, alog_ref, dtb_ref, nw_ref,
                y_ref,
                pad_s, q_s, k_s, v_s, beta_s, g_s, sa_s, sb_s, oq_s, oc_s, gl_s, *, seq):
    C = GDN_CHUNK
    n_chunks = seq // C
    head = pl.program_id(1)

    def conv_silu(x_ref, cw_ref):
        pad_s[0:SUBLANES, :] = jnp.zeros((SUBLANES, LANES), F32)
        pad_s[SUBLANES:SUBLANES + seq, :] = x_ref[...]
        cw = cw_ref[...]
        first = SUBLANES - (GDN_CONV - 1)
        acc = pad_s[first:first + seq, :] * cw[0:1, :]
        for j in range(1, GDN_CONV):
            acc = acc + pad_s[first + j:first + j + seq, :] * cw[j:j + 1, :]
        return acc * _sigmoid(acc)

    q = conv_silu(q_ref, cq_ref)
    q_s[...] = q * lax.rsqrt(jnp.sum(q * q, axis=-1, keepdims=True) + 1e-6) * (GDN_HEAD_DIM ** -0.5)
    k = conv_silu(k_ref, ck_ref)
    k_s[...] = k * lax.rsqrt(jnp.sum(k * k, axis=-1, keepdims=True) + 1e-6)
    v_s[...] = conv_silu(v_ref, cv_ref)

    bd = bd_ref[...]
    lane = lax.broadcasted_iota(jnp.int32, (seq, LANES), 1)
    beta_all = _sigmoid(bd)
    xg = bd + dtb_ref[...]
    softplus = jnp.maximum(xg, 0.0) + jnp.log(1.0 + jnp.exp(-jnp.abs(xg)))
    g_all = -jnp.exp(alog_ref[...]) * softplus
    beta = jnp.sum(jnp.where(lane == head, beta_all, 0.0), axis=-1, keepdims=True)
    g = jnp.sum(jnp.where(lane == head + GDN_HEADS, g_all, 0.0), axis=-1, keepdims=True)
    beta_s[...] = jnp.broadcast_to(beta, (seq, LANES))
    g_s[...] = jnp.broadcast_to(g, (seq, LANES))

    ri = lax.broadcasted_iota(jnp.int32, (C, C), 0)
    ci = lax.broadcasted_iota(jnp.int32, (C, C), 1)
    lower_incl = ri >= ci
    strict = ri > ci
    tri = jnp.where(lower_incl, 1.0, 0.0).astype(BF16)
    tri3 = jnp.concatenate([tri, tri, tri], axis=1)

    def chunk_group(gi, carry):
        grp = range(GDN_GROUP)
        rows = [pl.ds(pl.multiple_of((gi * GDN_GROUP + j) * C, C), C) for j in grp]
        qc = [q_s[r, :] for r in rows]
        kc = [k_s[r, :] for r in rows]
        vc = [v_s[r, :] for r in rows]
        bb = [beta_s[r, :] for r in rows]
        gg = [g_s[r, :] for r in rows]
        cum = [jnp.dot(tri3, jnp.concatenate(_split3_bf16(g), axis=0), preferred_element_type=F32)
               for g in gg]
        decay = [jnp.exp(jnp.where(lower_incl, c - c.T, -jnp.inf)) for c in cum]
        kb = [k * b for k, b in zip(kc, bb)]
        scores = [_dot_nt(jnp.concatenate([x, q], axis=0), k) for x, q, k in zip(kb, qc, kc)]
        m = [jnp.where(strict, s[0:C, :] * d, 0.0) for s, d in zip(scores, decay)]
        qk = [jnp.where(lower_incl, s[C:, :] * d, 0.0) for s, d in zip(scores, decay)]
        nmat = [-x for x in m]
        p = [_dot_split(x, x) for x in m]
        span = 2
        while 2 * span < C:
            dot = _dot_split if span < GDN_SPLIT_SPAN else _dot
            both = [dot(jnp.concatenate([n, x], axis=0), x) for n, x in zip(nmat, p)]
            nmat = [n + x + b[0:C, :] for n, x, b in zip(nmat, p, both)]
            p = [b[C:, :] for b in both]
            span *= 2
        nmat = [n + x + _dot(n, x) for n, x in zip(nmat, p)]
        ecum = [jnp.exp(c) for c in cum]
        vb = [v * b for v, b in zip(vc, bb)]
        kbe = [x * e for x, e in zip(kb, ecum)]
        wu = [jnp.concatenate([x, y], axis=1) for x, y in zip(kbe, vb)]
        wu = [x + _dot(n, x) for n, x in zip(nmat, wu)]
        qd = [q * e for q, e in zip(qc, ecum)]
        cum_last = [c[C - 1:C, :] for c in cum]
        ktt = [(k * jnp.exp(cl - c)).T for k, cl, c in zip(kc, cum_last, cum)]
        prod = [_dot(jnp.concatenate([kt, a], axis=0), x) for kt, a, x in zip(ktt, qk, wu)]
        for j in grp:
            sa_s[rows[j], :] = -prod[j][0:C, 0:C]
            sb_s[rows[j], :] = prod[j][0:C, C:]
            oq_s[rows[j], :] = qd[j] - prod[j][C:, 0:C]
            oc_s[rows[j], :] = prod[j][C:, C:]
            gl_s[pl.ds(pl.multiple_of((gi * GDN_GROUP + j) * SUBLANES, SUBLANES), SUBLANES), :] = (
                jnp.broadcast_to(jnp.exp(cum_last[j]), (SUBLANES, LANES)))
        return carry

    lax.fori_loop(0, n_chunks // GDN_GROUP, chunk_group, 0)

    nw = nw_ref[...]

    def scan_step(c, s):
        rows = pl.ds(pl.multiple_of(c * C, C), C)
        o = _dot(oq_s[rows, :], s) + oc_s[rows, :]
        gl = gl_s[pl.ds(pl.multiple_of(c * SUBLANES, SUBLANES), 1), :]
        s = s * gl + (_dot(sa_s[rows, :], s) + sb_s[rows, :])
        z = z_ref[rows, :]
        o = o * lax.rsqrt(jnp.mean(o * o, axis=-1, keepdims=True) + EPS) * nw
        y_ref[rows, :] = (o * (z * _sigmoid(z))).astype(y_ref.dtype)
        return s

    lax.fori_loop(0, n_chunks, scan_step, jnp.zeros((GDN_HEAD_DIM, GDN_HEAD_DIM), F32))


def _gdn(a, bd, conv_w, alog_row, dtb_row, norm_row, layer):
    b, t, _ = a.shape
    tok = lambda off: pl.BlockSpec((None, t, LANES), lambda i, h: (i, 0, off + h))
    cw = lambda off: pl.BlockSpec((None, GDN_CONV, LANES), lambda i, h: (layer, 0, off + h))
    row = pl.BlockSpec((None, 1, LANES), lambda i, h: (layer, 0, 0))
    seq_buf = pltpu.VMEM((t, LANES), F32)
    return pl.pallas_call(
        functools.partial(_gdn_kernel, seq=t),
        grid=(b, GDN_HEADS),
        in_specs=[tok(0), tok(GDN_HEADS), tok(2 * GDN_HEADS), tok(3 * GDN_HEADS),
                  pl.BlockSpec((None, t, LANES), lambda i, h: (i, 0, 0)),
                  cw(0), cw(GDN_HEADS), cw(2 * GDN_HEADS), row, row, row],
        out_specs=pl.BlockSpec((None, t, LANES), lambda i, h: (i, 0, h)),
        out_shape=jax.ShapeDtypeStruct((b, t, GDN_W), BF16),
        scratch_shapes=[pltpu.VMEM((t + SUBLANES, LANES), F32)] + [seq_buf] * 9
        + [pltpu.VMEM((t // GDN_CHUNK * SUBLANES, LANES), F32)],
        compiler_params=_params("arbitrary", "arbitrary"),
        name="gdn",
    )(a, a, a, a, bd, conv_w, conv_w, conv_w, alog_row, dtb_row, norm_row)


def _transpose_values(v_ref, vt_s, seq):
    for g in range(v_ref.shape[1] // LANES):
        for c in range(seq // LANES):
            t, off = divmod(c * LANES, ATT_K_TILE)
            blk = v_ref[c * LANES:(c + 1) * LANES, g * LANES:(g + 1) * LANES]
            vt_s[g, t, :, off:off + LANES] = blk.astype(F32).T.astype(BF16)


def _stack_masked(q, width):
    lane = lax.broadcasted_iota(jnp.int32, q.shape, 1)
    return jnp.concatenate(
        [jnp.where((lane >= j * width) & (lane < (j + 1) * width), q, 0.0).astype(BF16)
         for j in range(LANES // width)], axis=0)


def _softmax_init(dv, width):
    return (jnp.full((1, width), -jnp.inf, F32), jnp.zeros((1, width), F32), jnp.zeros((dv, width), F32))


def _softmax_weights(st, s, cnt):
    m_old, l_old, _ = st
    m_new = jnp.maximum(m_old, jnp.max(s, axis=0, keepdims=True))
    p = jnp.exp(s - m_new)
    if cnt is not None:
        p = p * cnt
    alpha = jnp.exp(m_old - m_new)
    l_new = alpha * l_old + jnp.sum(p, axis=0, keepdims=True)
    return m_new, l_new, alpha, p.astype(BF16)


def _dsw_kernel(q_ref, k_ref, v_ref, o_ref, vt_s, cnt_s, bias_s, *, seq):
    tq, tk = ATT_Q_TILE, ATT_K_TILE
    blocks = q_ref.shape[1] // LANES
    heads = LANES // DSW_HEAD_DIM
    _transpose_values(v_ref, vt_s, seq)

    @pl.when(pl.program_id(0) == 0)
    def _():
        kr = lax.broadcasted_iota(jnp.int32, (tk, tq), 0)
        qc = lax.broadcasted_iota(jnp.int32, (tk, tq), 1)
        for d in range(DSW_FAR_TILE + 1):
            delta = d * tq + qc - kr
            causal = delta >= 0
            cnt = (jnp.where(causal & (delta <= 128), 1.0, 0.0)
                   + jnp.where(causal & (delta <= DSW_MAX_BAND) & ((delta & 3) == 0), 1.0, 0.0)
                   + jnp.where(causal & ((delta & 15) == 0), 1.0, 0.0))
            cnt_s[d] = cnt.astype(F32)
            bias_s[d] = jnp.where(cnt > 0.0, 0.0, -jnp.inf).astype(F32)

    def q_tile(qi, carry):
        qrows = pl.ds(pl.multiple_of(qi * tq, tq), tq)
        qstack = [_stack_masked(q_ref[qrows, g * LANES:(g + 1) * LANES].astype(F32), DSW_HEAD_DIM)
                  for g in range(blocks)]

        def k_tile(kj, st):
            krows = pl.ds(pl.multiple_of(kj * tk, tk), tk)
            far = jnp.minimum(qi - kj * (tk // tq), DSW_FAR_TILE)
            cnt = cnt_s[far]
            bias = bias_s[far]
            s = [_dot_nt(k_ref[krows, g * LANES:(g + 1) * LANES], qstack[g]) for g in range(blocks)]
            new = []
            for g in range(blocks):
                for h in range(heads):
                    old = st[g * heads + h]
                    m, l, alpha, p = _softmax_weights(old, s[g][:, h * tq:(h + 1) * tq] + bias, cnt)
                    pv = jnp.dot(vt_s[g, kj, h * DSW_HEAD_DIM:(h + 1) * DSW_HEAD_DIM, :], p,
                                 preferred_element_type=F32)
                    new.append((m, l, alpha * old[2] + pv))
            return tuple(new)

        n_k = (qi * tq) // tk + 1
        st = lax.fori_loop(0, n_k, k_tile,
                           tuple(_softmax_init(DSW_HEAD_DIM, tq) for _ in range(blocks * heads)))
        o = jnp.concatenate([x[2] / x[1] for x in st], axis=0)
        o_ref[qrows, :] = o.T.astype(o_ref.dtype)
        return carry

    lax.fori_loop(0, seq // tq, q_tile, 0)


def _dsw(bc):
    b, t, _ = bc.shape
    spec = lambda j: pl.BlockSpec((None, t, DSW_W), lambda i: (i, 0, j))
    return pl.pallas_call(
        functools.partial(_dsw_kernel, seq=t),
        grid=(b,),
        in_specs=[spec(0), spec(1), spec(2)],
        out_specs=pl.BlockSpec((None, t, DSW_W), lambda i: (i, 0, 0)),
        out_shape=jax.ShapeDtypeStruct((b, t, DSW_W), BF16),
        scratch_shapes=[pltpu.VMEM((DSW_W // LANES, t // ATT_K_TILE, LANES, ATT_K_TILE), BF16)]
        + [pltpu.VMEM((DSW_FAR_TILE + 1, ATT_K_TILE, ATT_Q_TILE), F32)] * 2,
        compiler_params=_params("arbitrary"),
        name="dsw",
    )(bc, bc, bc)


def _diff_kernel(q_ref, k_ref, v_ref, lam_ref, laminit_ref, nw_ref, o_ref, vt_s, *, seq):
    tq, tk = ATT_Q_TILE, ATT_K_TILE
    blocks = q_ref.shape[1] // LANES
    heads = LANES // DIFF_V_DIM
    _transpose_values(v_ref, vt_s, seq)
    kr = lax.broadcasted_iota(jnp.int32, (tk, tq), 0)
    qc = lax.broadcasted_iota(jnp.int32, (tk, tq), 1)
    lv = lam_ref[...]
    lam_init = laminit_ref[...]
    lam = (jnp.exp(jnp.sum(lv[0:1, :] * lv[1:2, :], keepdims=True))
           - jnp.exp(jnp.sum(lv[2:3, :] * lv[3:4, :], keepdims=True)) + lam_init)
    nw = nw_ref[...]

    def q_tile(qi, carry):
        qrows = pl.ds(pl.multiple_of(qi * tq, tq), tq)
        qstack = [_stack_masked(q_ref[qrows, g * LANES:(g + 1) * LANES].astype(F32), DIFF_QK_DIM)
                  for g in range(blocks)]

        def k_tile(kj, st, valid):
            krows = pl.ds(pl.multiple_of(kj * tk, tk), tk)
            s = [_dot_nt(k_ref[krows, g * LANES:(g + 1) * LANES], qstack[g])
                 for g in range(blocks)]
            new = []
            for g in range(blocks):
                for h in range(heads):
                    ml, ps, alphas = [], [], []
                    for mp in range(2):
                        j = 2 * h + mp
                        sj = s[g][:, j * tq:(j + 1) * tq]
                        if valid is not None:
                            sj = jnp.where(valid, sj, -jnp.inf)
                        m, l, alpha, p = _softmax_weights(st[2 * heads * g + j], sj, None)
                        ml.append((m, l))
                        alphas.append(alpha)
                        ps.append(p)
                    pv = jnp.dot(vt_s[g, kj, h * DIFF_V_DIM:(h + 1) * DIFF_V_DIM, :],
                                 jnp.concatenate(ps, axis=1), preferred_element_type=F32)
                    for mp in range(2):
                        acc = st[2 * heads * g + 2 * h + mp][2]
                        new.append(ml[mp] + (alphas[mp] * acc + pv[:, mp * tq:(mp + 1) * tq],))
            return tuple(new)

        st = tuple(_softmax_init(DIFF_V_DIM, tq) for _ in range(2 * heads * blocks))
        n_full = (qi * tq) // tk
        st = lax.fori_loop(0, n_full, functools.partial(k_tile, valid=None), st)
        st = k_tile(n_full, st, kr + (n_full * tk - qi * tq) <= qc)
        halves = []
        for h in range(heads * blocks):
            o = st[2 * h][2] / st[2 * h][1] - lam * (st[2 * h + 1][2] / st[2 * h + 1][1])
            halves.append(o * lax.rsqrt(jnp.mean(o * o, axis=0, keepdims=True) + EPS))
        o = jnp.concatenate(halves, axis=0).T
        o_ref[qrows, :] = (o * nw * (1.0 - lam_init)).astype(o_ref.dtype)
        return carry

    lax.fori_loop(0, seq // tq, q_tile, 0)


def _diff(bc, lam_vecs, lam_init, norm_row, layer):
    b, t, _ = bc.shape
    base = 3 * DSW_W // DIFF_W
    spec = lambda j: pl.BlockSpec((None, t, DIFF_W), lambda i: (i, 0, base + j))
    return pl.pallas_call(
        functools.partial(_diff_kernel, seq=t),
        grid=(b,),
        in_specs=[spec(0), spec(1), spec(2),
                  pl.BlockSpec((None, 4, DIFF_QK_DIM), lambda i: (layer, 0, 0)),
                  pl.BlockSpec((None, 1, 1), lambda i: (layer, 0, 0)),
                  pl.BlockSpec((None, 1, DIFF_W), lambda i: (layer, 0, 0))],
        out_specs=pl.BlockSpec((None, t, DIFF_W), lambda i: (i, 0, 0)),
        out_shape=jax.ShapeDtypeStruct((b, t, DIFF_W), BF16),
        scratch_shapes=[pltpu.VMEM((DIFF_W // LANES, t // ATT_K_TILE, LANES, ATT_K_TILE), BF16)],
        compiler_params=_params("arbitrary"),
        name="diff",
    )(bc, bc, bc, lam_vecs, lam_init, norm_row)


def _layer_norm(h, g, b):
    mu = jnp.mean(h, axis=-1, keepdims=True)
    hc = h - mu
    var = jnp.mean(hc * hc, axis=-1, keepdims=True)
    return hc * lax.rsqrt(var + EPS) * g + b


def _out_ln_kernel(x_ref, ya_ref, yb_ref, yc_ref, w_ref, g_ref, b_ref, o_ref, *, alpha):
    y = jnp.dot(ya_ref[...], w_ref[0:GDN_W, :], preferred_element_type=F32)
    y = y + jnp.dot(yb_ref[...], w_ref[GDN_W:GDN_W + DSW_W, :], preferred_element_type=F32)
    y = y + jnp.dot(yc_ref[...], w_ref[GDN_W + DSW_W:, :], preferred_element_type=F32)
    o_ref[...] = _layer_norm(alpha * x_ref[...] + y, g_ref[...], b_ref[...])


def _out_ln(x, ya, yb, yc, w, g, b, layer, alpha):
    n, d = x.shape
    tm = min(ROW_TILE, n)
    rows = lambda width: pl.BlockSpec((tm, width), lambda i: (i, 0))
    vec = pl.BlockSpec((None, 1, d), lambda i: (layer, 0, 0))
    return pl.pallas_call(
        functools.partial(_out_ln_kernel, alpha=alpha),
        grid=(n // tm,),
        in_specs=[rows(d), rows(GDN_W), rows(DSW_W), rows(DIFF_W),
                  pl.BlockSpec((None, d, d), lambda i: (layer, 0, 0)), vec, vec],
        out_specs=rows(d),
        out_shape=jax.ShapeDtypeStruct((n, d), F32),
        compiler_params=_params("arbitrary"),
        name="out_ln",
    )(x, ya, yb, yc, w, g, b)


def _ffn_up_kernel(x_ref, wg_ref, wv_ref, cg_ref, cv_ref, h_ref, work_g, work_v, carry_g, carry_v,
                   *, tiles_per_seq, col_chunk):
    tm = x_ref.shape[0]
    width = h_ref.shape[1]

    @pl.when(pl.program_id(0) % tiles_per_seq == 0)
    def _():
        carry_g[...] = jnp.zeros(carry_g.shape, F32)
        carry_v[...] = jnp.zeros(carry_v.shape, F32)

    x = x_ref[...].astype(BF16)
    first = SUBLANES - (FFN_CONV - 1)

    def conv(u, cw_ref, work, carry, cols):
        work[0:SUBLANES, :] = carry[:, cols]
        work[SUBLANES:SUBLANES + tm, :] = u
        carry[:, cols] = work[tm:tm + SUBLANES, :]
        cw = cw_ref[:, cols]
        acc = u * cw[FFN_CONV - 1:FFN_CONV, :]
        for j in range(FFN_CONV - 1):
            acc = acc + work[first + j:first + j + tm, :] * cw[j:j + 1, :]
        return acc

    for c in range(0, width, col_chunk):
        cols = slice(c, c + col_chunk)
        gate = conv(jnp.dot(x, wg_ref[:, cols], preferred_element_type=F32), cg_ref, work_g, carry_g, cols)
        val = conv(jnp.dot(x, wv_ref[:, cols], preferred_element_type=F32), cv_ref, work_v, carry_v, cols)
        h_ref[:, cols] = (gate * _sigmoid(gate) * val).astype(h_ref.dtype)


def _ffn_up(x, wg, wv, cg, cv, layer, seq):
    n, d = x.shape
    width = wg.shape[-1]
    tm = min(ROW_TILE, seq)
    col_chunk = 256
    wspec = pl.BlockSpec((None, d, width), lambda i: (layer, 0, 0))
    cspec = pl.BlockSpec((None, FFN_CONV, width), lambda i: (layer, 0, 0))
    return pl.pallas_call(
        functools.partial(_ffn_up_kernel, tiles_per_seq=seq // tm, col_chunk=col_chunk),
        grid=(n // tm,),
        in_specs=[pl.BlockSpec((tm, d), lambda i: (i, 0)), wspec, wspec, cspec, cspec],
        out_specs=pl.BlockSpec((tm, width), lambda i: (i, 0)),
        out_shape=jax.ShapeDtypeStruct((n, width), BF16),
        scratch_shapes=[pltpu.VMEM((tm + SUBLANES, col_chunk), F32)] * 2
        + [pltpu.VMEM((SUBLANES, width), F32)] * 2,
        compiler_params=_params("arbitrary"),
        name="ffn_up",
    )(x, wg, wv, cg, cv)


def _ffn_down_kernel(x_ref, h_ref, w_ref, g_ref, b_ref, o_ref, *, alpha):
    f = jnp.dot(h_ref[...], w_ref[...], preferred_element_type=F32)
    o_ref[...] = _layer_norm(alpha * x_ref[...] + f, g_ref[...], b_ref[...])


def _ffn_down(x, h, w, g, b, layer, alpha):
    n, d = x.shape
    width = h.shape[1]
    tm = min(ROW_TILE, n)
    vec = pl.BlockSpec((None, 1, d), lambda i: (layer, 0, 0))
    return pl.pallas_call(
        functools.partial(_ffn_down_kernel, alpha=alpha),
        grid=(n // tm,),
        in_specs=[pl.BlockSpec((tm, d), lambda i: (i, 0)), pl.BlockSpec((tm, width), lambda i: (i, 0)),
                  pl.BlockSpec((None, width, d), lambda i: (layer, 0, 0)), vec, vec],
        out_specs=pl.BlockSpec((tm, d), lambda i: (i, 0)),
        out_shape=jax.ShapeDtypeStruct((n, d), F32),
        compiler_params=_params("arbitrary"),
        name="ffn_down",
    )(x, h, w, g, b)


def _pad_last(a, width):
    return jnp.pad(a, [(0, 0)] * (a.ndim - 1) + [(0, width - a.shape[-1])])


def kernel(x, w_in, gdn_conv, gdn_a_log, gdn_dt_bias, gdn_norm, diff_lambda, diff_norm, w_out,
           ln1_g, ln1_b, w_up, ffn_conv, w_down, ln2_g, ln2_b):
    batch, seq, d = x.shape
    depth = w_in.shape[0]
    d_ff = w_down.shape[1]
    ff_pad = -(-d_ff // LANES) * LANES
    alpha = (2 * depth) ** 0.25

    bd0 = A_W
    bc0 = A_W + 2 * GDN_HEADS
    wa = w_in[:, :, :A_W].astype(BF16)
    wbd = _pad_last(w_in[:, :, bd0:bc0], LANES).astype(BF16)
    wbc = w_in[:, :, bc0:].astype(BF16)
    scale = jnp.concatenate([
        jnp.full((DSW_W,), DSW_HEAD_DIM ** -0.5, F32), jnp.ones((2 * DSW_W,), F32),
        jnp.full((DIFF_QK_W,), DIFF_QK_DIM ** -0.5, F32), jnp.ones((DIFF_QK_W + DIFF_W,), F32)])[None, :]
    lane_row = lambda v, off: jnp.pad(v, ((0, 0), (off, LANES - off - v.shape[1])))[:, None, :]
    alog_row = lane_row(gdn_a_log, GDN_HEADS)
    dtb_row = lane_row(gdn_dt_bias, GDN_HEADS)
    gdn_norm_row = gdn_norm[:, None, :]
    diff_norm_row = jnp.tile(diff_norm, (1, DIFF_HEADS))[:, None, :]
    lam_init = jnp.asarray([0.8 - 0.6 * math.exp(-0.3 * l) for l in range(depth)], F32)[:, None, None]
    w_out_b = w_out.astype(BF16)
    wg = _pad_last(w_up[:, :, :d_ff], ff_pad).astype(BF16)
    wv = _pad_last(w_up[:, :, d_ff:], ff_pad).astype(BF16)
    cg = _pad_last(ffn_conv[:, :, :d_ff], ff_pad)
    cv = _pad_last(ffn_conv[:, :, d_ff:], ff_pad)
    w_down_b = jnp.pad(w_down, ((0, 0), (0, ff_pad - d_ff), (0, 0))).astype(BF16)
    vec3 = lambda v: v[:, None, :]

    xf = x.reshape(batch * seq, d)
    for l in range(depth):
        a, bc, bd = _in_proj(xf, wa, wbc, wbd, scale, l)
        bc3 = bc.reshape(batch, seq, BC_W)
        ya = _gdn(a.reshape(batch, seq, A_W), bd.reshape(batch, seq, LANES), gdn_conv,
                  alog_row, dtb_row, gdn_norm_row, l)
        yb = _dsw(bc3)
        yc = _diff(bc3, diff_lambda, lam_init, diff_norm_row, l)
        n = batch * seq
        x1 = _out_ln(xf, ya.reshape(n, GDN_W), yb.reshape(n, DSW_W), yc.reshape(n, DIFF_W),
                     w_out_b, vec3(ln1_g), vec3(ln1_b), l, alpha)
        h = _ffn_up(x1, wg, wv, cg, cv, l, seq)
        xf = _ffn_down(x1, h, w_down_b, vec3(ln2_g), vec3(ln2_b), l, alpha)
    return xf.reshape(batch, seq, d)
```

```python
import functools
import math

import jax
import jax.numpy as jnp
from jax import lax
from jax.experimental import pallas as pl
from jax.experimental.pallas import tpu as pltpu

F32 = jnp.float32
BF16 = jnp.bfloat16

LANES = 128
SUBLANES = 8
VMEM_LIMIT = 48 * 1024 * 1024

GDN_HEADS = 4
GDN_HEAD_DIM = 128
GDN_CONV = 4
DSW_HEADS = 4
DSW_HEAD_DIM = 64
DIFF_HEADS = 4
DIFF_QK_DIM = 32
DIFF_V_DIM = 64
FFN_CONV = 3
EPS = 1e-5

GDN_W = GDN_HEADS * GDN_HEAD_DIM
DSW_W = DSW_HEADS * DSW_HEAD_DIM
DIFF_W = DIFF_HEADS * DIFF_V_DIM
DIFF_QK_W = DIFF_HEADS * 2 * DIFF_QK_DIM
A_W = 4 * GDN_W
BC_W = 3 * DSW_W + 2 * DIFF_QK_W + DIFF_W

ROW_TILE = 512
ATT_Q_TILE = 512
ATT_K_TILE = 512
ATT_AUG = 16
LOG2E = math.log2(math.e)
GDN_CHUNK = 128
GDN_GROUP = 8
GDN_SPLIT_SPAN = 16
DSW_MAX_BAND = 512
DSW_FAR_TILE = -(-(DSW_MAX_BAND + ATT_K_TILE) // ATT_Q_TILE)


def _dot(a, b):
    return jnp.dot(a.astype(BF16), b.astype(BF16), preferred_element_type=F32)


def _dot_nt(a, b):
    return lax.dot_general(a.astype(BF16), b.astype(BF16), (((1,), (1,)), ((), ())),
                           preferred_element_type=F32)


def _split_bf16(a):
    hi = a.astype(BF16)
    return hi, (a - hi.astype(F32)).astype(BF16)


def _split3_bf16(a):
    hi, mid = _split_bf16(a)
    return hi, mid, (a - hi.astype(F32) - mid.astype(F32)).astype(BF16)


def _dot_split(a, b):
    ah, al = _split_bf16(a)
    bh, bl = _split_bf16(b)
    return jnp.dot(jnp.concatenate([ah, al, ah], axis=1), jnp.concatenate([bh, bh, bl], axis=0),
                   preferred_element_type=F32)


def _sigmoid(x):
    return 1.0 / (1.0 + jnp.exp(-x))


def _params(*sem):
    return pltpu.CompilerParams(dimension_semantics=sem, vmem_limit_bytes=VMEM_LIMIT)


def _in_proj_kernel(x_ref, wa_ref, wbc_ref, wbd_ref, scale_ref, a_ref, bc_ref, bd_ref, *, col_chunk):
    x = x_ref[...].astype(BF16)
    for c in range(0, A_W, col_chunk):
        a_ref[:, c:c + col_chunk] = jnp.dot(x, wa_ref[:, c:c + col_chunk], preferred_element_type=F32)
    for c in range(0, BC_W, col_chunk):
        r = jnp.dot(x, wbc_ref[:, c:c + col_chunk], preferred_element_type=F32)
        bc_ref[:, c:c + col_chunk] = (r * scale_ref[:, c:c + col_chunk]).astype(BF16)
    bd_ref[...] = jnp.dot(x, wbd_ref[...], preferred_element_type=F32)


def _in_proj(x, wa, wbc, wbd, scale, layer):
    n, d = x.shape
    tm = min(ROW_TILE, n)
    return pl.pallas_call(
        functools.partial(_in_proj_kernel, col_chunk=512),
        grid=(n // tm,),
        in_specs=[
            pl.BlockSpec((tm, d), lambda i: (i, 0)),
            pl.BlockSpec((None, d, A_W), lambda i: (layer, 0, 0)),
            pl.BlockSpec((None, d, BC_W), lambda i: (layer, 0, 0)),
            pl.BlockSpec((None, d, LANES), lambda i: (layer, 0, 0)),
            pl.BlockSpec((1, BC_W), lambda i: (0, 0)),
        ],
        out_specs=[
            pl.BlockSpec((tm, A_W), lambda i: (i, 0)),
            pl.BlockSpec((tm, BC_W), lambda i: (i, 0)),
            pl.BlockSpec((tm, LANES), lambda i: (i, 0)),
        ],
        out_shape=[
            jax.ShapeDtypeStruct((n, A_W), F32),
            jax.ShapeDtypeStruct((n, BC_W), BF16),
            jax.ShapeDtypeStruct((n, LANES), F32),
        ],
        compiler_params=_params("arbitrary"),
        name="in_proj",
    )(x, wa, wbc, wbd, scale)


def _gdn_kernel(q_ref, k_ref, v_ref, z_ref, bd_ref, cq_ref, ck_ref, cv_ref, alog_ref, dtb_ref, nw_ref,
                y_ref,
                pad_s, q_s, k_s, v_s, beta_s, g_s, sa_s, sb_s, oq_s, oc_s, gl_s, *, seq):
    C = GDN_CHUNK
    n_chunks = seq // C
    head = pl.program_id(1)

    def conv_silu(x_ref, cw_ref):
        pad_s[0:SUBLANES, :] = jnp.zeros((SUBLANES, LANES), F32)
        pad_s[SUBLANES:SUBLANES + seq, :] = x_ref[...]
        cw = cw_ref[...]
        first = SUBLANES - (GDN_CONV - 1)
        acc = pad_s[first:first + seq, :] * cw[0:1, :]
        for j in range(1, GDN_CONV):
            acc = acc + pad_s[first + j:first + j + seq, :] * cw[j:j + 1, :]
        return acc * _sigmoid(acc)

    q = conv_silu(q_ref, cq_ref)
    q_s[...] = q * lax.rsqrt(jnp.sum(q * q, axis=-1, keepdims=True) + 1e-6) * (GDN_HEAD_DIM ** -0.5)
    k = conv_silu(k_ref, ck_ref)
    k_s[...] = k * lax.rsqrt(jnp.sum(k * k, axis=-1, keepdims=True) + 1e-6)
    v_s[...] = conv_silu(v_ref, cv_ref)

    bd = bd_ref[...]
    lane = lax.broadcasted_iota(jnp.int32, (seq, LANES), 1)
    beta_all = _sigmoid(bd)
    xg = bd + dtb_ref[...]
    softplus = jnp.maximum(xg, 0.0) + jnp.log(1.0 + jnp.exp(-jnp.abs(xg)))
    g_all = -jnp.exp(alog_ref[...]) * softplus
    beta = jnp.sum(jnp.where(lane == head, beta_all, 0.0), axis=-1, keepdims=True)
    g = jnp.sum(jnp.where(lane == head + GDN_HEADS, g_all, 0.0), axis=-1, keepdims=True)
    beta_s[...] = jnp.broadcast_to(beta, (seq, LANES))
    g_s[...] = jnp.broadcast_to(g, (seq, LANES))

    ri = lax.broadcasted_iota(jnp.int32, (C, C), 0)
    ci = lax.broadcasted_iota(jnp.int32, (C, C), 1)
    lower_incl = ri >= ci
    strict = ri > ci
    tri = jnp.where(lower_incl, 1.0, 0.0).astype(BF16)
    tri3 = jnp.concatenate([tri, tri, tri], axis=1)

    def chunk_group(gi, carry):
        grp = range(GDN_GROUP)
        rows = [pl.ds(pl.multiple_of((gi * GDN_GROUP + j) * C, C), C) for j in grp]
        qc = [q_s[r, :] for r in rows]
        kc = [k_s[r, :] for r in rows]
        vc = [v_s[r, :] for r in rows]
        bb = [beta_s[r, :] for r in rows]
        gg = [g_s[r, :] for r in rows]
        cum = [jnp.dot(tri3, jnp.concatenate(_split3_bf16(g), axis=0), preferred_element_type=F32)
               for g in gg]
        decay = [jnp.exp(jnp.where(lower_incl, c - c.T, -jnp.inf)) for c in cum]
        kb = [k * b for k, b in zip(kc, bb)]
        scores = [_dot_nt(jnp.concatenate([x, q], axis=0), k) for x, q, k in zip(kb, qc, kc)]
        m = [jnp.where(strict, s[0:C, :] * d, 0.0) for s, d in zip(scores, decay)]
        qk = [jnp.where(lower_incl, s[C:, :] * d, 0.0) for s, d in zip(scores, decay)]
        nmat = [-x for x in m]
        p = [_dot_split(x, x) for x in m]
        span = 2
        while 2 * span < C:
            dot = _dot_split if span < GDN_SPLIT_SPAN else _dot
            both = [dot(jnp.concatenate([n, x], axis=0), x) for n, x in zip(nmat, p)]
            nmat = [n + x + b[0:C, :] for n, x, b in zip(nmat, p, both)]
            p = [b[C:, :] for b in both]
            span *= 2
        nmat = [n + x + _dot(n, x) for n, x in zip(nmat, p)]
        ecum = [jnp.exp(c) for c in cum]
        vb = [v * b for v, b in zip(vc, bb)]
        kbe = [x * e for x, e in zip(kb, ecum)]
        wu = [jnp.concatenate([x, y], axis=1) for x, y in zip(kbe, vb)]
        wu = [x + _dot(n, x) for n, x in zip(nmat, wu)]
        qd = [q * e for q, e in zip(qc, ecum)]
        cum_last = [c[C - 1:C, :] for c in cum]
        ktt = [(k * jnp.exp(cl - c)).T for k, cl, c in zip(kc, cum_last, cum)]
        prod = [_dot(jnp.concatenate([kt, a], axis=0), x) for kt, a, x in zip(ktt, qk, wu)]
        for j in grp:
            sa_s[rows[j], :] = -prod[j][0:C, 0:C]
            sb_s[rows[j], :] = prod[j][0:C, C:]
            oq_s[rows[j], :] = qd[j] - prod[j][C:, 0:C]
            oc_s[rows[j], :] = prod[j][C:, C:]
            gl_s[pl.ds(pl.multiple_of((gi * GDN_GROUP + j) * SUBLANES, SUBLANES), SUBLANES), :] = (
                jnp.broadcast_to(jnp.exp(cum_last[j]), (SUBLANES, LANES)))
        return carry

    lax.fori_loop(0, n_chunks // GDN_GROUP, chunk_group, 0)

    nw = nw_ref[...]

    def scan_step(c, s):
        rows = pl.ds(pl.multiple_of(c * C, C), C)
        o = _dot(oq_s[rows, :], s) + oc_s[rows, :]
        gl = gl_s[pl.ds(pl.multiple_of(c * SUBLANES, SUBLANES), 1), :]
        s = s * gl + (_dot(sa_s[rows, :], s) + sb_s[rows, :])
        z = z_ref[rows, :]
        o = o * lax.rsqrt(jnp.mean(o * o, axis=-1, keepdims=True) + EPS) * nw
        y_ref[rows, :] = (o * (z * _sigmoid(z))).astype(y_ref.dtype)
        return s

    lax.fori_loop(0, n_chunks, scan_step, jnp.zeros((GDN_HEAD_DIM, GDN_HEAD_DIM), F32))


def _gdn(a, bd, conv_w, alog_row, dtb_row, norm_row, layer):
    b, t, _ = a.shape
    tok = lambda off: pl.BlockSpec((None, t, LANES), lambda i, h: (i, 0, off + h))
    cw = lambda off: pl.BlockSpec((None, GDN_CONV, LANES), lambda i, h: (layer, 0, off + h))
    row = pl.BlockSpec((None, 1, LANES), lambda i, h: (layer, 0, 0))
    seq_buf = pltpu.VMEM((t, LANES), F32)
    return pl.pallas_call(
        functools.partial(_gdn_kernel, seq=t),
        grid=(b, GDN_HEADS),
        in_specs=[tok(0), tok(GDN_HEADS), tok(2 * GDN_HEADS), tok(3 * GDN_HEADS),
                  pl.BlockSpec((None, t, LANES), lambda i, h: (i, 0, 0)),
                  cw(0), cw(GDN_HEADS), cw(2 * GDN_HEADS), row, row, row],
        out_specs=pl.BlockSpec((None, t, LANES), lambda i, h: (i, 0, h)),
        out_shape=jax.ShapeDtypeStruct((b, t, GDN_W), BF16),
        scratch_shapes=[pltpu.VMEM((t + SUBLANES, LANES), F32)] + [seq_buf] * 9
        + [pltpu.VMEM((t // GDN_CHUNK * SUBLANES, LANES), F32)],
        compiler_params=_params("arbitrary", "arbitrary"),
        name="gdn",
    )(a, a, a, a, bd, conv_w, conv_w, conv_w, alog_row, dtb_row, norm_row)


def _stage_values(v_ref, vt_s, seq, dv):
    aug = dv + ATT_AUG
    ones_row = jnp.where(lax.broadcasted_iota(jnp.int32, (ATT_AUG, ATT_K_TILE), 0) == 0, 1.0, 0.0).astype(BF16)
    for g in range(v_ref.shape[1] // LANES):
        for t in range(seq // ATT_K_TILE):
            for h in range(LANES // dv):
                vt_s[g, t, h * aug + dv:(h + 1) * aug, :] = ones_row
        for c in range(seq // LANES):
            t, off = divmod(c * LANES, ATT_K_TILE)
            blk = v_ref[c * LANES:(c + 1) * LANES, g * LANES:(g + 1) * LANES].astype(F32).T.astype(BF16)
            for h in range(LANES // dv):
                vt_s[g, t, h * aug:h * aug + dv, off:off + LANES] = blk[h * dv:(h + 1) * dv, :]


def _stack_masked(q, width):
    lane = lax.broadcasted_iota(jnp.int32, q.shape, 1)
    return jnp.concatenate(
        [jnp.where((lane >= j * width) & (lane < (j + 1) * width), q, 0.0).astype(BF16)
         for j in range(LANES // width)], axis=0)


def _softmax_reset(m_s, acc_s):
    m_s[...] = jnp.full(m_s.shape, -jnp.inf, F32)
    acc_s[...] = jnp.zeros(acc_s.shape, F32)


def _softmax_tile(m_s, acc_s, streams, scores, cnt, vt):
    ps, alphas = [], []
    for i, s in zip(streams, scores):
        m_old = m_s[i, 0:1, :]
        m_new = jnp.maximum(m_old, jnp.max(s, axis=0, keepdims=True))
        p = jnp.exp2(s - m_new)
        if cnt is not None:
            p = p * cnt
        alphas.append(jnp.exp2(m_old - m_new))
        m_s[i, 0:1, :] = m_new
        ps.append(p.astype(BF16))
    pv = jnp.dot(vt, jnp.concatenate(ps, axis=1), preferred_element_type=F32)
    width = pv.shape[1] // len(streams)
    for n, i in enumerate(streams):
        acc_s[i] = alphas[n] * acc_s[i] + pv[:, n * width:(n + 1) * width]


def _dsw_kernel(q_ref, k_ref, v_ref, o_ref, vt_s, cnt_s, bias_s, m_s, acc_s, *, seq):
    tq, tk = ATT_Q_TILE, ATT_K_TILE
    blocks = q_ref.shape[1] // LANES
    dv = DSW_HEAD_DIM
    heads = LANES // dv
    aug = dv + ATT_AUG
    _stage_values(v_ref, vt_s, seq, dv)

    @pl.when(pl.program_id(0) == 0)
    def _():
        kr = lax.broadcasted_iota(jnp.int32, (tk, tq), 0)
        qc = lax.broadcasted_iota(jnp.int32, (tk, tq), 1)
        for d in range(DSW_FAR_TILE + 1):
            delta = d * tq + qc - kr
            causal = delta >= 0
            cnt = (jnp.where(causal & (delta <= 128), 1.0, 0.0)
                   + jnp.where(causal & (delta <= DSW_MAX_BAND) & ((delta & 3) == 0), 1.0, 0.0)
                   + jnp.where(causal & ((delta & 15) == 0), 1.0, 0.0))
            cnt_s[d] = cnt.astype(F32)
            bias_s[d] = jnp.where(cnt > 0.0, 0.0, -jnp.inf).astype(F32)

    def q_tile(qi, carry):
        qrows = pl.ds(pl.multiple_of(qi * tq, tq), tq)
        qstack = [_stack_masked(q_ref[qrows, g * LANES:(g + 1) * LANES].astype(F32), DSW_HEAD_DIM)
                  for g in range(blocks)]

        def k_tile(kj, st):
            krows = pl.ds(pl.multiple_of(kj * tk, tk), tk)
            far = jnp.minimum(qi - kj * (tk // tq), DSW_FAR_TILE)
            cnt = cnt_s[far]
            bias = bias_s[far]
            s = [_dot_nt(k_ref[krows, g * LANES:(g + 1) * LANES], qstack[g]) for g in range(blocks)]
            for g in range(blocks):
                for h in range(heads):
                    _softmax_tile(m_s, acc_s, [g * heads + h], [s[g][:, h * tq:(h + 1) * tq] + bias], cnt,
                                  vt_s[g, kj, h * aug:(h + 1) * aug, :])
            return st

        _softmax_reset(m_s, acc_s)
        n_k = (qi * tq) // tk + 1
        lax.fori_loop(0, n_k, k_tile, 0)
        o = jnp.concatenate([acc_s[i, 0:dv, :] / acc_s[i, dv:dv + 1, :] for i in range(blocks * heads)],
                            axis=0)
        o_ref[qrows, :] = o.T.astype(o_ref.dtype)
        return carry

    lax.fori_loop(0, seq // tq, q_tile, 0)


def _dsw(bc):
    b, t, _ = bc.shape
    spec = lambda j: pl.BlockSpec((None, t, DSW_W), lambda i: (i, 0, j))
    return pl.pallas_call(
        functools.partial(_dsw_kernel, seq=t),
        grid=(b,),
        in_specs=[spec(0), spec(1), spec(2)],
        out_specs=pl.BlockSpec((None, t, DSW_W), lambda i: (i, 0, 0)),
        out_shape=jax.ShapeDtypeStruct((b, t, DSW_W), BF16),
        scratch_shapes=[pltpu.VMEM((DSW_W // LANES, t // ATT_K_TILE,
                                    LANES // DSW_HEAD_DIM * (DSW_HEAD_DIM + ATT_AUG), ATT_K_TILE), BF16)]
        + [pltpu.VMEM((DSW_FAR_TILE + 1, ATT_K_TILE, ATT_Q_TILE), F32)] * 2
        + [pltpu.VMEM((DSW_HEADS, SUBLANES, ATT_Q_TILE), F32),
           pltpu.VMEM((DSW_HEADS, DSW_HEAD_DIM + ATT_AUG, ATT_Q_TILE), F32)],
        compiler_params=_params("arbitrary"),
        name="dsw",
    )(bc, bc, bc)


def _diff_kernel(q_ref, k_ref, v_ref, lam_ref, laminit_ref, nw_ref, o_ref, vt_s, m_s, acc_s, *, seq):
    tq, tk = ATT_Q_TILE, ATT_K_TILE
    blocks = q_ref.shape[1] // LANES
    dv = DIFF_V_DIM
    heads = LANES // dv
    aug = dv + ATT_AUG
    _stage_values(v_ref, vt_s, seq, dv)
    kr = lax.broadcasted_iota(jnp.int32, (tk, tq), 0)
    qc = lax.broadcasted_iota(jnp.int32, (tk, tq), 1)
    lv = lam_ref[...]
    lam_init = laminit_ref[...]
    lam = (jnp.exp(jnp.sum(lv[0:1, :] * lv[1:2, :], keepdims=True))
           - jnp.exp(jnp.sum(lv[2:3, :] * lv[3:4, :], keepdims=True)) + lam_init)
    nw = nw_ref[...]

    def q_tile(qi, carry):
        qrows = pl.ds(pl.multiple_of(qi * tq, tq), tq)
        qstack = [_stack_masked(q_ref[qrows, g * LANES:(g + 1) * LANES].astype(F32), DIFF_QK_DIM)
                  for g in range(blocks)]

        def k_tile(kj, st, valid):
            krows = pl.ds(pl.multiple_of(kj * tk, tk), tk)
            s = [_dot_nt(k_ref[krows, g * LANES:(g + 1) * LANES], qstack[g])
                 for g in range(blocks)]
            for g in range(blocks):
                for h in range(heads):
                    scores = [s[g][:, (2 * h + mp) * tq:(2 * h + mp + 1) * tq] for mp in range(2)]
                    if valid is not None:
                        scores = [jnp.where(valid, x, -jnp.inf) for x in scores]
                    first = 2 * (heads * g + h)
                    _softmax_tile(m_s, acc_s, [first, first + 1], scores, None,
                                  vt_s[g, kj, h * aug:(h + 1) * aug, :])
            return st

        _softmax_reset(m_s, acc_s)
        n_full = (qi * tq) // tk
        lax.fori_loop(0, n_full, functools.partial(k_tile, valid=None), 0)
        k_tile(n_full, 0, kr + (n_full * tk - qi * tq) <= qc)
        att = [acc_s[i, 0:dv, :] / acc_s[i, dv:dv + 1, :] for i in range(2 * heads * blocks)]
        halves = []
        for h in range(heads * blocks):
            o = att[2 * h] - lam * att[2 * h + 1]
            halves.append(o * lax.rsqrt(jnp.mean(o * o, axis=0, keepdims=True) + EPS))
        o = jnp.concatenate(halves, axis=0).T
        o_ref[qrows, :] = (o * nw * (1.0 - lam_init)).astype(o_ref.dtype)
        return carry

    lax.fori_loop(0, seq // tq, q_tile, 0)


def _diff(bc, lam_vecs, lam_init, norm_row, layer):
    b, t, _ = bc.shape
    base = 3 * DSW_W // DIFF_W
    spec = lambda j: pl.BlockSpec((None, t, DIFF_W), lambda i: (i, 0, base + j))
    return pl.pallas_call(
        functools.partial(_diff_kernel, seq=t),
        grid=(b,),
        in_specs=[spec(0), spec(1), spec(2),
                  pl.BlockSpec((None, 4, DIFF_QK_DIM), lambda i: (layer, 0, 0)),
                  pl.BlockSpec((None, 1, 1), lambda i: (layer, 0, 0)),
                  pl.BlockSpec((None, 1, DIFF_W), lambda i: (layer, 0, 0))],
        out_specs=pl.BlockSpec((None, t, DIFF_W), lambda i: (i, 0, 0)),
        out_shape=jax.ShapeDtypeStruct((b, t, DIFF_W), BF16),
        scratch_shapes=[pltpu.VMEM((DIFF_W // LANES, t // ATT_K_TILE,
                                    LANES // DIFF_V_DIM * (DIFF_V_DIM + ATT_AUG), ATT_K_TILE), BF16),
                        pltpu.VMEM((2 * DIFF_HEADS, SUBLANES, ATT_Q_TILE), F32),
                        pltpu.VMEM((2 * DIFF_HEADS, DIFF_V_DIM + ATT_AUG, ATT_Q_TILE), F32)],
        compiler_params=_params("arbitrary"),
        name="diff",
    )(bc, bc, bc, lam_vecs, lam_init, norm_row)


def _layer_norm(h, g, b):
    mu = jnp.mean(h, axis=-1, keepdims=True)
    hc = h - mu
    var = jnp.mean(hc * hc, axis=-1, keepdims=True)
    return hc * lax.rsqrt(var + EPS) * g + b


def _out_ln_kernel(x_ref, ya_ref, yb_ref, yc_ref, w_ref, g_ref, b_ref, o_ref, *, alpha):
    y = jnp.dot(ya_ref[...], w_ref[0:GDN_W, :], preferred_element_type=F32)
    y = y + jnp.dot(yb_ref[...], w_ref[GDN_W:GDN_W + DSW_W, :], preferred_element_type=F32)
    y = y + jnp.dot(yc_ref[...], w_ref[GDN_W + DSW_W:, :], preferred_element_type=F32)
    o_ref[...] = _layer_norm(alpha * x_ref[...] + y, g_ref[...], b_ref[...])


def _out_ln(x, ya, yb, yc, w, g, b, layer, alpha):
    n, d = x.shape
    tm = min(ROW_TILE, n)
    rows = lambda width: pl.BlockSpec((tm, width), lambda i: (i, 0))
    vec = pl.BlockSpec((None, 1, d), lambda i: (layer, 0, 0))
    return pl.pallas_call(
        functools.partial(_out_ln_kernel, alpha=alpha),
        grid=(n // tm,),
        in_specs=[rows(d), rows(GDN_W), rows(DSW_W), rows(DIFF_W),
                  pl.BlockSpec((None, d, d), lambda i: (layer, 0, 0)), vec, vec],
        out_specs=rows(d),
        out_shape=jax.ShapeDtypeStruct((n, d), F32),
        compiler_params=_params("arbitrary"),
        name="out_ln",
    )(x, ya, yb, yc, w, g, b)


def _ffn_up_kernel(x_ref, wg_ref, wv_ref, cg_ref, cv_ref, h_ref, work_g, work_v, carry_g, carry_v,
                   *, tiles_per_seq, col_chunk):
    tm = x_ref.shape[0]
    width = h_ref.shape[1]

    @pl.when(pl.program_id(0) % tiles_per_seq == 0)
    def _():
        carry_g[...] = jnp.zeros(carry_g.shape, F32)
        carry_v[...] = jnp.zeros(carry_v.shape, F32)

    x = x_ref[...].astype(BF16)
    first = SUBLANES - (FFN_CONV - 1)

    def conv(u, cw_ref, work, carry, cols):
        work[0:SUBLANES, :] = carry[:, cols]
        work[SUBLANES:SUBLANES + tm, :] = u
        carry[:, cols] = work[tm:tm + SUBLANES, :]
        cw = cw_ref[:, cols]
        acc = u * cw[FFN_CONV - 1:FFN_CONV, :]
        for j in range(FFN_CONV - 1):
            acc = acc + work[first + j:first + j + tm, :] * cw[j:j + 1, :]
        return acc

    for c in range(0, width, col_chunk):
        cols = slice(c, c + col_chunk)
        gate = conv(jnp.dot(x, wg_ref[:, cols], preferred_element_type=F32), cg_ref, work_g, carry_g, cols)
        val = conv(jnp.dot(x, wv_ref[:, cols], preferred_element_type=F32), cv_ref, work_v, carry_v, cols)
        h_ref[:, cols] = (gate * _sigmoid(gate) * val).astype(h_ref.dtype)


def _ffn_up(x, wg, wv, cg, cv, layer, seq):
    n, d = x.shape
    width = wg.shape[-1]
    tm = min(ROW_TILE, seq)
    col_chunk = 256
    wspec = pl.BlockSpec((None, d, width), lambda i: (layer, 0, 0))
    cspec = pl.BlockSpec((None, FFN_CONV, width), lambda i: (layer, 0, 0))
    return pl.pallas_call(
        functools.partial(_ffn_up_kernel, tiles_per_seq=seq // tm, col_chunk=col_chunk),
        grid=(n // tm,),
        in_specs=[pl.BlockSpec((tm, d), lambda i: (i, 0)), wspec, wspec, cspec, cspec],
        out_specs=pl.BlockSpec((tm, width), lambda i: (i, 0)),
        out_shape=jax.ShapeDtypeStruct((n, width), BF16),
        scratch_shapes=[pltpu.VMEM((tm + SUBLANES, col_chunk), F32)] * 2
        + [pltpu.VMEM((SUBLANES, width), F32)] * 2,
        compiler_params=_params("arbitrary"),
        name="ffn_up",
    )(x, wg, wv, cg, cv)


def _ffn_down_kernel(x_ref, h_ref, w_ref, g_ref, b_ref, o_ref, *, alpha):
    f = jnp.dot(h_ref[...], w_ref[...], preferred_element_type=F32)
    o_ref[...] = _layer_norm(alpha * x_ref[...] + f, g_ref[...], b_ref[...])


def _ffn_down(x, h, w, g, b, layer, alpha):
    n, d = x.shape
    width = h.shape[1]
    tm = min(ROW_TILE, n)
    vec = pl.BlockSpec((None, 1, d), lambda i: (layer, 0, 0))
    return pl.pallas_call(
        functools.partial(_ffn_down_kernel, alpha=alpha),
        grid=(n // tm,),
        in_specs=[pl.BlockSpec((tm, d), lambda i: (i, 0)), pl.BlockSpec((tm, width), lambda i: (i, 0)),
                  pl.BlockSpec((None, width, d), lambda i: (layer, 0, 0)), vec, vec],
        out_specs=pl.BlockSpec((tm, d), lambda i: (i, 0)),
        out_shape=jax.ShapeDtypeStruct((n, d), F32),
        compiler_params=_params("arbitrary"),
        name="ffn_down",
    )(x, h, w, g, b)


def _pad_last(a, width):
    return jnp.pad(a, [(0, 0)] * (a.ndim - 1) + [(0, width - a.shape[-1])])


def kernel(x, w_in, gdn_conv, gdn_a_log, gdn_dt_bias, gdn_norm, diff_lambda, diff_norm, w_out,
           ln1_g, ln1_b, w_up, ffn_conv, w_down, ln2_g, ln2_b):
    batch, seq, d = x.shape
    depth = w_in.shape[0]
    d_ff = w_down.shape[1]
    ff_pad = -(-d_ff // LANES) * LANES
    alpha = (2 * depth) ** 0.25

    bd0 = A_W
    bc0 = A_W + 2 * GDN_HEADS
    wa = w_in[:, :, :A_W].astype(BF16)
    wbd = _pad_last(w_in[:, :, bd0:bc0], LANES).astype(BF16)
    wbc = w_in[:, :, bc0:].astype(BF16)
    scale = jnp.concatenate([
        jnp.full((DSW_W,), DSW_HEAD_DIM ** -0.5 * LOG2E, F32), jnp.ones((2 * DSW_W,), F32),
        jnp.full((DIFF_QK_W,), DIFF_QK_DIM ** -0.5 * LOG2E, F32), jnp.ones((DIFF_QK_W + DIFF_W,), F32)])[None, :]
    lane_row = lambda v, off: jnp.pad(v, ((0, 0), (off, LANES - off - v.shape[1])))[:, None, :]
    alog_row = lane_row(gdn_a_log, GDN_HEADS)
    dtb_row = lane_row(gdn_dt_bias, GDN_HEADS)
    gdn_norm_row = gdn_norm[:, None, :]
    diff_norm_row = jnp.tile(diff_norm, (1, DIFF_HEADS))[:, None, :]
    lam_init = jnp.asarray([0.8 - 0.6 * math.exp(-0.3 * l) for l in range(depth)], F32)[:, None, None]
    w_out_b = w_out.astype(BF16)
    wg = _pad_last(w_up[:, :, :d_ff], ff_pad).astype(BF16)
    wv = _pad_last(w_up[:, :, d_ff:], ff_pad).astype(BF16)
    cg = _pad_last(ffn_conv[:, :, :d_ff], ff_pad)
    cv = _pad_last(ffn_conv[:, :, d_ff:], ff_pad)
    w_down_b = jnp.pad(w_down, ((0, 0), (0, ff_pad - d_ff), (0, 0))).astype(BF16)
    vec3 = lambda v: v[:, None, :]

    xf = x.reshape(batch * seq, d)
    for l in range(depth):
        a, bc, bd = _in_proj(xf, wa, wbc, wbd, scale, l)
        bc3 = bc.reshape(batch, seq, BC_W)
        ya = _gdn(a.reshape(batch, seq, A_W), bd.reshape(batch, seq, LANES), gdn_conv,
                  alog_row, dtb_row, gdn_norm_row, l)
        yb = _dsw(bc3)
        yc = _diff(bc3, diff_lambda, lam_init, diff_norm_row, l)
        n = batch * seq
        x1 = _out_ln(xf, ya.reshape(n, GDN_W), yb.reshape(n, DSW_W), yc.reshape(n, DIFF_W),
                     w_out_b, vec3(ln1_g), vec3(ln1_b), l, alpha)
        h = _ffn_up(x1, wg, wv, cg, cv, l, seq)
        xf = _ffn_down(x1, h, w_down_b, vec3(ln2_g), vec3(ln2_b), l, alpha)
    return xf.reshape(batch, seq, d)
```

```python
import functools
import math

import jax
import jax.numpy as jnp
from jax import lax
from jax.experimental import pallas as pl
from jax.experimental.pallas import tpu as pltpu

F32 = jnp.float32
BF16 = jnp.bfloat16

LANES = 128
SUBLANES = 8
VMEM_LIMIT = 48 * 1024 * 1024

GDN_HEADS = 4
GDN_HEAD_DIM = 128
GDN_CONV = 4
DSW_HEADS = 4
DSW_HEAD_DIM = 64
DIFF_HEADS = 4
DIFF_QK_DIM = 32
DIFF_V_DIM = 64
FFN_CONV = 3
EPS = 1e-5

GDN_W = GDN_HEADS * GDN_HEAD_DIM
DSW_W = DSW_HEADS * DSW_HEAD_DIM
DIFF_W = DIFF_HEADS * DIFF_V_DIM
DIFF_QK_W = DIFF_HEADS * 2 * DIFF_QK_DIM
A_W = 4 * GDN_W
BC_W = 3 * DSW_W + 2 * DIFF_QK_W + DIFF_W

ROW_TILE = 512
LN_ROWS = 128
PROJ_CHUNK = 256
ATT_Q_TILE = 512
ATT_K_TILE = 512
ATT_AUG = 16
LOG2E = math.log2(math.e)
GDN_CHUNK = 128
GDN_STEP_HEADS = 2
GDN_GROUP = 4
GDN_SPLIT_SPAN = 16
DSW_MAX_BAND = 512
DSW_FAR_TILE = -(-(DSW_MAX_BAND + ATT_K_TILE) // ATT_Q_TILE)


def _dot(a, b):
    return jnp.dot(a.astype(BF16), b.astype(BF16), preferred_element_type=F32)


def _dot_nt(a, b):
    return lax.dot_general(a.astype(BF16), b.astype(BF16), (((1,), (1,)), ((), ())),
                           preferred_element_type=F32)


def _split_bf16(a):
    hi = a.astype(BF16)
    return hi, (a - hi.astype(F32)).astype(BF16)


def _split3_bf16(a):
    hi, mid = _split_bf16(a)
    return hi, mid, (a - hi.astype(F32) - mid.astype(F32)).astype(BF16)


def _dot_split(a, b):
    ah, al = _split_bf16(a)
    bh, bl = _split_bf16(b)
    return jnp.dot(jnp.concatenate([ah, al, ah], axis=1), jnp.concatenate([bh, bh, bl], axis=0),
                   preferred_element_type=F32)


def _sigmoid(x):
    return 1.0 / (1.0 + jnp.exp(-x))


def _params(*sem):
    return pltpu.CompilerParams(dimension_semantics=sem, vmem_limit_bytes=VMEM_LIMIT)


def _in_proj_kernel(x_ref, wa_ref, wbc_ref, wbd_ref, scale_ref, cw_ref, a_ref, bc_ref, bd_ref,
                    work, carry, *, tiles_per_seq):
    tm = x_ref.shape[0]
    x = x_ref[...].astype(BF16)

    @pl.when(pl.program_id(0) % tiles_per_seq == 0)
    def _():
        carry[...] = jnp.zeros(carry.shape, F32)

    first = SUBLANES - (GDN_CONV - 1)

    def conv_silu(u, cols):
        work[0:SUBLANES, :] = carry[:, cols]
        work[SUBLANES:SUBLANES + tm, :] = u
        carry[:, cols] = work[tm:tm + SUBLANES, :]
        cw = cw_ref[:, cols]
        acc = u * cw[GDN_CONV - 1:GDN_CONV, :]
        for j in range(GDN_CONV - 1):
            acc = acc + work[first + j:first + j + tm, :] * cw[j:j + 1, :]
        return acc * _sigmoid(acc)

    def l2norm_heads(y, scale):
        heads = [y[:, h * GDN_HEAD_DIM:(h + 1) * GDN_HEAD_DIM] for h in range(y.shape[1] // GDN_HEAD_DIM)]
        return jnp.concatenate(
            [v * (lax.rsqrt(jnp.sum(v * v, axis=-1, keepdims=True) + 1e-6) * scale) for v in heads], axis=1)

    def finish_q(u, cols):
        a_ref[:, cols] = l2norm_heads(conv_silu(u, cols), GDN_HEAD_DIM ** -0.5)

    def finish_k(u, cols):
        a_ref[:, cols] = l2norm_heads(conv_silu(u, cols), 1.0)

    def finish_v(u, cols):
        a_ref[:, cols] = conv_silu(u, cols)

    def finish_z(u, cols):
        a_ref[:, cols] = u

    def finish_bc(u, cols):
        bc_ref[:, cols] = (u * scale_ref[:, cols]).astype(BF16)

    def finish_bd(u, cols):
        bd_ref[...] = u

    a_jobs = lambda j, f: [(wa_ref, slice(j * GDN_W + c, j * GDN_W + c + PROJ_CHUNK), f)
                           for c in range(0, GDN_W, PROJ_CHUNK)]
    heavy = a_jobs(0, finish_q) + a_jobs(1, finish_k) + a_jobs(2, finish_v)
    light = ([(wbc_ref, slice(c, c + PROJ_CHUNK), finish_bc) for c in range(0, BC_W, PROJ_CHUNK)]
             + a_jobs(3, finish_z) + [(wbd_ref, slice(0, LANES), finish_bd)])
    jobs = [job for pair in zip(heavy, light) for job in pair] + light[len(heavy):]
    project = lambda job: jnp.dot(x, job[0][:, job[1]], preferred_element_type=F32)
    ahead = project(jobs[0])
    for n, job in enumerate(jobs):
        u = ahead
        if n + 1 < len(jobs):
            ahead = project(jobs[n + 1])
        job[2](u, job[1])


def _in_proj(x, wa, wbc, wbd, scale, conv_w, layer, seq):
    n, d = x.shape
    tm = min(ROW_TILE, seq)
    return pl.pallas_call(
        functools.partial(_in_proj_kernel, tiles_per_seq=seq // tm),
        grid=(n // tm,),
        in_specs=[
            pl.BlockSpec((tm, d), lambda i: (i, 0)),
            pl.BlockSpec((None, d, A_W), lambda i: (layer, 0, 0)),
            pl.BlockSpec((None, d, BC_W), lambda i: (layer, 0, 0)),
            pl.BlockSpec((None, d, LANES), lambda i: (layer, 0, 0)),
            pl.BlockSpec((1, BC_W), lambda i: (0, 0)),
            pl.BlockSpec((None, GDN_CONV, 3 * GDN_W), lambda i: (layer, 0, 0)),
        ],
        out_specs=[
            pl.BlockSpec((tm, A_W), lambda i: (i, 0)),
            pl.BlockSpec((tm, BC_W), lambda i: (i, 0)),
            pl.BlockSpec((tm, LANES), lambda i: (i, 0)),
        ],
        out_shape=[
            jax.ShapeDtypeStruct((n, A_W), F32),
            jax.ShapeDtypeStruct((n, BC_W), BF16),
            jax.ShapeDtypeStruct((n, LANES), F32),
        ],
        scratch_shapes=[pltpu.VMEM((tm + SUBLANES, PROJ_CHUNK), F32), pltpu.VMEM((SUBLANES, 3 * GDN_W), F32)],
        compiler_params=_params("arbitrary"),
        name="in_proj",
    )(x, wa, wbc, wbd, scale, conv_w)


def _gdn_kernel(q_ref, k_ref, v_ref, z_ref, bd_ref, alog_ref, dtb_ref, nw_ref, y_ref,
                beta_s, g_s, sa_s, sb_s, oq_s, oc_s, gl_s, *, seq):
    C = GDN_CHUNK
    n_chunks = seq // C
    n_heads = q_ref.shape[1] // LANES
    head0 = pl.program_id(1) * n_heads
    lanes_of = lambda h: slice(h * LANES, (h + 1) * LANES)

    bd = bd_ref[...]
    lane = lax.broadcasted_iota(jnp.int32, (seq, LANES), 1)
    beta_all = _sigmoid(bd)
    xg = bd + dtb_ref[...]
    softplus = jnp.maximum(xg, 0.0) + jnp.log(1.0 + jnp.exp(-jnp.abs(xg)))
    g_all = -jnp.exp(alog_ref[...]) * softplus
    for h in range(n_heads):
        beta = jnp.sum(jnp.where(lane == head0 + h, beta_all, 0.0), axis=-1, keepdims=True)
        g = jnp.sum(jnp.where(lane == head0 + h + GDN_HEADS, g_all, 0.0), axis=-1, keepdims=True)
        beta_s[h] = jnp.broadcast_to(beta, (seq, LANES))
        g_s[h] = jnp.broadcast_to(g, (seq, LANES))

    ri = lax.broadcasted_iota(jnp.int32, (C, C), 0)
    ci = lax.broadcasted_iota(jnp.int32, (C, C), 1)
    lower_incl = ri >= ci
    strict = ri > ci
    tri = jnp.where(lower_incl, 1.0, 0.0).astype(BF16)
    tri3 = jnp.concatenate([tri, tri, tri], axis=1)

    def chunk_group(gi, carry):
        items = [(h, gi * GDN_GROUP + j) for j in range(GDN_GROUP) for h in range(n_heads)]
        grp = range(len(items))
        rows = [pl.ds(pl.multiple_of(c * C, C), C) for _, c in items]
        qc = [q_ref[r, lanes_of(h)] for (h, _), r in zip(items, rows)]
        kc = [k_ref[r, lanes_of(h)] for (h, _), r in zip(items, rows)]
        vc = [v_ref[r, lanes_of(h)] for (h, _), r in zip(items, rows)]
        bb = [beta_s[h, r, :] for (h, _), r in zip(items, rows)]
        gg = [g_s[h, r, :] for (h, _), r in zip(items, rows)]
        cum = [jnp.dot(tri3, jnp.concatenate(_split3_bf16(g), axis=0), preferred_element_type=F32)
               for g in gg]
        decay = [jnp.exp(jnp.where(lower_incl, c - c.T, -jnp.inf)) for c in cum]
        kb = [k * b for k, b in zip(kc, bb)]
        scores = [_dot_nt(jnp.concatenate([x, q], axis=0), k) for x, q, k in zip(kb, qc, kc)]
        m = [jnp.where(strict, s[0:C, :] * d, 0.0) for s, d in zip(scores, decay)]
        qk = [jnp.where(lower_incl, s[C:, :] * d, 0.0) for s, d in zip(scores, decay)]
        nmat = [-x for x in m]
        p = [_dot_split(x, x) for x in m]
        span = 2
        while 2 * span < C:
            dot = _dot_split if span < GDN_SPLIT_SPAN else _dot
            both = [dot(jnp.concatenate([n, x], axis=0), x) for n, x in zip(nmat, p)]
            nmat = [n + x + b[0:C, :] for n, x, b in zip(nmat, p, both)]
            p = [b[C:, :] for b in both]
            span *= 2
        nmat = [n + x + _dot(n, x) for n, x in zip(nmat, p)]
        ecum = [jnp.exp(c) for c in cum]
        vb = [v * b for v, b in zip(vc, bb)]
        kbe = [x * e for x, e in zip(kb, ecum)]
        wu = [jnp.concatenate([x, y], axis=1) for x, y in zip(kbe, vb)]
        wu = [x + _dot(n, x) for n, x in zip(nmat, wu)]
        qd = [q * e for q, e in zip(qc, ecum)]
        cum_last = [c[C - 1:C, :] for c in cum]
        ktt = [(k * jnp.exp(cl - c)).T for k, cl, c in zip(kc, cum_last, cum)]
        prod = [_dot(jnp.concatenate([kt, a], axis=0), x) for kt, a, x in zip(ktt, qk, wu)]
        for j, ((h, c), r) in enumerate(zip(items, rows)):
            sa_s[h, r, :] = -prod[j][0:C, 0:C]
            sb_s[h, r, :] = prod[j][0:C, C:]
            oq_s[h, r, :] = qd[j] - prod[j][C:, 0:C]
            oc_s[h, r, :] = prod[j][C:, C:]
            gl_s[h, pl.ds(pl.multiple_of(c * SUBLANES, SUBLANES), SUBLANES), :] = (
                jnp.broadcast_to(jnp.exp(cum_last[j]), (SUBLANES, LANES)))
        return carry

    lax.fori_loop(0, n_chunks // GDN_GROUP, chunk_group, 0)

    nw = nw_ref[...]

    def scan_step(c, states):
        rows = pl.ds(pl.multiple_of(c * C, C), C)
        new = []
        for h, s in enumerate(states):
            gl = gl_s[h, pl.ds(pl.multiple_of(c * SUBLANES, SUBLANES), 1), :]
            new.append(s * gl + (_dot(sa_s[h, rows, :], s) + sb_s[h, rows, :]))
        for h, s in enumerate(states):
            o = _dot(oq_s[h, rows, :], s) + oc_s[h, rows, :]
            z = z_ref[rows, lanes_of(h)]
            o = o * lax.rsqrt(jnp.mean(o * o, axis=-1, keepdims=True) + EPS) * nw
            y_ref[rows, lanes_of(h)] = (o * (z * _sigmoid(z))).astype(y_ref.dtype)
        return tuple(new)

    lax.fori_loop(0, n_chunks, scan_step,
                  tuple(jnp.zeros((GDN_HEAD_DIM, GDN_HEAD_DIM), F32) for _ in range(n_heads)))


def _gdn(a, bd, alog_row, dtb_row, norm_row, layer):
    b, t, _ = a.shape
    width = GDN_STEP_HEADS * GDN_HEAD_DIM
    steps = GDN_HEADS // GDN_STEP_HEADS
    tok = lambda j: pl.BlockSpec((None, t, width), lambda i, p: (i, 0, j * steps + p))
    row = pl.BlockSpec((None, 1, LANES), lambda i, p: (layer, 0, 0))
    head_buf = pltpu.VMEM((GDN_STEP_HEADS, t, LANES), F32)
    return pl.pallas_call(
        functools.partial(_gdn_kernel, seq=t),
        grid=(b, steps),
        in_specs=[tok(0), tok(1), tok(2), tok(3),
                  pl.BlockSpec((None, t, LANES), lambda i, p: (i, 0, 0)), row, row, row],
        out_specs=pl.BlockSpec((None, t, width), lambda i, p: (i, 0, p)),
        out_shape=jax.ShapeDtypeStruct((b, t, GDN_W), BF16),
        scratch_shapes=[head_buf] * 6
        + [pltpu.VMEM((GDN_STEP_HEADS, t // GDN_CHUNK * SUBLANES, LANES), F32)],
        compiler_params=_params("arbitrary", "arbitrary"),
        name="gdn",
    )(a, a, a, a, bd, alog_row, dtb_row, norm_row)


def _stage_values(v_ref, vt_s, seq, dv):
    aug = dv + ATT_AUG
    ones_row = jnp.where(lax.broadcasted_iota(jnp.int32, (ATT_AUG, ATT_K_TILE), 0) == 0, 1.0, 0.0).astype(BF16)
    for g in range(v_ref.shape[1] // LANES):
        for t in range(seq // ATT_K_TILE):
            for h in range(LANES // dv):
                vt_s[g, t, h * aug + dv:(h + 1) * aug, :] = ones_row
        for c in range(seq // LANES):
            t, off = divmod(c * LANES, ATT_K_TILE)
            blk = v_ref[c * LANES:(c + 1) * LANES, g * LANES:(g + 1) * LANES].astype(F32).T.astype(BF16)
            for h in range(LANES // dv):
                vt_s[g, t, h * aug:h * aug + dv, off:off + LANES] = blk[h * dv:(h + 1) * dv, :]


def _stack_masked(q, width):
    lane = lax.broadcasted_iota(jnp.int32, q.shape, 1)
    return jnp.concatenate(
        [jnp.where((lane >= j * width) & (lane < (j + 1) * width), q, 0.0).astype(BF16)
         for j in range(LANES // width)], axis=0)


def _softmax_reset(m_s, acc_s):
    m_s[...] = jnp.full(m_s.shape, -jnp.inf, F32)
    acc_s[...] = jnp.zeros(acc_s.shape, F32)


def _softmax_tile(m_s, acc_s, streams, scores, cnt, vt):
    ps, alphas = [], []
    for i, s in zip(streams, scores):
        m_old = m_s[i, 0:1, :]
        m_new = jnp.maximum(m_old, jnp.max(s, axis=0, keepdims=True))
        p = jnp.exp2(s - m_new)
        if cnt is not None:
            p = p * cnt
        alphas.append(jnp.exp2(m_old - m_new))
        m_s[i, 0:1, :] = m_new
        ps.append(p.astype(BF16))
    pv = jnp.dot(vt, jnp.concatenate(ps, axis=1), preferred_element_type=F32)
    width = pv.shape[1] // len(streams)
    for n, i in enumerate(streams):
        acc_s[i] = alphas[n] * acc_s[i] + pv[:, n * width:(n + 1) * width]


def _dsw_kernel(q_ref, k_ref, v_ref, o_ref, vt_s, cnt_s, bias_s, m_s, acc_s, *, seq):
    tq, tk = ATT_Q_TILE, ATT_K_TILE
    blocks = q_ref.shape[1] // LANES
    dv = DSW_HEAD_DIM
    heads = LANES // dv
    aug = dv + ATT_AUG
    _stage_values(v_ref, vt_s, seq, dv)

    @pl.when(pl.program_id(0) == 0)
    def _():
        kr = lax.broadcasted_iota(jnp.int32, (tk, tq), 0)
        qc = lax.broadcasted_iota(jnp.int32, (tk, tq), 1)
        for d in range(DSW_FAR_TILE + 1):
            delta = d * tq + qc - kr
            causal = delta >= 0
            cnt = (jnp.where(causal & (delta <= 128), 1.0, 0.0)
                   + jnp.where(causal & (delta <= DSW_MAX_BAND) & ((delta & 3) == 0), 1.0, 0.0)
                   + jnp.where(causal & ((delta & 15) == 0), 1.0, 0.0))
            cnt_s[d] = cnt.astype(F32)
            bias_s[d] = jnp.where(cnt > 0.0, 0.0, -jnp.inf).astype(F32)

    def q_tile(qi, carry):
        qrows = pl.ds(pl.multiple_of(qi * tq, tq), tq)
        qstack = [_stack_masked(q_ref[qrows, g * LANES:(g + 1) * LANES].astype(F32), DSW_HEAD_DIM)
                  for g in range(blocks)]

        def k_tile(kj, st):
            krows = pl.ds(pl.multiple_of(kj * tk, tk), tk)
            far = jnp.minimum(qi - kj * (tk // tq), DSW_FAR_TILE)
            cnt = cnt_s[far]
            bias = bias_s[far]
            s = [_dot_nt(k_ref[krows, g * LANES:(g + 1) * LANES], qstack[g]) for g in range(blocks)]
            for g in range(blocks):
                for h in range(heads):
                    _softmax_tile(m_s, acc_s, [g * heads + h], [s[g][:, h * tq:(h + 1) * tq] + bias], cnt,
                                  vt_s[g, kj, h * aug:(h + 1) * aug, :])
            return st

        _softmax_reset(m_s, acc_s)
        n_k = (qi * tq) // tk + 1
        lax.fori_loop(0, n_k, k_tile, 0)
        o = jnp.concatenate([acc_s[i, 0:dv, :] / acc_s[i, dv:dv + 1, :] for i in range(blocks * heads)],
                            axis=0)
        o_ref[qrows, :] = o.T.astype(o_ref.dtype)
        return carry

    lax.fori_loop(0, seq // tq, q_tile, 0)


def _dsw(bc):
    b, t, _ = bc.shape
    spec = lambda j: pl.BlockSpec((None, t, DSW_W), lambda i: (i, 0, j))
    return pl.pallas_call(
        functools.partial(_dsw_kernel, seq=t),
        grid=(b,),
        in_specs=[spec(0), spec(1), spec(2)],
        out_specs=pl.BlockSpec((None, t, DSW_W), lambda i: (i, 0, 0)),
        out_shape=jax.ShapeDtypeStruct((b, t, DSW_W), BF16),
        scratch_shapes=[pltpu.VMEM((DSW_W // LANES, t // ATT_K_TILE,
                                    LANES // DSW_HEAD_DIM * (DSW_HEAD_DIM + ATT_AUG), ATT_K_TILE), BF16)]
        + [pltpu.VMEM((DSW_FAR_TILE + 1, ATT_K_TILE, ATT_Q_TILE), F32)] * 2
        + [pltpu.VMEM((DSW_HEADS, SUBLANES, ATT_Q_TILE), F32),
           pltpu.VMEM((DSW_HEADS, DSW_HEAD_DIM + ATT_AUG, ATT_Q_TILE), F32)],
        compiler_params=_params("arbitrary"),
        name="dsw",
    )(bc, bc, bc)


def _diff_kernel(q_ref, k_ref, v_ref, lam_ref, laminit_ref, nw_ref, o_ref, vt_s, m_s, acc_s, *, seq):
    tq, tk = ATT_Q_TILE, ATT_K_TILE
    blocks = q_ref.shape[1] // LANES
    dv = DIFF_V_DIM
    heads = LANES // dv
    aug = dv + ATT_AUG
    _stage_values(v_ref, vt_s, seq, dv)
    kr = lax.broadcasted_iota(jnp.int32, (tk, tq), 0)
    qc = lax.broadcasted_iota(jnp.int32, (tk, tq), 1)
    lv = lam_ref[...]
    lam_init = laminit_ref[...]
    lam = (jnp.exp(jnp.sum(lv[0:1, :] * lv[1:2, :], keepdims=True))
           - jnp.exp(jnp.sum(lv[2:3, :] * lv[3:4, :], keepdims=True)) + lam_init)
    nw = nw_ref[...]

    def q_tile(qi, carry):
        qrows = pl.ds(pl.multiple_of(qi * tq, tq), tq)
        qstack = [_stack_masked(q_ref[qrows, g * LANES:(g + 1) * LANES].astype(F32), DIFF_QK_DIM)
                  for g in range(blocks)]

        def k_tile(kj, st, valid):
            krows = pl.ds(pl.multiple_of(kj * tk, tk), tk)
            s = [_dot_nt(k_ref[krows, g * LANES:(g + 1) * LANES], qstack[g])
                 for g in range(blocks)]
            for g in range(blocks):
                for h in range(heads):
                    scores = [s[g][:, (2 * h + mp) * tq:(2 * h + mp + 1) * tq] for mp in range(2)]
                    if valid is not None:
                        scores = [jnp.where(valid, x, -jnp.inf) for x in scores]
                    first = 2 * (heads * g + h)
                    _softmax_tile(m_s, acc_s, [first, first + 1], scores, None,
                                  vt_s[g, kj, h * aug:(h + 1) * aug, :])
            return st

        _softmax_reset(m_s, acc_s)
        n_full = (qi * tq) // tk
        lax.fori_loop(0, n_full, functools.partial(k_tile, valid=None), 0)
        k_tile(n_full, 0, kr + (n_full * tk - qi * tq) <= qc)
        att = [acc_s[i, 0:dv, :] / acc_s[i, dv:dv + 1, :] for i in range(2 * heads * blocks)]
        halves = []
        for h in range(heads * blocks):
            o = att[2 * h] - lam * att[2 * h + 1]
            halves.append(o * lax.rsqrt(jnp.mean(o * o, axis=0, keepdims=True) + EPS))
        o = jnp.concatenate(halves, axis=0).T
        o_ref[qrows, :] = (o * nw * (1.0 - lam_init)).astype(o_ref.dtype)
        return carry

    lax.fori_loop(0, seq // tq, q_tile, 0)


def _diff(bc, lam_vecs, lam_init, norm_row, layer):
    b, t, _ = bc.shape
    base = 3 * DSW_W // DIFF_W
    spec = lambda j: pl.BlockSpec((None, t, DIFF_W), lambda i: (i, 0, base + j))
    return pl.pallas_call(
        functools.partial(_diff_kernel, seq=t),
        grid=(b,),
        in_specs=[spec(0), spec(1), spec(2),
                  pl.BlockSpec((None, 4, DIFF_QK_DIM), lambda i: (layer, 0, 0)),
                  pl.BlockSpec((None, 1, 1), lambda i: (layer, 0, 0)),
                  pl.BlockSpec((None, 1, DIFF_W), lambda i: (layer, 0, 0))],
        out_specs=pl.BlockSpec((None, t, DIFF_W), lambda i: (i, 0, 0)),
        out_shape=jax.ShapeDtypeStruct((b, t, DIFF_W), BF16),
        scratch_shapes=[pltpu.VMEM((DIFF_W // LANES, t // ATT_K_TILE,
                                    LANES // DIFF_V_DIM * (DIFF_V_DIM + ATT_AUG), ATT_K_TILE), BF16),
                        pltpu.VMEM((2 * DIFF_HEADS, SUBLANES, ATT_Q_TILE), F32),
                        pltpu.VMEM((2 * DIFF_HEADS, DIFF_V_DIM + ATT_AUG, ATT_Q_TILE), F32)],
        compiler_params=_params("arbitrary"),
        name="diff",
    )(bc, bc, bc, lam_vecs, lam_init, norm_row)


def _layer_norm(h, g, b):
    mu = jnp.mean(h, axis=-1, keepdims=True)
    hc = h - mu
    var = jnp.mean(hc * hc, axis=-1, keepdims=True)
    return hc * lax.rsqrt(var + EPS) * g + b


def _out_ln_kernel(x_ref, ya_ref, yb_ref, yc_ref, w_ref, g_ref, b_ref, o_ref, *, alpha):
    def project(r):
        rows = slice(r, r + LN_ROWS)
        mixed = jnp.concatenate([ya_ref[rows, :], yb_ref[rows, :], yc_ref[rows, :]], axis=1)
        return jnp.dot(mixed, w_ref[...], preferred_element_type=F32)

    tm = x_ref.shape[0]
    ahead = project(0)
    for r in range(0, tm, LN_ROWS):
        y = ahead
        if r + LN_ROWS < tm:
            ahead = project(r + LN_ROWS)
        o_ref[r:r + LN_ROWS, :] = _layer_norm(alpha * x_ref[r:r + LN_ROWS, :] + y, g_ref[...], b_ref[...])


def _out_ln(x, ya, yb, yc, w, g, b, layer, alpha):
    n, d = x.shape
    tm = min(ROW_TILE, n)
    rows = lambda width: pl.BlockSpec((tm, width), lambda i: (i, 0))
    vec = pl.BlockSpec((None, 1, d), lambda i: (layer, 0, 0))
    return pl.pallas_call(
        functools.partial(_out_ln_kernel, alpha=alpha),
        grid=(n // tm,),
        in_specs=[rows(d), rows(GDN_W), rows(DSW_W), rows(DIFF_W),
                  pl.BlockSpec((None, d, d), lambda i: (layer, 0, 0)), vec, vec],
        out_specs=rows(d),
        out_shape=jax.ShapeDtypeStruct((n, d), F32),
        compiler_params=_params("arbitrary"),
        name="out_ln",
    )(x, ya, yb, yc, w, g, b)


def _ffn_up_kernel(x_ref, wg_ref, wv_ref, cg_ref, cv_ref, h_ref, work_g, work_v, carry_g, carry_v,
                   *, tiles_per_seq, col_chunk):
    tm = x_ref.shape[0]
    width = h_ref.shape[1]

    @pl.when(pl.program_id(0) % tiles_per_seq == 0)
    def _():
        carry_g[...] = jnp.zeros(carry_g.shape, F32)
        carry_v[...] = jnp.zeros(carry_v.shape, F32)

    x = x_ref[...].astype(BF16)
    first = SUBLANES - (FFN_CONV - 1)

    def conv(u, cw_ref, work, carry, cols):
        work[0:SUBLANES, :] = carry[:, cols]
        work[SUBLANES:SUBLANES + tm, :] = u
        carry[:, cols] = work[tm:tm + SUBLANES, :]
        cw = cw_ref[:, cols]
        acc = u * cw[FFN_CONV - 1:FFN_CONV, :]
        for j in range(FFN_CONV - 1):
            acc = acc + work[first + j:first + j + tm, :] * cw[j:j + 1, :]
        return acc

    def project(c):
        cols = slice(c, c + col_chunk)
        return (jnp.dot(x, wg_ref[:, cols], preferred_element_type=F32),
                jnp.dot(x, wv_ref[:, cols], preferred_element_type=F32))

    ahead = project(0)
    for c in range(0, width, col_chunk):
        ug, uv = ahead
        if c + col_chunk < width:
            ahead = project(c + col_chunk)
        cols = slice(c, c + col_chunk)
        gate = conv(ug, cg_ref, work_g, carry_g, cols)
        val = conv(uv, cv_ref, work_v, carry_v, cols)
        h_ref[:, cols] = (gate * _sigmoid(gate) * val).astype(h_ref.dtype)


def _ffn_up(x, wg, wv, cg, cv, layer, seq):
    n, d = x.shape
    width = wg.shape[-1]
    tm = min(ROW_TILE, seq)
    col_chunk = 256
    wspec = pl.BlockSpec((None, d, width), lambda i: (layer, 0, 0))
    cspec = pl.BlockSpec((None, FFN_CONV, width), lambda i: (layer, 0, 0))
    return pl.pallas_call(
        functools.partial(_ffn_up_kernel, tiles_per_seq=seq // tm, col_chunk=col_chunk),
        grid=(n // tm,),
        in_specs=[pl.BlockSpec((tm, d), lambda i: (i, 0)), wspec, wspec, cspec, cspec],
        out_specs=pl.BlockSpec((tm, width), lambda i: (i, 0)),
        out_shape=jax.ShapeDtypeStruct((n, width), BF16),
        scratch_shapes=[pltpu.VMEM((tm + SUBLANES, col_chunk), F32)] * 2
        + [pltpu.VMEM((SUBLANES, width), F32)] * 2,
        compiler_params=_params("arbitrary"),
        name="ffn_up",
    )(x, wg, wv, cg, cv)


def _ffn_down_kernel(x_ref, h_ref, w_ref, g_ref, b_ref, o_ref, *, alpha):
    def project(r):
        return jnp.dot(h_ref[r:r + LN_ROWS, :], w_ref[...], preferred_element_type=F32)

    tm = x_ref.shape[0]
    ahead = project(0)
    for r in range(0, tm, LN_ROWS):
        f = ahead
        if r + LN_ROWS < tm:
            ahead = project(r + LN_ROWS)
        o_ref[r:r + LN_ROWS, :] = _layer_norm(alpha * x_ref[r:r + LN_ROWS, :] + f, g_ref[...], b_ref[...])


def _ffn_down(x, h, w, g, b, layer, alpha):
    n, d = x.shape
    width = h.shape[1]
    tm = min(ROW_TILE, n)
    vec = pl.BlockSpec((None, 1, d), lambda i: (layer, 0, 0))
    return pl.pallas_call(
        functools.partial(_ffn_down_kernel, alpha=alpha),
        grid=(n // tm,),
        in_specs=[pl.BlockSpec((tm, d), lambda i: (i, 0)), pl.BlockSpec((tm, width), lambda i: (i, 0)),
                  pl.BlockSpec((None, width, d), lambda i: (layer, 0, 0)), vec, vec],
        out_specs=pl.BlockSpec((tm, d), lambda i: (i, 0)),
        out_shape=jax.ShapeDtypeStruct((n, d), F32),
        compiler_params=_params("arbitrary"),
        name="ffn_down",
    )(x, h, w, g, b)


def _pad_last(a, width):
    return jnp.pad(a, [(0, 0)] * (a.ndim - 1) + [(0, width - a.shape[-1])])


def kernel(x, w_in, gdn_conv, gdn_a_log, gdn_dt_bias, gdn_norm, diff_lambda, diff_norm, w_out,
           ln1_g, ln1_b, w_up, ffn_conv, w_down, ln2_g, ln2_b):
    batch, seq, d = x.shape
    depth = w_in.shape[0]
    d_ff = w_down.shape[1]
    ff_pad = -(-d_ff // LANES) * LANES
    alpha = (2 * depth) ** 0.25

    bd0 = A_W
    bc0 = A_W + 2 * GDN_HEADS
    wa = w_in[:, :, :A_W].astype(BF16)
    wbd = _pad_last(w_in[:, :, bd0:bc0], LANES).astype(BF16)
    wbc = w_in[:, :, bc0:].astype(BF16)
    scale = jnp.concatenate([
        jnp.full((DSW_W,), DSW_HEAD_DIM ** -0.5 * LOG2E, F32), jnp.ones((2 * DSW_W,), F32),
        jnp.full((DIFF_QK_W,), DIFF_QK_DIM ** -0.5 * LOG2E, F32), jnp.ones((DIFF_QK_W + DIFF_W,), F32)])[None, :]
    lane_row = lambda v, off: jnp.pad(v, ((0, 0), (off, LANES - off - v.shape[1])))[:, None, :]
    alog_row = lane_row(gdn_a_log, GDN_HEADS)
    dtb_row = lane_row(gdn_dt_bias, GDN_HEADS)
    gdn_norm_row = gdn_norm[:, None, :]
    diff_norm_row = jnp.tile(diff_norm, (1, DIFF_HEADS))[:, None, :]
    lam_init = jnp.asarray([0.8 - 0.6 * math.exp(-0.3 * l) for l in range(depth)], F32)[:, None, None]
    w_out_b = w_out.astype(BF16)
    wg = _pad_last(w_up[:, :, :d_ff], ff_pad).astype(BF16)
    wv = _pad_last(w_up[:, :, d_ff:], ff_pad).astype(BF16)
    cg = _pad_last(ffn_conv[:, :, :d_ff], ff_pad)
    cv = _pad_last(ffn_conv[:, :, d_ff:], ff_pad)
    w_down_b = jnp.pad(w_down, ((0, 0), (0, ff_pad - d_ff), (0, 0))).astype(BF16)
    vec3 = lambda v: v[:, None, :]

    xf = x.reshape(batch * seq, d)
    for l in range(depth):
        a, bc, bd = _in_proj(xf, wa, wbc, wbd, scale, gdn_conv, l, seq)
        bc3 = bc.reshape(batch, seq, BC_W)
        ya = _gdn(a.reshape(batch, seq, A_W), bd.reshape(batch, seq, LANES),
                  alog_row, dtb_row, gdn_norm_row, l)
        yb = _dsw(bc3)
        yc = _diff(bc3, diff_lambda, lam_init, diff_norm_row, l)
        n = batch * seq
        x1 = _out_ln(xf, ya.reshape(n, GDN_W), yb.reshape(n, DSW_W), yc.reshape(n, DIFF_W),
                     w_out_b, vec3(ln1_g), vec3(ln1_b), l, alpha)
        h = _ffn_up(x1, wg, wv, cg, cv, l, seq)
        xf = _ffn_down(x1, h, w_down_b, vec3(ln2_g), vec3(ln2_b), l, alpha)
    return xf.reshape(batch, seq, d)
```

```python
import functools
import math

import jax
import jax.numpy as jnp
from jax import lax
from jax.experimental import pallas as pl
from jax.experimental.pallas import tpu as pltpu

F32 = jnp.float32
BF16 = jnp.bfloat16

LANES = 128
SUBLANES = 8
VMEM_LIMIT = 48 * 1024 * 1024

GDN_HEADS = 4
GDN_HEAD_DIM = 128
GDN_CONV = 4
DSW_HEADS = 4
DSW_HEAD_DIM = 64
DIFF_HEADS = 4
DIFF_QK_DIM = 32
DIFF_V_DIM = 64
FFN_CONV = 3
EPS = 1e-5

GDN_W = GDN_HEADS * GDN_HEAD_DIM
DSW_W = DSW_HEADS * DSW_HEAD_DIM
DIFF_W = DIFF_HEADS * DIFF_V_DIM
DIFF_QK_W = DIFF_HEADS * 2 * DIFF_QK_DIM
A_W = 4 * GDN_W
BC_W = 3 * DSW_W + 2 * DIFF_QK_W + DIFF_W

ROW_TILE = 512
LN_ROWS = 128
PROJ_CHUNK = 256
ATT_Q_TILE = 512
ATT_K_TILE = 512
ATT_AUG = 16
LOG2E = math.log2(math.e)
GDN_CHUNK = 128
GDN_STEP_HEADS = 2
GDN_GROUP = 4
GDN_SPLIT_SPAN = 16
DSW_MAX_BAND = 512
DSW_FAR_TILE = -(-(DSW_MAX_BAND + ATT_K_TILE) // ATT_Q_TILE)


def _dot(a, b):
    return jnp.dot(a.astype(BF16), b.astype(BF16), preferred_element_type=F32)


def _dot_nt(a, b):
    return lax.dot_general(a.astype(BF16), b.astype(BF16), (((1,), (1,)), ((), ())),
                           preferred_element_type=F32)


def _split_bf16(a):
    hi = a.astype(BF16)
    return hi, (a - hi.astype(F32)).astype(BF16)


def _split3_bf16(a):
    hi, mid = _split_bf16(a)
    return hi, mid, (a - hi.astype(F32) - mid.astype(F32)).astype(BF16)


def _dot_split(a, b):
    ah, al = _split_bf16(a)
    bh, bl = _split_bf16(b)
    return jnp.dot(jnp.concatenate([ah, al, ah], axis=1), jnp.concatenate([bh, bh, bl], axis=0),
                   preferred_element_type=F32)


def _sigmoid(x):
    return 1.0 / (1.0 + jnp.exp(-x))


def _params(*sem):
    return pltpu.CompilerParams(dimension_semantics=sem, vmem_limit_bytes=VMEM_LIMIT)


def _in_proj_kernel(x_ref, wa_ref, wbc_ref, wbd_ref, scale_ref, cw_ref, a_ref, bc_ref, bd_ref,
                    work, carry, *, tiles_per_seq):
    tm = x_ref.shape[0]
    x = x_ref[...].astype(BF16)

    @pl.when(pl.program_id(0) % tiles_per_seq == 0)
    def _():
        carry[...] = jnp.zeros(carry.shape, F32)

    first = SUBLANES - (GDN_CONV - 1)

    def conv_silu(u, cols):
        work[0:SUBLANES, :] = carry[:, cols]
        work[SUBLANES:SUBLANES + tm, :] = u
        carry[:, cols] = work[tm:tm + SUBLANES, :]
        cw = cw_ref[:, cols]
        acc = u * cw[GDN_CONV - 1:GDN_CONV, :]
        for j in range(GDN_CONV - 1):
            acc = acc + work[first + j:first + j + tm, :] * cw[j:j + 1, :]
        return acc * _sigmoid(acc)

    def l2norm_heads(y, scale):
        heads = [y[:, h * GDN_HEAD_DIM:(h + 1) * GDN_HEAD_DIM] for h in range(y.shape[1] // GDN_HEAD_DIM)]
        return jnp.concatenate(
            [v * (lax.rsqrt(jnp.sum(v * v, axis=-1, keepdims=True) + 1e-6) * scale) for v in heads], axis=1)

    def finish_q(u, cols):
        a_ref[:, cols] = l2norm_heads(conv_silu(u, cols), GDN_HEAD_DIM ** -0.5)

    def finish_k(u, cols):
        a_ref[:, cols] = l2norm_heads(conv_silu(u, cols), 1.0)

    def finish_v(u, cols):
        a_ref[:, cols] = conv_silu(u, cols)

    def finish_z(u, cols):
        a_ref[:, cols] = u

    def finish_bc(u, cols):
        bc_ref[:, cols] = (u * scale_ref[:, cols]).astype(BF16)

    def finish_bd(u, cols):
        bd_ref[...] = u

    a_jobs = lambda j, f: [(wa_ref, slice(j * GDN_W + c, j * GDN_W + c + PROJ_CHUNK), f)
                           for c in range(0, GDN_W, PROJ_CHUNK)]
    heavy = a_jobs(0, finish_q) + a_jobs(1, finish_k) + a_jobs(2, finish_v)
    light = ([(wbc_ref, slice(c, c + PROJ_CHUNK), finish_bc) for c in range(0, BC_W, PROJ_CHUNK)]
             + a_jobs(3, finish_z) + [(wbd_ref, slice(0, LANES), finish_bd)])
    jobs = [job for pair in zip(heavy, light) for job in pair] + light[len(heavy):]
    project = lambda job: jnp.dot(x, job[0][:, job[1]], preferred_element_type=F32)
    ahead = project(jobs[0])
    for n, job in enumerate(jobs):
        u = ahead
        if n + 1 < len(jobs):
            ahead = project(jobs[n + 1])
        job[2](u, job[1])


def _in_proj(x, wa, wbc, wbd, scale, conv_w, layer, seq):
    n, d = x.shape
    tm = min(ROW_TILE, seq)
    return pl.pallas_call(
        functools.partial(_in_proj_kernel, tiles_per_seq=seq // tm),
        grid=(n // tm,),
        in_specs=[
            pl.BlockSpec((tm, d), lambda i: (i, 0)),
            pl.BlockSpec((None, d, A_W), lambda i: (layer, 0, 0)),
            pl.BlockSpec((None, d, BC_W), lambda i: (layer, 0, 0)),
            pl.BlockSpec((None, d, LANES), lambda i: (layer, 0, 0)),
            pl.BlockSpec((1, BC_W), lambda i: (0, 0)),
            pl.BlockSpec((None, GDN_CONV, 3 * GDN_W), lambda i: (layer, 0, 0)),
        ],
        out_specs=[
            pl.BlockSpec((tm, A_W), lambda i: (i, 0)),
            pl.BlockSpec((tm, BC_W), lambda i: (i, 0)),
            pl.BlockSpec((tm, LANES), lambda i: (i, 0)),
        ],
        out_shape=[
            jax.ShapeDtypeStruct((n, A_W), F32),
            jax.ShapeDtypeStruct((n, BC_W), BF16),
            jax.ShapeDtypeStruct((n, LANES), F32),
        ],
        scratch_shapes=[pltpu.VMEM((tm + SUBLANES, PROJ_CHUNK), F32), pltpu.VMEM((SUBLANES, 3 * GDN_W), F32)],
        compiler_params=_params("arbitrary"),
        name="in_proj",
    )(x, wa, wbc, wbd, scale, conv_w)


def _gdn_kernel(q_ref, k_ref, v_ref, z_ref, bd_ref, alog_ref, dtb_ref, nw_ref, y_ref,
                beta_s, g_s, sa_s, sb_s, oq_s, oc_s, gl_s, *, seq):
    C = GDN_CHUNK
    n_chunks = seq // C
    n_heads = q_ref.shape[1] // LANES
    head0 = pl.program_id(1) * n_heads
    lanes_of = lambda h: slice(h * LANES, (h + 1) * LANES)

    bd = bd_ref[...]
    lane = lax.broadcasted_iota(jnp.int32, (seq, LANES), 1)
    beta_all = _sigmoid(bd)
    xg = bd + dtb_ref[...]
    softplus = jnp.maximum(xg, 0.0) + jnp.log(1.0 + jnp.exp(-jnp.abs(xg)))
    g_all = -jnp.exp(alog_ref[...]) * softplus
    for h in range(n_heads):
        beta = jnp.sum(jnp.where(lane == head0 + h, beta_all, 0.0), axis=-1, keepdims=True)
        g = jnp.sum(jnp.where(lane == head0 + h + GDN_HEADS, g_all, 0.0), axis=-1, keepdims=True)
        beta_s[h] = jnp.broadcast_to(beta, (seq, LANES))
        g_s[h] = jnp.broadcast_to(g, (seq, LANES))

    ri = lax.broadcasted_iota(jnp.int32, (C, C), 0)
    ci = lax.broadcasted_iota(jnp.int32, (C, C), 1)
    lower_incl = ri >= ci
    strict = ri > ci
    tri = jnp.where(lower_incl, 1.0, 0.0).astype(BF16)
    tri3 = jnp.concatenate([tri, tri, tri], axis=1)

    def chunk_group(gi, carry):
        items = [(h, gi * GDN_GROUP + j) for j in range(GDN_GROUP) for h in range(n_heads)]
        grp = range(len(items))
        rows = [pl.ds(pl.multiple_of(c * C, C), C) for _, c in items]
        qc = [q_ref[r, lanes_of(h)] for (h, _), r in zip(items, rows)]
        kc = [k_ref[r, lanes_of(h)] for (h, _), r in zip(items, rows)]
        vc = [v_ref[r, lanes_of(h)] for (h, _), r in zip(items, rows)]
        bb = [beta_s[h, r, :] for (h, _), r in zip(items, rows)]
        gg = [g_s[h, r, :] for (h, _), r in zip(items, rows)]
        cum = [jnp.dot(tri3, jnp.concatenate(_split3_bf16(g), axis=0), preferred_element_type=F32)
               for g in gg]
        decay = [jnp.exp(jnp.where(lower_incl, c - c.T, -jnp.inf)) for c in cum]
        kb = [k * b for k, b in zip(kc, bb)]
        scores = [_dot_nt(jnp.concatenate([x, q], axis=0), k) for x, q, k in zip(kb, qc, kc)]
        m = [jnp.where(strict, s[0:C, :] * d, 0.0) for s, d in zip(scores, decay)]
        qk = [jnp.where(lower_incl, s[C:, :] * d, 0.0) for s, d in zip(scores, decay)]
        nmat = [-x for x in m]
        p = [_dot_split(x, x) for x in m]
        span = 2
        while 2 * span < C:
            dot = _dot_split if span < GDN_SPLIT_SPAN else _dot
            both = [dot(jnp.concatenate([n, x], axis=0), x) for n, x in zip(nmat, p)]
            nmat = [n + x + b[0:C, :] for n, x, b in zip(nmat, p, both)]
            p = [b[C:, :] for b in both]
            span *= 2
        nmat = [n + x + _dot(n, x) for n, x in zip(nmat, p)]
        ecum = [jnp.exp(c) for c in cum]
        vb = [v * b for v, b in zip(vc, bb)]
        kbe = [x * e for x, e in zip(kb, ecum)]
        wu = [jnp.concatenate([x, y], axis=1) for x, y in zip(kbe, vb)]
        wu = [x + _dot(n, x) for n, x in zip(nmat, wu)]
        qd = [q * e for q, e in zip(qc, ecum)]
        cum_last = [c[C - 1:C, :] for c in cum]
        ktt = [(k * jnp.exp(cl - c)).T for k, cl, c in zip(kc, cum_last, cum)]
        prod = [_dot(jnp.concatenate([kt, a], axis=0), x) for kt, a, x in zip(ktt, qk, wu)]
        for j, ((h, c), r) in enumerate(zip(items, rows)):
            sa_s[h, r, :] = -prod[j][0:C, 0:C]
            sb_s[h, r, :] = prod[j][0:C, C:]
            oq_s[h, r, :] = qd[j] - prod[j][C:, 0:C]
            oc_s[h, r, :] = prod[j][C:, C:]
            gl_s[h, pl.ds(pl.multiple_of(c * SUBLANES, SUBLANES), SUBLANES), :] = (
                jnp.broadcast_to(jnp.exp(cum_last[j]), (SUBLANES, LANES)))
        return carry

    lax.fori_loop(0, n_chunks // GDN_GROUP, chunk_group, 0)

    nw = nw_ref[...]

    def scan_step(c, states):
        rows = pl.ds(pl.multiple_of(c * C, C), C)
        new = []
        for h, s in enumerate(states):
            gl = gl_s[h, pl.ds(pl.multiple_of(c * SUBLANES, SUBLANES), 1), :]
            new.append(s * gl + (_dot(sa_s[h, rows, :], s) + sb_s[h, rows, :]))
        for h, s in enumerate(states):
            o = _dot(oq_s[h, rows, :], s) + oc_s[h, rows, :]
            z = z_ref[rows, lanes_of(h)]
            o = o * lax.rsqrt(jnp.mean(o * o, axis=-1, keepdims=True) + EPS) * nw
            y_ref[rows, lanes_of(h)] = (o * (z * _sigmoid(z))).astype(y_ref.dtype)
        return tuple(new)

    lax.fori_loop(0, n_chunks, scan_step,
                  tuple(jnp.zeros((GDN_HEAD_DIM, GDN_HEAD_DIM), F32) for _ in range(n_heads)))


def _gdn(a, bd, alog_row, dtb_row, norm_row, layer):
    b, t, _ = a.shape
    width = GDN_STEP_HEADS * GDN_HEAD_DIM
    steps = GDN_HEADS // GDN_STEP_HEADS
    tok = lambda j: pl.BlockSpec((None, t, width), lambda i, p: (i, 0, j * steps + p))
    row = pl.BlockSpec((None, 1, LANES), lambda i, p: (layer, 0, 0))
    head_buf = pltpu.VMEM((GDN_STEP_HEADS, t, LANES), F32)
    return pl.pallas_call(
        functools.partial(_gdn_kernel, seq=t),
        grid=(b, steps),
        in_specs=[tok(0), tok(1), tok(2), tok(3),
                  pl.BlockSpec((None, t, LANES), lambda i, p: (i, 0, 0)), row, row, row],
        out_specs=pl.BlockSpec((None, t, width), lambda i, p: (i, 0, p)),
        out_shape=jax.ShapeDtypeStruct((b, t, GDN_W), BF16),
        scratch_shapes=[head_buf] * 6
        + [pltpu.VMEM((GDN_STEP_HEADS, t // GDN_CHUNK * SUBLANES, LANES), F32)],
        compiler_params=_params("arbitrary", "arbitrary"),
        name="gdn",
    )(a, a, a, a, bd, alog_row, dtb_row, norm_row)


def _stage_values(v_ref, vt_s, seq, dv):
    aug = dv + ATT_AUG
    ones_row = jnp.where(lax.broadcasted_iota(jnp.int32, (ATT_AUG, ATT_K_TILE), 0) == 0, 1.0, 0.0).astype(BF16)
    for g in range(v_ref.shape[1] // LANES):
        for t in range(seq // ATT_K_TILE):
            for h in range(LANES // dv):
                vt_s[g, t, h * aug + dv:(h + 1) * aug, :] = ones_row
        for c in range(seq // LANES):
            t, off = divmod(c * LANES, ATT_K_TILE)
            blk = v_ref[c * LANES:(c + 1) * LANES, g * LANES:(g + 1) * LANES].astype(F32).T.astype(BF16)
            for h in range(LANES // dv):
                vt_s[g, t, h * aug:h * aug + dv, off:off + LANES] = blk[h * dv:(h + 1) * dv, :]


def _stack_masked(q, width):
    lane = lax.broadcasted_iota(jnp.int32, q.shape, 1)
    return jnp.concatenate(
        [jnp.where((lane >= j * width) & (lane < (j + 1) * width), q, 0.0).astype(BF16)
         for j in range(LANES // width)], axis=0)


def _softmax_init(rows, width):
    return (jnp.full((1, width), -jnp.inf, F32), jnp.zeros((rows, width), F32))


def _softmax_tile(state, scores, cnt, vt):
    ps, alphas, maxes = [], [], []
    for (m_old, _), s in zip(state, scores):
        m_new = jnp.maximum(m_old, jnp.max(s, axis=0, keepdims=True))
        p = jnp.exp2(s - m_new)
        if cnt is not None:
            p = p * cnt
        alphas.append(jnp.exp2(m_old - m_new))
        maxes.append(m_new)
        ps.append(p.astype(BF16))
    pv = jnp.dot(vt, jnp.concatenate(ps, axis=1), preferred_element_type=F32)
    width = pv.shape[1] // len(state)
    return [(m, a * acc + pv[:, n * width:(n + 1) * width])
            for n, (m, a, (_, acc)) in enumerate(zip(maxes, alphas, state))]


def _dsw_kernel(q_ref, k_ref, v_ref, o_ref, vt_s, cnt_s, bias_s, *, seq):
    tq, tk = ATT_Q_TILE, ATT_K_TILE
    blocks = q_ref.shape[1] // LANES
    dv = DSW_HEAD_DIM
    heads = LANES // dv
    aug = dv + ATT_AUG
    _stage_values(v_ref, vt_s, seq, dv)

    @pl.when(pl.program_id(0) == 0)
    def _():
        kr = lax.broadcasted_iota(jnp.int32, (tk, tq), 0)
        qc = lax.broadcasted_iota(jnp.int32, (tk, tq), 1)
        for d in range(DSW_FAR_TILE + 1):
            delta = d * tq + qc - kr
            causal = delta >= 0
            cnt = (jnp.where(causal & (delta <= 128), 1.0, 0.0)
                   + jnp.where(causal & (delta <= DSW_MAX_BAND) & ((delta & 3) == 0), 1.0, 0.0)
                   + jnp.where(causal & ((delta & 15) == 0), 1.0, 0.0))
            cnt_s[d] = cnt.astype(F32)
            bias_s[d] = jnp.where(cnt > 0.0, 0.0, -jnp.inf).astype(F32)

    for qi in range(seq // tq):
        qrows = slice(qi * tq, (qi + 1) * tq)
        qstack = [_stack_masked(q_ref[qrows, g * LANES:(g + 1) * LANES].astype(F32), DSW_HEAD_DIM)
                  for g in range(blocks)]
        state = [_softmax_init(aug, tq) for _ in range(blocks * heads)]
        for kj in range((qi * tq) // tk + 1):
            krows = slice(kj * tk, (kj + 1) * tk)
            far = min(qi - kj * (tk // tq), DSW_FAR_TILE)
            cnt = cnt_s[far] if far < DSW_FAR_TILE else None
            bias = bias_s[far]
            s = [_dot_nt(k_ref[krows, g * LANES:(g + 1) * LANES], qstack[g]) for g in range(blocks)]
            for g in range(blocks):
                for h in range(heads):
                    i = g * heads + h
                    state[i:i + 1] = _softmax_tile(state[i:i + 1], [s[g][:, h * tq:(h + 1) * tq] + bias],
                                                   cnt, vt_s[g, kj, h * aug:(h + 1) * aug, :])
        o = jnp.concatenate([acc[0:dv, :] / acc[dv:dv + 1, :] for _, acc in state], axis=0)
        o_ref[qrows, :] = o.T.astype(o_ref.dtype)


def _dsw(bc):
    b, t, _ = bc.shape
    spec = lambda j: pl.BlockSpec((None, t, DSW_W), lambda i: (i, 0, j))
    return pl.pallas_call(
        functools.partial(_dsw_kernel, seq=t),
        grid=(b,),
        in_specs=[spec(0), spec(1), spec(2)],
        out_specs=pl.BlockSpec((None, t, DSW_W), lambda i: (i, 0, 0)),
        out_shape=jax.ShapeDtypeStruct((b, t, DSW_W), BF16),
        scratch_shapes=[pltpu.VMEM((DSW_W // LANES, t // ATT_K_TILE,
                                    LANES // DSW_HEAD_DIM * (DSW_HEAD_DIM + ATT_AUG), ATT_K_TILE), BF16)]
        + [pltpu.VMEM((DSW_FAR_TILE + 1, ATT_K_TILE, ATT_Q_TILE), F32)] * 2,
        compiler_params=_params("arbitrary"),
        name="dsw",
    )(bc, bc, bc)


def _diff_kernel(q_ref, k_ref, v_ref, lam_ref, laminit_ref, nw_ref, o_ref, vt_s, *, seq):
    tq, tk = ATT_Q_TILE, ATT_K_TILE
    blocks = q_ref.shape[1] // LANES
    dv = DIFF_V_DIM
    heads = LANES // dv
    aug = dv + ATT_AUG
    _stage_values(v_ref, vt_s, seq, dv)
    kr = lax.broadcasted_iota(jnp.int32, (tk, tq), 0)
    qc = lax.broadcasted_iota(jnp.int32, (tk, tq), 1)
    lv = lam_ref[...]
    lam_init = laminit_ref[...]
    lam = (jnp.exp(jnp.sum(lv[0:1, :] * lv[1:2, :], keepdims=True))
           - jnp.exp(jnp.sum(lv[2:3, :] * lv[3:4, :], keepdims=True)) + lam_init)
    nw = nw_ref[...]

    for qi in range(seq // tq):
        qrows = slice(qi * tq, (qi + 1) * tq)
        qstack = [_stack_masked(q_ref[qrows, g * LANES:(g + 1) * LANES].astype(F32), DIFF_QK_DIM)
                  for g in range(blocks)]
        state = [_softmax_init(dv + ATT_AUG, tq) for _ in range(2 * heads * blocks)]
        n_full = (qi * tq) // tk
        for kj in range(n_full + 1):
            krows = slice(kj * tk, (kj + 1) * tk)
            valid = None if kj < n_full else kr + (kj * tk - qi * tq) <= qc
            s = [_dot_nt(k_ref[krows, g * LANES:(g + 1) * LANES], qstack[g])
                 for g in range(blocks)]
            for g in range(blocks):
                for h in range(heads):
                    scores = [s[g][:, (2 * h + mp) * tq:(2 * h + mp + 1) * tq] for mp in range(2)]
                    if valid is not None:
                        scores = [jnp.where(valid, x, -jnp.inf) for x in scores]
                    first = 2 * (heads * g + h)
                    state[first:first + 2] = _softmax_tile(state[first:first + 2], scores, None,
                                                           vt_s[g, kj, h * aug:(h + 1) * aug, :])
        att = [acc[0:dv, :] / acc[dv:dv + 1, :] for _, acc in state]
        halves = []
        for h in range(heads * blocks):
            o = att[2 * h] - lam * att[2 * h + 1]
            halves.append(o * lax.rsqrt(jnp.mean(o * o, axis=0, keepdims=True) + EPS))
        o = jnp.concatenate(halves, axis=0).T
        o_ref[qrows, :] = (o * nw * (1.0 - lam_init)).astype(o_ref.dtype)


def _diff(bc, lam_vecs, lam_init, norm_row, layer):
    b, t, _ = bc.shape
    base = 3 * DSW_W // DIFF_W
    spec = lambda j: pl.BlockSpec((None, t, DIFF_W), lambda i: (i, 0, base + j))
    return pl.pallas_call(
        functools.partial(_diff_kernel, seq=t),
        grid=(b,),
        in_specs=[spec(0), spec(1), spec(2),
                  pl.BlockSpec((None, 4, DIFF_QK_DIM), lambda i: (layer, 0, 0)),
                  pl.BlockSpec((None, 1, 1), lambda i: (layer, 0, 0)),
                  pl.BlockSpec((None, 1, DIFF_W), lambda i: (layer, 0, 0))],
        out_specs=pl.BlockSpec((None, t, DIFF_W), lambda i: (i, 0, 0)),
        out_shape=jax.ShapeDtypeStruct((b, t, DIFF_W), BF16),
        scratch_shapes=[pltpu.VMEM((DIFF_W // LANES, t // ATT_K_TILE,
                                    LANES // DIFF_V_DIM * (DIFF_V_DIM + ATT_AUG), ATT_K_TILE), BF16)],
        compiler_params=_params("arbitrary"),
        name="diff",
    )(bc, bc, bc, lam_vecs, lam_init, norm_row)


def _layer_norm(h, g, b):
    mu = jnp.mean(h, axis=-1, keepdims=True)
    hc = h - mu
    var = jnp.mean(hc * hc, axis=-1, keepdims=True)
    return hc * lax.rsqrt(var + EPS) * g + b


def _out_ln_kernel(x_ref, ya_ref, yb_ref, yc_ref, w_ref, g_ref, b_ref, o_ref, *, alpha):
    def project(r):
        rows = slice(r, r + LN_ROWS)
        mixed = jnp.concatenate([ya_ref[rows, :], yb_ref[rows, :], yc_ref[rows, :]], axis=1)
        return jnp.dot(mixed, w_ref[...], preferred_element_type=F32)

    tm = x_ref.shape[0]
    ahead = project(0)
    for r in range(0, tm, LN_ROWS):
        y = ahead
        if r + LN_ROWS < tm:
            ahead = project(r + LN_ROWS)
        o_ref[r:r + LN_ROWS, :] = _layer_norm(alpha * x_ref[r:r + LN_ROWS, :] + y, g_ref[...], b_ref[...])


def _out_ln(x, ya, yb, yc, w, g, b, layer, alpha):
    n, d = x.shape
    tm = min(ROW_TILE, n)
    rows = lambda width: pl.BlockSpec((tm, width), lambda i: (i, 0))
    vec = pl.BlockSpec((None, 1, d), lambda i: (layer, 0, 0))
    return pl.pallas_call(
        functools.partial(_out_ln_kernel, alpha=alpha),
        grid=(n // tm,),
        in_specs=[rows(d), rows(GDN_W), rows(DSW_W), rows(DIFF_W),
                  pl.BlockSpec((None, d, d), lambda i: (layer, 0, 0)), vec, vec],
        out_specs=rows(d),
        out_shape=jax.ShapeDtypeStruct((n, d), F32),
        compiler_params=_params("arbitrary"),
        name="out_ln",
    )(x, ya, yb, yc, w, g, b)


def _ffn_up_kernel(x_ref, wg_ref, wv_ref, cg_ref, cv_ref, h_ref, work_g, work_v, carry_g, carry_v,
                   *, tiles_per_seq, col_chunk):
    tm = x_ref.shape[0]
    width = h_ref.shape[1]

    @pl.when(pl.program_id(0) % tiles_per_seq == 0)
    def _():
        carry_g[...] = jnp.zeros(carry_g.shape, F32)
        carry_v[...] = jnp.zeros(carry_v.shape, F32)

    x = x_ref[...].astype(BF16)
    first = SUBLANES - (FFN_CONV - 1)

    def conv(u, cw_ref, work, carry, cols):
        work[0:SUBLANES, :] = carry[:, cols]
        work[SUBLANES:SUBLANES + tm, :] = u
        carry[:, cols] = work[tm:tm + SUBLANES, :]
        cw = cw_ref[:, cols]
        acc = u * cw[FFN_CONV - 1:FFN_CONV, :]
        for j in range(FFN_CONV - 1):
            acc = acc + work[first + j:first + j + tm, :] * cw[j:j + 1, :]
        return acc

    def project(c):
        cols = slice(c, c + col_chunk)
        return (jnp.dot(x, wg_ref[:, cols], preferred_element_type=F32),
                jnp.dot(x, wv_ref[:, cols], preferred_element_type=F32))

    ahead = project(0)
    for c in range(0, width, col_chunk):
        ug, uv = ahead
        if c + col_chunk < width:
            ahead = project(c + col_chunk)
        cols = slice(c, c + col_chunk)
        gate = conv(ug, cg_ref, work_g, carry_g, cols)
        val = conv(uv, cv_ref, work_v, carry_v, cols)
        h_ref[:, cols] = (gate * _sigmoid(gate) * val).astype(h_ref.dtype)


def _ffn_up(x, wg, wv, cg, cv, layer, seq):
    n, d = x.shape
    width = wg.shape[-1]
    tm = min(ROW_TILE, seq)
    col_chunk = 256
    wspec = pl.BlockSpec((None, d, width), lambda i: (layer, 0, 0))
    cspec = pl.BlockSpec((None, FFN_CONV, width), lambda i: (layer, 0, 0))
    return pl.pallas_call(
        functools.partial(_ffn_up_kernel, tiles_per_seq=seq // tm, col_chunk=col_chunk),
        grid=(n // tm,),
        in_specs=[pl.BlockSpec((tm, d), lambda i: (i, 0)), wspec, wspec, cspec, cspec],
        out_specs=pl.BlockSpec((tm, width), lambda i: (i, 0)),
        out_shape=jax.ShapeDtypeStruct((n, width), BF16),
        scratch_shapes=[pltpu.VMEM((tm + SUBLANES, col_chunk), F32)] * 2
        + [pltpu.VMEM((SUBLANES, width), F32)] * 2,
        compiler_params=_params("arbitrary"),
        name="ffn_up",
    )(x, wg, wv, cg, cv)


def _ffn_down_kernel(x_ref, h_ref, w_ref, g_ref, b_ref, o_ref, *, alpha):
    def project(r):
        return jnp.dot(h_ref[r:r + LN_ROWS, :], w_ref[...], preferred_element_type=F32)

    tm = x_ref.shape[0]
    ahead = project(0)
    for r in range(0, tm, LN_ROWS):
        f = ahead
        if r + LN_ROWS < tm:
            ahead = project(r + LN_ROWS)
        o_ref[r:r + LN_ROWS, :] = _layer_norm(alpha * x_ref[r:r + LN_ROWS, :] + f, g_ref[...], b_ref[...])


def _ffn_down(x, h, w, g, b, layer, alpha):
    n, d = x.shape
    width = h.shape[1]
    tm = min(ROW_TILE, n)
    vec = pl.BlockSpec((None, 1, d), lambda i: (layer, 0, 0))
    return pl.pallas_call(
        functools.partial(_ffn_down_kernel, alpha=alpha),
        grid=(n // tm,),
        in_specs=[pl.BlockSpec((tm, d), lambda i: (i, 0)), pl.BlockSpec((tm, width), lambda i: (i, 0)),
                  pl.BlockSpec((None, width, d), lambda i: (layer, 0, 0)), vec, vec],
        out_specs=pl.BlockSpec((tm, d), lambda i: (i, 0)),
        out_shape=jax.ShapeDtypeStruct((n, d), F32),
        compiler_params=_params("arbitrary"),
        name="ffn_down",
    )(x, h, w, g, b)


def _pad_last(a, width):
    return jnp.pad(a, [(0, 0)] * (a.ndim - 1) + [(0, width - a.shape[-1])])


def kernel(x, w_in, gdn_conv, gdn_a_log, gdn_dt_bias, gdn_norm, diff_lambda, diff_norm, w_out,
           ln1_g, ln1_b, w_up, ffn_conv, w_down, ln2_g, ln2_b):
    batch, seq, d = x.shape
    depth = w_in.shape[0]
    d_ff = w_down.shape[1]
    ff_pad = -(-d_ff // LANES) * LANES
    alpha = (2 * depth) ** 0.25

    bd0 = A_W
    bc0 = A_W + 2 * GDN_HEADS
    wa = w_in[:, :, :A_W].astype(BF16)
    wbd = _pad_last(w_in[:, :, bd0:bc0], LANES).astype(BF16)
    wbc = w_in[:, :, bc0:].astype(BF16)
    scale = jnp.concatenate([
        jnp.full((DSW_W,), DSW_HEAD_DIM ** -0.5 * LOG2E, F32), jnp.ones((2 * DSW_W,), F32),
        jnp.full((DIFF_QK_W,), DIFF_QK_DIM ** -0.5 * LOG2E, F32), jnp.ones((DIFF_QK_W + DIFF_W,), F32)])[None, :]
    lane_row = lambda v, off: jnp.pad(v, ((0, 0), (off, LANES - off - v.shape[1])))[:, None, :]
    alog_row = lane_row(gdn_a_log, GDN_HEADS)
    dtb_row = lane_row(gdn_dt_bias, GDN_HEADS)
    gdn_norm_row = gdn_norm[:, None, :]
    diff_norm_row = jnp.tile(diff_norm, (1, DIFF_HEADS))[:, None, :]
    lam_init = jnp.asarray([0.8 - 0.6 * math.exp(-0.3 * l) for l in range(depth)], F32)[:, None, None]
    w_out_b = w_out.astype(BF16)
    wg = _pad_last(w_up[:, :, :d_ff], ff_pad).astype(BF16)
    wv = _pad_last(w_up[:, :, d_ff:], ff_pad).astype(BF16)
    cg = _pad_last(ffn_conv[:, :, :d_ff], ff_pad)
    cv = _pad_last(ffn_conv[:, :, d_ff:], ff_pad)
    w_down_b = jnp.pad(w_down, ((0, 0), (0, ff_pad - d_ff), (0, 0))).astype(BF16)
    vec3 = lambda v: v[:, None, :]

    xf = x.reshape(batch * seq, d)
    for l in range(depth):
        a, bc, bd = _in_proj(xf, wa, wbc, wbd, scale, gdn_conv, l, seq)
        bc3 = bc.reshape(batch, seq, BC_W)
        ya = _gdn(a.reshape(batch, seq, A_W), bd.reshape(batch, seq, LANES),
                  alog_row, dtb_row, gdn_norm_row, l)
        yb = _dsw(bc3)
        yc = _diff(bc3, diff_lambda, lam_init, diff_norm_row, l)
        n = batch * seq
        x1 = _out_ln(xf, ya.reshape(n, GDN_W), yb.reshape(n, DSW_W), yc.reshape(n, DIFF_W),
                     w_out_b, vec3(ln1_g), vec3(ln1_b), l, alpha)
        h = _ffn_up(x1, wg, wv, cg, cv, l, seq)
        xf = _ffn_down(x1, h, w_down_b, vec3(ln2_g), vec3(ln2_b), l, alpha)
    return xf.reshape(batch, seq, d)
```

```python
import functools
import math

import jax
import jax.numpy as jnp
from jax import lax
from jax.experimental import pallas as pl
from jax.experimental.pallas import tpu as pltpu

F32 = jnp.float32
BF16 = jnp.bfloat16

LANES = 128
SUBLANES = 8
VMEM_LIMIT = 48 * 1024 * 1024

GDN_HEADS = 4
GDN_HEAD_DIM = 128
GDN_CONV = 4
DSW_HEADS = 4
DSW_HEAD_DIM = 64
DIFF_HEADS = 4
DIFF_QK_DIM = 32
DIFF_V_DIM = 64
FFN_CONV = 3
EPS = 1e-5

GDN_W = GDN_HEADS * GDN_HEAD_DIM
DSW_W = DSW_HEADS * DSW_HEAD_DIM
DIFF_W = DIFF_HEADS * DIFF_V_DIM
DIFF_QK_W = DIFF_HEADS * 2 * DIFF_QK_DIM
A_W = 4 * GDN_W
BC_W = 3 * DSW_W + 2 * DIFF_QK_W + DIFF_W

ROW_TILE = 512
LN_ROWS = 128
PROJ_CHUNK = 256
EPI_ROWS = 64
ATT_Q_TILE = 512
ATT_K_TILE = 512
ATT_AUG = 16
LOG2E = math.log2(math.e)
GDN_CHUNK = 128
GDN_STEP_HEADS = 2
GDN_GROUP = 4
GDN_SPLIT_SPAN = 16
DSW_MAX_BAND = 512
DSW_FAR_TILE = -(-(DSW_MAX_BAND + ATT_K_TILE) // ATT_Q_TILE)


def _dot(a, b):
    return jnp.dot(a.astype(BF16), b.astype(BF16), preferred_element_type=F32)


def _dot_nt(a, b):
    return lax.dot_general(a.astype(BF16), b.astype(BF16), (((1,), (1,)), ((), ())),
                           preferred_element_type=F32)


def _split_bf16(a):
    hi = a.astype(BF16)
    return hi, (a - hi.astype(F32)).astype(BF16)


def _split3_bf16(a):
    hi, mid = _split_bf16(a)
    return hi, mid, (a - hi.astype(F32) - mid.astype(F32)).astype(BF16)


def _dot_split(a, b):
    ah, al = _split_bf16(a)
    bh, bl = _split_bf16(b)
    return jnp.dot(jnp.concatenate([ah, al, ah], axis=1), jnp.concatenate([bh, bh, bl], axis=0),
                   preferred_element_type=F32)


def _sigmoid(x):
    return 1.0 / (1.0 + jnp.exp(-x))


def _params(*sem):
    return pltpu.CompilerParams(dimension_semantics=sem, vmem_limit_bytes=VMEM_LIMIT)


def _in_proj_kernel(x_ref, w_ref, scale_ref, cw_ref, a_ref, bc_ref, bd_ref,
                    work, carry, *, tiles_per_seq):
    tm = x_ref.shape[0]
    x = x_ref[...].astype(BF16)

    @pl.when(pl.program_id(0) % tiles_per_seq == 0)
    def _():
        carry[...] = jnp.zeros(carry.shape, F32)

    first = SUBLANES - (GDN_CONV - 1)

    def project_conv(cols, slot):
        work[slot, 0:SUBLANES, :] = carry[:, cols]
        work[slot, SUBLANES:SUBLANES + tm, :] = jnp.dot(x, w_ref[:, cols], preferred_element_type=F32)
        carry[:, cols] = work[slot, tm:tm + SUBLANES, :]

    def finish_conv(cols, slot, norm_scale):
        taps = [jnp.broadcast_to(0.5 * cw_ref[j:j + 1, cols], (EPI_ROWS, PROJ_CHUNK)) for j in range(GDN_CONV)]
        for r in range(0, tm, EPI_ROWS):
            y = work[slot, first + r:first + r + EPI_ROWS, :] * taps[0]
            for j in range(1, GDN_CONV):
                y = y + work[slot, first + j + r:first + j + r + EPI_ROWS, :] * taps[j]
            y = y + y * jnp.tanh(y)
            if norm_scale is not None:
                heads = [y[:, h * GDN_HEAD_DIM:(h + 1) * GDN_HEAD_DIM] for h in range(PROJ_CHUNK // GDN_HEAD_DIM)]
                y = jnp.concatenate(
                    [v * (lax.rsqrt(jnp.sum(v * v, axis=-1, keepdims=True) + 1e-6) * norm_scale)
                     for v in heads], axis=1)
            a_ref[r:r + EPI_ROWS, cols] = y

    def plain(first_col, cols, store):
        wcols = slice(first_col + cols.start, first_col + cols.stop)
        return lambda: store(jnp.dot(x, w_ref[:, wcols], preferred_element_type=F32))

    def store_z(cols):
        def store(u):
            a_ref[:, cols] = u
        return store

    def store_bc(cols):
        def store(u):
            bc_ref[:, cols] = (u * scale_ref[:, cols]).astype(BF16)
        return store

    def store_bd(u):
        bd_ref[...] = u

    chunks = lambda base, width: [slice(base + c, base + c + PROJ_CHUNK) for c in range(0, width, PROJ_CHUNK)]
    heavy = ([(c, GDN_HEAD_DIM ** -0.5) for c in chunks(0, GDN_W)] + [(c, 1.0) for c in chunks(GDN_W, GDN_W)]
             + [(c, None) for c in chunks(2 * GDN_W, GDN_W)])
    light = ([plain(A_W, c, store_bc(c)) for c in chunks(0, BC_W)]
             + [plain(0, c, store_z(c)) for c in chunks(3 * GDN_W, GDN_W)]
             + [plain(A_W + BC_W, slice(0, LANES), store_bd)])
    project_conv(heavy[0][0], 0)
    for n, (cols, norm_scale) in enumerate(heavy):
        if n + 1 < len(heavy):
            project_conv(heavy[n + 1][0], (n + 1) % 2)
        light[n]()
        finish_conv(cols, n % 2, norm_scale)
    for job in light[len(heavy):]:
        job()


def _in_proj(x, w, scale, conv_w, layer, seq):
    n, d = x.shape
    tm = min(ROW_TILE, seq)
    return pl.pallas_call(
        functools.partial(_in_proj_kernel, tiles_per_seq=seq // tm),
        grid=(n // tm,),
        in_specs=[
            pl.BlockSpec((tm, d), lambda i: (i, 0)),
            pl.BlockSpec((None, d, A_W + BC_W + LANES), lambda i: (layer, 0, 0)),
            pl.BlockSpec((1, BC_W), lambda i: (0, 0)),
            pl.BlockSpec((None, GDN_CONV, 3 * GDN_W), lambda i: (layer, 0, 0)),
        ],
        out_specs=[
            pl.BlockSpec((tm, A_W), lambda i: (i, 0)),
            pl.BlockSpec((tm, BC_W), lambda i: (i, 0)),
            pl.BlockSpec((tm, LANES), lambda i: (i, 0)),
        ],
        out_shape=[
            jax.ShapeDtypeStruct((n, A_W), F32),
            jax.ShapeDtypeStruct((n, BC_W), BF16),
            jax.ShapeDtypeStruct((n, LANES), F32),
        ],
        scratch_shapes=[pltpu.VMEM((2, tm + SUBLANES, PROJ_CHUNK), F32), pltpu.VMEM((SUBLANES, 3 * GDN_W), F32)],
        compiler_params=_params("arbitrary"),
        name="in_proj",
    )(x, w, scale, conv_w)


def _gdn_kernel(q_ref, k_ref, v_ref, z_ref, bd_ref, alog_ref, dtb_ref, nw_ref, y_ref,
                beta_s, g_s, sa_s, sb_s, oq_s, oc_s, gl_s, *, seq):
    C = GDN_CHUNK
    n_chunks = seq // C
    n_heads = q_ref.shape[1] // LANES
    head0 = pl.program_id(1) * n_heads
    lanes_of = lambda h: slice(h * LANES, (h + 1) * LANES)

    bd = bd_ref[...]
    lane = lax.broadcasted_iota(jnp.int32, (seq, LANES), 1)
    beta_all = _sigmoid(bd)
    xg = bd + dtb_ref[...]
    softplus = jnp.maximum(xg, 0.0) + jnp.log(1.0 + jnp.exp(-jnp.abs(xg)))
    g_all = -jnp.exp(alog_ref[...]) * softplus
    for h in range(n_heads):
        beta = jnp.sum(jnp.where(lane == head0 + h, beta_all, 0.0), axis=-1, keepdims=True)
        g = jnp.sum(jnp.where(lane == head0 + h + GDN_HEADS, g_all, 0.0), axis=-1, keepdims=True)
        beta_s[h] = jnp.broadcast_to(beta, (seq, LANES))
        g_s[h] = jnp.broadcast_to(g, (seq, LANES))

    ri = lax.broadcasted_iota(jnp.int32, (C, C), 0)
    ci = lax.broadcasted_iota(jnp.int32, (C, C), 1)
    lower_incl = ri >= ci
    strict = ri > ci
    tri = jnp.where(lower_incl, 1.0, 0.0).astype(BF16)
    tri3 = jnp.concatenate([tri, tri, tri], axis=1)

    def chunk_group(gi, carry):
        items = [(h, gi * GDN_GROUP + j) for j in range(GDN_GROUP) for h in range(n_heads)]
        grp = range(len(items))
        rows = [pl.ds(pl.multiple_of(c * C, C), C) for _, c in items]
        qc = [q_ref[r, lanes_of(h)] for (h, _), r in zip(items, rows)]
        kc = [k_ref[r, lanes_of(h)] for (h, _), r in zip(items, rows)]
        vc = [v_ref[r, lanes_of(h)] for (h, _), r in zip(items, rows)]
        bb = [beta_s[h, r, :] for (h, _), r in zip(items, rows)]
        gg = [g_s[h, r, :] for (h, _), r in zip(items, rows)]
        cum = [jnp.dot(tri3, jnp.concatenate(_split3_bf16(g), axis=0), preferred_element_type=F32)
               for g in gg]
        decay = [jnp.exp(jnp.where(lower_incl, c - c.T, -jnp.inf)) for c in cum]
        kb = [k * b for k, b in zip(kc, bb)]
        scores = [_dot_nt(jnp.concatenate([x, q], axis=0), k) for x, q, k in zip(kb, qc, kc)]
        m = [jnp.where(strict, s[0:C, :] * d, 0.0) for s, d in zip(scores, decay)]
        qk = [jnp.where(lower_incl, s[C:, :] * d, 0.0) for s, d in zip(scores, decay)]
        nmat = [-x for x in m]
        p = [_dot_split(x, x) for x in m]
        span = 2
        while 2 * span < C:
            dot = _dot_split if span < GDN_SPLIT_SPAN else _dot
            both = [dot(jnp.concatenate([n, x], axis=0), x) for n, x in zip(nmat, p)]
            nmat = [n + x + b[0:C, :] for n, x, b in zip(nmat, p, both)]
            p = [b[C:, :] for b in both]
            span *= 2
        nmat = [n + x + _dot(n, x) for n, x in zip(nmat, p)]
        ecum = [jnp.exp(c) for c in cum]
        vb = [v * b for v, b in zip(vc, bb)]
        kbe = [x * e for x, e in zip(kb, ecum)]
        wu = [jnp.concatenate([x, y], axis=1) for x, y in zip(kbe, vb)]
        wu = [x + _dot(n, x) for n, x in zip(nmat, wu)]
        qd = [q * e for q, e in zip(qc, ecum)]
        cum_last = [c[C - 1:C, :] for c in cum]
        ktt = [(k * jnp.exp(cl - c)).T for k, cl, c in zip(kc, cum_last, cum)]
        prod = [_dot(jnp.concatenate([kt, a], axis=0), x) for kt, a, x in zip(ktt, qk, wu)]
        for j, ((h, c), r) in enumerate(zip(items, rows)):
            sa_s[h, r, :] = -prod[j][0:C, 0:C]
            sb_s[h, r, :] = prod[j][0:C, C:]
            oq_s[h, r, :] = qd[j] - prod[j][C:, 0:C]
            oc_s[h, r, :] = prod[j][C:, C:]
            gl_s[h, pl.ds(pl.multiple_of(c * SUBLANES, SUBLANES), SUBLANES), :] = (
                jnp.broadcast_to(jnp.exp(cum_last[j]), (SUBLANES, LANES)))
        return carry

    lax.fori_loop(0, n_chunks // GDN_GROUP, chunk_group, 0)

    nw = nw_ref[...]

    def scan_step(c, states):
        rows = pl.ds(pl.multiple_of(c * C, C), C)
        new = []
        for h, s in enumerate(states):
            gl = gl_s[h, pl.ds(pl.multiple_of(c * SUBLANES, SUBLANES), 1), :]
            new.append(s * gl + (_dot(sa_s[h, rows, :], s) + sb_s[h, rows, :]))
        for h, s in enumerate(states):
            o = _dot(oq_s[h, rows, :], s) + oc_s[h, rows, :]
            z = z_ref[rows, lanes_of(h)]
            o = o * lax.rsqrt(jnp.mean(o * o, axis=-1, keepdims=True) + EPS) * nw
            y_ref[rows, lanes_of(h)] = (o * (z * _sigmoid(z))).astype(y_ref.dtype)
        return tuple(new)

    lax.fori_loop(0, n_chunks, scan_step,
                  tuple(jnp.zeros((GDN_HEAD_DIM, GDN_HEAD_DIM), F32) for _ in range(n_heads)))


def _gdn(a, bd, alog_row, dtb_row, norm_row, layer):
    b, t, _ = a.shape
    width = GDN_STEP_HEADS * GDN_HEAD_DIM
    steps = GDN_HEADS // GDN_STEP_HEADS
    tok = lambda j: pl.BlockSpec((None, t, width), lambda i, p: (i, 0, j * steps + p))
    row = pl.BlockSpec((None, 1, LANES), lambda i, p: (layer, 0, 0))
    head_buf = pltpu.VMEM((GDN_STEP_HEADS, t, LANES), F32)
    return pl.pallas_call(
        functools.partial(_gdn_kernel, seq=t),
        grid=(b, steps),
        in_specs=[tok(0), tok(1), tok(2), tok(3),
                  pl.BlockSpec((None, t, LANES), lambda i, p: (i, 0, 0)), row, row, row],
        out_specs=pl.BlockSpec((None, t, width), lambda i, p: (i, 0, p)),
        out_shape=jax.ShapeDtypeStruct((b, t, GDN_W), BF16),
        scratch_shapes=[head_buf] * 6
        + [pltpu.VMEM((GDN_STEP_HEADS, t // GDN_CHUNK * SUBLANES, LANES), F32)],
        compiler_params=_params("arbitrary", "arbitrary"),
        name="gdn",
    )(a, a, a, a, bd, alog_row, dtb_row, norm_row)


def _stage_values(v_ref, vt_s, seq, dv):
    aug = dv + ATT_AUG
    ones_row = jnp.where(lax.broadcasted_iota(jnp.int32, (ATT_AUG, ATT_K_TILE), 0) == 0, 1.0, 0.0).astype(BF16)
    for g in range(v_ref.shape[1] // LANES):
        for t in range(seq // ATT_K_TILE):
            for h in range(LANES // dv):
                vt_s[g, t, h * aug + dv:(h + 1) * aug, :] = ones_row
        for c in range(seq // LANES):
            t, off = divmod(c * LANES, ATT_K_TILE)
            blk = v_ref[c * LANES:(c + 1) * LANES, g * LANES:(g + 1) * LANES].astype(F32).T.astype(BF16)
            for h in range(LANES // dv):
                vt_s[g, t, h * aug:h * aug + dv, off:off + LANES] = blk[h * dv:(h + 1) * dv, :]


def _stack_masked(q, width):
    lane = lax.broadcasted_iota(jnp.int32, q.shape, 1)
    return jnp.concatenate(
        [jnp.where((lane >= j * width) & (lane < (j + 1) * width), q, 0.0).astype(BF16)
         for j in range(LANES // width)], axis=0)


def _softmax_init(rows, width):
    return (jnp.full((1, width), -jnp.inf, F32), jnp.zeros((rows, width), F32))


def _softmax_tile(state, scores, cnt, vt):
    ps, alphas, maxes = [], [], []
    for (m_old, _), s in zip(state, scores):
        m_new = jnp.maximum(m_old, jnp.max(s, axis=0, keepdims=True))
        p = jnp.exp2(s - m_new)
        if cnt is not None:
            p = p * cnt
        alphas.append(jnp.exp2(m_old - m_new))
        maxes.append(m_new)
        ps.append(p.astype(BF16))
    pv = jnp.dot(vt, jnp.concatenate(ps, axis=1), preferred_element_type=F32)
    width = pv.shape[1] // len(state)
    return [(m, a * acc + pv[:, n * width:(n + 1) * width])
            for n, (m, a, (_, acc)) in enumerate(zip(maxes, alphas, state))]


def _dsw_kernel(q_ref, k_ref, v_ref, o_ref, vt_s, cnt_s, bias_s, *, seq):
    tq, tk = ATT_Q_TILE, ATT_K_TILE
    blocks = q_ref.shape[1] // LANES
    dv = DSW_HEAD_DIM
    heads = LANES // dv
    aug = dv + ATT_AUG
    _stage_values(v_ref, vt_s, seq, dv)

    @pl.when(pl.program_id(0) == 0)
    def _():
        kr = lax.broadcasted_iota(jnp.int32, (tk, tq), 0)
        qc = lax.broadcasted_iota(jnp.int32, (tk, tq), 1)
        for d in range(DSW_FAR_TILE + 1):
            delta = d * tq + qc - kr
            causal = delta >= 0
            cnt = (jnp.where(causal & (delta <= 128), 1.0, 0.0)
                   + jnp.where(causal & (delta <= DSW_MAX_BAND) & ((delta & 3) == 0), 1.0, 0.0)
                   + jnp.where(causal & ((delta & 15) == 0), 1.0, 0.0))
            cnt_s[d] = cnt.astype(F32)
            bias_s[d] = jnp.where(cnt > 0.0, 0.0, -jnp.inf).astype(F32)

    for qi in range(seq // tq):
        qrows = slice(qi * tq, (qi + 1) * tq)
        qstack = [_stack_masked(q_ref[qrows, g * LANES:(g + 1) * LANES].astype(F32), DSW_HEAD_DIM)
                  for g in range(blocks)]
        state = [_softmax_init(aug, tq) for _ in range(blocks * heads)]
        for kj in range((qi * tq) // tk + 1):
            krows = slice(kj * tk, (kj + 1) * tk)
            far = min(qi - kj * (tk // tq), DSW_FAR_TILE)
            cnt = cnt_s[far] if far < DSW_FAR_TILE else None
            bias = bias_s[far]
            s = [_dot_nt(k_ref[krows, g * LANES:(g + 1) * LANES], qstack[g]) for g in range(blocks)]
            for g in range(blocks):
                for h in range(heads):
                    i = g * heads + h
                    state[i:i + 1] = _softmax_tile(state[i:i + 1], [s[g][:, h * tq:(h + 1) * tq] + bias],
                                                   cnt, vt_s[g, kj, h * aug:(h + 1) * aug, :])
        o = jnp.concatenate([acc[0:dv, :] / acc[dv:dv + 1, :] for _, acc in state], axis=0)
        o_ref[qrows, :] = o.T.astype(o_ref.dtype)


def _dsw(bc):
    b, t, _ = bc.shape
    spec = lambda j: pl.BlockSpec((None, t, DSW_W), lambda i: (i, 0, j))
    return pl.pallas_call(
        functools.partial(_dsw_kernel, seq=t),
        grid=(b,),
        in_specs=[spec(0), spec(1), spec(2)],
        out_specs=pl.BlockSpec((None, t, DSW_W), lambda i: (i, 0, 0)),
        out_shape=jax.ShapeDtypeStruct((b, t, DSW_W), BF16),
        scratch_shapes=[pltpu.VMEM((DSW_W // LANES, t // ATT_K_TILE,
                                    LANES // DSW_HEAD_DIM * (DSW_HEAD_DIM + ATT_AUG), ATT_K_TILE), BF16)]
        + [pltpu.VMEM((DSW_FAR_TILE + 1, ATT_K_TILE, ATT_Q_TILE), F32)] * 2,
        compiler_params=_params("arbitrary"),
        name="dsw",
    )(bc, bc, bc)


def _diff_kernel(q_ref, k_ref, v_ref, lam_ref, laminit_ref, nw_ref, o_ref, vt_s, *, seq):
    tq, tk = ATT_Q_TILE, ATT_K_TILE
    blocks = q_ref.shape[1] // LANES
    dv = DIFF_V_DIM
    heads = LANES // dv
    aug = dv + ATT_AUG
    _stage_values(v_ref, vt_s, seq, dv)
    kr = lax.broadcasted_iota(jnp.int32, (tk, tq), 0)
    qc = lax.broadcasted_iota(jnp.int32, (tk, tq), 1)
    lv = lam_ref[...]
    lam_init = laminit_ref[...]
    lam = (jnp.exp(jnp.sum(lv[0:1, :] * lv[1:2, :], keepdims=True))
           - jnp.exp(jnp.sum(lv[2:3, :] * lv[3:4, :], keepdims=True)) + lam_init)
    nw = nw_ref[...]

    for qi in range(seq // tq):
        qrows = slice(qi * tq, (qi + 1) * tq)
        qstack = [_stack_masked(q_ref[qrows, g * LANES:(g + 1) * LANES].astype(F32), DIFF_QK_DIM)
                  for g in range(blocks)]
        state = [_softmax_init(dv + ATT_AUG, tq) for _ in range(2 * heads * blocks)]
        n_full = (qi * tq) // tk
        for kj in range(n_full + 1):
            krows = slice(kj * tk, (kj + 1) * tk)
            valid = None if kj < n_full else kr + (kj * tk - qi * tq) <= qc
            s = [_dot_nt(k_ref[krows, g * LANES:(g + 1) * LANES], qstack[g])
                 for g in range(blocks)]
            for g in range(blocks):
                for h in range(heads):
                    scores = [s[g][:, (2 * h + mp) * tq:(2 * h + mp + 1) * tq] for mp in range(2)]
                    if valid is not None:
                        scores = [jnp.where(valid, x, -jnp.inf) for x in scores]
                    first = 2 * (heads * g + h)
                    state[first:first + 2] = _softmax_tile(state[first:first + 2], scores, None,
                                                           vt_s[g, kj, h * aug:(h + 1) * aug, :])
        att = [acc[0:dv, :] / acc[dv:dv + 1, :] for _, acc in state]
        halves = []
        for h in range(heads * blocks):
            o = att[2 * h] - lam * att[2 * h + 1]
            halves.append(o * lax.rsqrt(jnp.mean(o * o, axis=0, keepdims=True) + EPS))
        o = jnp.concatenate(halves, axis=0).T
        o_ref[qrows, :] = (o * nw * (1.0 - lam_init)).astype(o_ref.dtype)


def _diff(bc, lam_vecs, lam_init, norm_row, layer):
    b, t, _ = bc.shape
    base = 3 * DSW_W // DIFF_W
    spec = lambda j: pl.BlockSpec((None, t, DIFF_W), lambda i: (i, 0, base + j))
    return pl.pallas_call(
        functools.partial(_diff_kernel, seq=t),
        grid=(b,),
        in_specs=[spec(0), spec(1), spec(2),
                  pl.BlockSpec((None, 4, DIFF_QK_DIM), lambda i: (layer, 0, 0)),
                  pl.BlockSpec((None, 1, 1), lambda i: (layer, 0, 0)),
                  pl.BlockSpec((None, 1, DIFF_W), lambda i: (layer, 0, 0))],
        out_specs=pl.BlockSpec((None, t, DIFF_W), lambda i: (i, 0, 0)),
        out_shape=jax.ShapeDtypeStruct((b, t, DIFF_W), BF16),
        scratch_shapes=[pltpu.VMEM((DIFF_W // LANES, t // ATT_K_TILE,
                                    LANES // DIFF_V_DIM * (DIFF_V_DIM + ATT_AUG), ATT_K_TILE), BF16)],
        compiler_params=_params("arbitrary"),
        name="diff",
    )(bc, bc, bc, lam_vecs, lam_init, norm_row)


def _layer_norm(h, g, b):
    mu = jnp.mean(h, axis=-1, keepdims=True)
    hc = h - mu
    var = jnp.mean(hc * hc, axis=-1, keepdims=True)
    return hc * lax.rsqrt(var + EPS) * g + b


def _out_ln_kernel(x_ref, ya_ref, yb_ref, yc_ref, w_ref, g_ref, b_ref, o_ref, *, alpha):
    def project(r):
        rows = slice(r, r + LN_ROWS)
        mixed = jnp.concatenate([ya_ref[rows, :], yb_ref[rows, :], yc_ref[rows, :]], axis=1)
        return jnp.dot(mixed, w_ref[...], preferred_element_type=F32)

    tm = x_ref.shape[0]
    ahead = project(0)
    for r in range(0, tm, LN_ROWS):
        y = ahead
        if r + LN_ROWS < tm:
            ahead = project(r + LN_ROWS)
        o_ref[r:r + LN_ROWS, :] = _layer_norm(alpha * x_ref[r:r + LN_ROWS, :] + y, g_ref[...], b_ref[...])


def _out_ln(x, ya, yb, yc, w, g, b, layer, alpha):
    n, d = x.shape
    tm = min(ROW_TILE, n)
    rows = lambda width: pl.BlockSpec((tm, width), lambda i: (i, 0))
    vec = pl.BlockSpec((None, 1, d), lambda i: (layer, 0, 0))
    return pl.pallas_call(
        functools.partial(_out_ln_kernel, alpha=alpha),
        grid=(n // tm,),
        in_specs=[rows(d), rows(GDN_W), rows(DSW_W), rows(DIFF_W),
                  pl.BlockSpec((None, d, d), lambda i: (layer, 0, 0)), vec, vec],
        out_specs=rows(d),
        out_shape=jax.ShapeDtypeStruct((n, d), F32),
        compiler_params=_params("arbitrary"),
        name="out_ln",
    )(x, ya, yb, yc, w, g, b)


def _ffn_up_kernel(x_ref, w_ref, cg_ref, cv_ref, h_ref, work_g, work_v, carry_g, carry_v,
                   *, tiles_per_seq, col_chunk):
    tm = x_ref.shape[0]
    width = h_ref.shape[1]

    @pl.when(pl.program_id(0) % tiles_per_seq == 0)
    def _():
        carry_g[...] = jnp.zeros(carry_g.shape, F32)
        carry_v[...] = jnp.zeros(carry_v.shape, F32)

    x = x_ref[...].astype(BF16)
    first = SUBLANES - (FFN_CONV - 1)

    def project(c, slot):
        cols = slice(c, c + col_chunk)
        for first_col, work, carry in ((0, work_g, carry_g), (width, work_v, carry_v)):
            work[slot, 0:SUBLANES, :] = carry[:, cols]
            work[slot, SUBLANES:SUBLANES + tm, :] = jnp.dot(
                x, w_ref[:, first_col + c:first_col + c + col_chunk], preferred_element_type=F32)
            carry[:, cols] = work[slot, tm:tm + SUBLANES, :]

    def conv(work, slot, taps, r):
        acc = work[slot, first + r:first + r + EPI_ROWS, :] * taps[0]
        for j in range(1, FFN_CONV):
            acc = acc + work[slot, first + j + r:first + j + r + EPI_ROWS, :] * taps[j]
        return acc

    project(0, 0)
    for n, c in enumerate(range(0, width, col_chunk)):
        slot = n % 2
        if c + col_chunk < width:
            project(c + col_chunk, 1 - slot)
        cols = slice(c, c + col_chunk)
        taps_g = [jnp.broadcast_to(0.5 * cg_ref[j:j + 1, cols], (EPI_ROWS, col_chunk)) for j in range(FFN_CONV)]
        taps_v = [jnp.broadcast_to(cv_ref[j:j + 1, cols], (EPI_ROWS, col_chunk)) for j in range(FFN_CONV)]
        for r in range(0, tm, EPI_ROWS):
            half = conv(work_g, slot, taps_g, r)
            val = conv(work_v, slot, taps_v, r)
            h_ref[r:r + EPI_ROWS, cols] = ((half + half * jnp.tanh(half)) * val).astype(h_ref.dtype)


def _ffn_up(x, w, cg, cv, layer, seq):
    n, d = x.shape
    width = w.shape[-1] // 2
    tm = min(ROW_TILE, seq)
    col_chunk = 256
    wspec = pl.BlockSpec((None, d, 2 * width), lambda i: (layer, 0, 0))
    cspec = pl.BlockSpec((None, FFN_CONV, width), lambda i: (layer, 0, 0))
    return pl.pallas_call(
        functools.partial(_ffn_up_kernel, tiles_per_seq=seq // tm, col_chunk=col_chunk),
        grid=(n // tm,),
        in_specs=[pl.BlockSpec((tm, d), lambda i: (i, 0)), wspec, cspec, cspec],
        out_specs=pl.BlockSpec((tm, width), lambda i: (i, 0)),
        out_shape=jax.ShapeDtypeStruct((n, width), BF16),
        scratch_shapes=[pltpu.VMEM((2, tm + SUBLANES, col_chunk), F32)] * 2
        + [pltpu.VMEM((SUBLANES, width), F32)] * 2,
        compiler_params=_params("arbitrary"),
        name="ffn_up",
    )(x, w, cg, cv)


def _ffn_down_kernel(x_ref, h_ref, w_ref, g_ref, b_ref, o_ref, *, alpha):
    def project(r):
        return jnp.dot(h_ref[r:r + LN_ROWS, :], w_ref[...], preferred_element_type=F32)

    tm = x_ref.shape[0]
    ahead = project(0)
    for r in range(0, tm, LN_ROWS):
        f = ahead
        if r + LN_ROWS < tm:
            ahead = project(r + LN_ROWS)
        o_ref[r:r + LN_ROWS, :] = _layer_norm(alpha * x_ref[r:r + LN_ROWS, :] + f, g_ref[...], b_ref[...])


def _ffn_down(x, h, w, g, b, layer, alpha):
    n, d = x.shape
    width = h.shape[1]
    tm = min(ROW_TILE, n)
    vec = pl.BlockSpec((None, 1, d), lambda i: (layer, 0, 0))
    return pl.pallas_call(
        functools.partial(_ffn_down_kernel, alpha=alpha),
        grid=(n // tm,),
        in_specs=[pl.BlockSpec((tm, d), lambda i: (i, 0)), pl.BlockSpec((tm, width), lambda i: (i, 0)),
                  pl.BlockSpec((None, width, d), lambda i: (layer, 0, 0)), vec, vec],
        out_specs=pl.BlockSpec((tm, d), lambda i: (i, 0)),
        out_shape=jax.ShapeDtypeStruct((n, d), F32),
        compiler_params=_params("arbitrary"),
        name="ffn_down",
    )(x, h, w, g, b)


def _pad_last(a, width):
    return jnp.pad(a, [(0, 0)] * (a.ndim - 1) + [(0, width - a.shape[-1])])


def kernel(x, w_in, gdn_conv, gdn_a_log, gdn_dt_bias, gdn_norm, diff_lambda, diff_norm, w_out,
           ln1_g, ln1_b, w_up, ffn_conv, w_down, ln2_g, ln2_b):
    batch, seq, d = x.shape
    depth = w_in.shape[0]
    d_ff = w_down.shape[1]
    ff_pad = -(-d_ff // LANES) * LANES
    alpha = (2 * depth) ** 0.25

    bd0 = A_W
    bc0 = A_W + 2 * GDN_HEADS
    w_in_r = jnp.concatenate(
        [w_in[:, :, :A_W], w_in[:, :, bc0:], _pad_last(w_in[:, :, bd0:bc0], LANES)], axis=-1).astype(BF16)
    scale = jnp.concatenate([
        jnp.full((DSW_W,), DSW_HEAD_DIM ** -0.5 * LOG2E, F32), jnp.ones((2 * DSW_W,), F32),
        jnp.full((DIFF_QK_W,), DIFF_QK_DIM ** -0.5 * LOG2E, F32), jnp.ones((DIFF_QK_W + DIFF_W,), F32)])[None, :]
    lane_row = lambda v, off: jnp.pad(v, ((0, 0), (off, LANES - off - v.shape[1])))[:, None, :]
    alog_row = lane_row(gdn_a_log, GDN_HEADS)
    dtb_row = lane_row(gdn_dt_bias, GDN_HEADS)
    gdn_norm_row = gdn_norm[:, None, :]
    diff_norm_row = jnp.tile(diff_norm, (1, DIFF_HEADS))[:, None, :]
    lam_init = jnp.asarray([0.8 - 0.6 * math.exp(-0.3 * l) for l in range(depth)], F32)[:, None, None]
    w_out_b = w_out.astype(BF16)
    w_up_r = jnp.concatenate(
        [_pad_last(w_up[:, :, :d_ff], ff_pad), _pad_last(w_up[:, :, d_ff:], ff_pad)], axis=-1).astype(BF16)
    cg = _pad_last(ffn_conv[:, :, :d_ff], ff_pad)
    cv = _pad_last(ffn_conv[:, :, d_ff:], ff_pad)
    w_down_b = jnp.pad(w_down, ((0, 0), (0, ff_pad - d_ff), (0, 0))).astype(BF16)
    vec3 = lambda v: v[:, None, :]

    xf = x.reshape(batch * seq, d)
    for l in range(depth):
        a, bc, bd = _in_proj(xf, w_in_r, scale, gdn_conv, l, seq)
        bc3 = bc.reshape(batch, seq, BC_W)
        ya = _gdn(a.reshape(batch, seq, A_W), bd.reshape(batch, seq, LANES),
                  alog_row, dtb_row, gdn_norm_row, l)
        yb = _dsw(bc3)
        yc = _diff(bc3, diff_lambda, lam_init, diff_norm_row, l)
        n = batch * seq
        x1 = _out_ln(xf, ya.reshape(n, GDN_W), yb.reshape(n, DSW_W), yc.reshape(n, DIFF_W),
                     w_out_b, vec3(ln1_g), vec3(ln1_b), l, alpha)
        h = _ffn_up(x1, w_up_r, cg, cv, l, seq)
        xf = _ffn_down(x1, h, w_down_b, vec3(ln2_g), vec3(ln2_b), l, alpha)
    return xf.reshape(batch, seq, d)
```

```python
import functools
import math

import jax
import jax.numpy as jnp
from jax import lax
from jax.experimental import pallas as pl
from jax.experimental.pallas import tpu as pltpu

F32 = jnp.float32
BF16 = jnp.bfloat16

LANES = 128
SUBLANES = 8
VMEM_LIMIT = 48 * 1024 * 1024

GDN_HEADS = 4
GDN_HEAD_DIM = 128
GDN_CONV = 4
DSW_HEADS = 4
DSW_HEAD_DIM = 64
DIFF_HEADS = 4
DIFF_QK_DIM = 32
DIFF_V_DIM = 64
FFN_CONV = 3
EPS = 1e-5

GDN_W = GDN_HEADS * GDN_HEAD_DIM
DSW_W = DSW_HEADS * DSW_HEAD_DIM
DIFF_W = DIFF_HEADS * DIFF_V_DIM
DIFF_QK_W = DIFF_HEADS * 2 * DIFF_QK_DIM
A_W = 4 * GDN_W
BC_W = 3 * DSW_W + 2 * DIFF_QK_W + DIFF_W

ROW_TILE = 512
LN_ROW_TILE = 1024
LN_ROWS = 128
PROJ_CHUNK = 256
EPI_ROWS = 64
ATT_Q_TILE = 512
ATT_K_TILE = 512
ATT_AUG = 16
LOG2E = math.log2(math.e)
GDN_CHUNK = 128
GDN_STEP_HEADS = 2
GDN_GROUP = 4
GDN_SPLIT_SPAN = 16
DSW_MAX_BAND = 512
DSW_FAR_TILE = -(-(DSW_MAX_BAND + ATT_K_TILE) // ATT_Q_TILE)


def _dot(a, b):
    return jnp.dot(a.astype(BF16), b.astype(BF16), preferred_element_type=F32)


def _dot_nt(a, b):
    return lax.dot_general(a.astype(BF16), b.astype(BF16), (((1,), (1,)), ((), ())),
                           preferred_element_type=F32)


def _split_bf16(a):
    hi = a.astype(BF16)
    return hi, (a - hi.astype(F32)).astype(BF16)


def _split3_bf16(a):
    hi, mid = _split_bf16(a)
    return hi, mid, (a - hi.astype(F32) - mid.astype(F32)).astype(BF16)


def _dot_split(a, b):
    ah, al = _split_bf16(a)
    bh, bl = _split_bf16(b)
    return jnp.dot(jnp.concatenate([ah, al, ah], axis=1), jnp.concatenate([bh, bh, bl], axis=0),
                   preferred_element_type=F32)


def _sigmoid(x):
    return 1.0 / (1.0 + jnp.exp(-x))


def _params(*sem):
    return pltpu.CompilerParams(dimension_semantics=sem, vmem_limit_bytes=VMEM_LIMIT)


def _in_proj_kernel(x_ref, w_ref, scale_ref, cw_ref, a_ref, bc_ref, bd_ref,
                    work, carry, *, tiles_per_seq):
    tm = x_ref.shape[0]
    x = x_ref[...].astype(BF16)

    @pl.when(pl.program_id(0) % tiles_per_seq == 0)
    def _():
        carry[...] = jnp.zeros(carry.shape, F32)

    first = SUBLANES - (GDN_CONV - 1)

    def project_conv(cols, slot):
        work[slot, 0:SUBLANES, :] = carry[:, cols]
        work[slot, SUBLANES:SUBLANES + tm, :] = jnp.dot(x, w_ref[:, cols], preferred_element_type=F32)
        carry[:, cols] = work[slot, tm:tm + SUBLANES, :]

    def finish_conv(cols, slot, norm_scale):
        taps = [jnp.broadcast_to(0.5 * cw_ref[j:j + 1, cols], (EPI_ROWS, PROJ_CHUNK)) for j in range(GDN_CONV)]
        for r in range(0, tm, EPI_ROWS):
            y = work[slot, first + r:first + r + EPI_ROWS, :] * taps[0]
            for j in range(1, GDN_CONV):
                y = y + work[slot, first + j + r:first + j + r + EPI_ROWS, :] * taps[j]
            y = y + y * jnp.tanh(y)
            if norm_scale is not None:
                heads = [y[:, h * GDN_HEAD_DIM:(h + 1) * GDN_HEAD_DIM] for h in range(PROJ_CHUNK // GDN_HEAD_DIM)]
                y = jnp.concatenate(
                    [v * (lax.rsqrt(jnp.sum(v * v, axis=-1, keepdims=True) + 1e-6) * norm_scale)
                     for v in heads], axis=1)
            a_ref[r:r + EPI_ROWS, cols] = y

    def plain(first_col, cols, store):
        wcols = slice(first_col + cols.start, first_col + cols.stop)
        return lambda: store(jnp.dot(x, w_ref[:, wcols], preferred_element_type=F32))

    def store_z(cols):
        def store(u):
            a_ref[:, cols] = u
        return store

    def store_bc(cols):
        def store(u):
            bc_ref[:, cols] = (u * scale_ref[:, cols]).astype(BF16)
        return store

    def store_bd(u):
        bd_ref[...] = u

    chunks = lambda base, width: [slice(base + c, base + c + PROJ_CHUNK) for c in range(0, width, PROJ_CHUNK)]
    heavy = ([(c, GDN_HEAD_DIM ** -0.5) for c in chunks(0, GDN_W)] + [(c, 1.0) for c in chunks(GDN_W, GDN_W)]
             + [(c, None) for c in chunks(2 * GDN_W, GDN_W)])
    light = ([plain(A_W, c, store_bc(c)) for c in chunks(0, BC_W)]
             + [plain(0, c, store_z(c)) for c in chunks(3 * GDN_W, GDN_W)]
             + [plain(A_W + BC_W, slice(0, LANES), store_bd)])
    project_conv(heavy[0][0], 0)
    for n, (cols, norm_scale) in enumerate(heavy):
        if n + 1 < len(heavy):
            project_conv(heavy[n + 1][0], (n + 1) % 2)
        light[n]()
        finish_conv(cols, n % 2, norm_scale)
    for job in light[len(heavy):]:
        job()


def _in_proj(x, w, scale, conv_w, layer, seq):
    n, d = x.shape
    tm = min(ROW_TILE, seq)
    return pl.pallas_call(
        functools.partial(_in_proj_kernel, tiles_per_seq=seq // tm),
        grid=(n // tm,),
        in_specs=[
            pl.BlockSpec((tm, d), lambda i: (i, 0)),
            pl.BlockSpec((None, d, A_W + BC_W + LANES), lambda i: (layer, 0, 0)),
            pl.BlockSpec((1, BC_W), lambda i: (0, 0)),
            pl.BlockSpec((None, GDN_CONV, 3 * GDN_W), lambda i: (layer, 0, 0)),
        ],
        out_specs=[
            pl.BlockSpec((tm, A_W), lambda i: (i, 0)),
            pl.BlockSpec((tm, BC_W), lambda i: (i, 0)),
            pl.BlockSpec((tm, LANES), lambda i: (i, 0)),
        ],
        out_shape=[
            jax.ShapeDtypeStruct((n, A_W), F32),
            jax.ShapeDtypeStruct((n, BC_W), BF16),
            jax.ShapeDtypeStruct((n, LANES), F32),
        ],
        scratch_shapes=[pltpu.VMEM((2, tm + SUBLANES, PROJ_CHUNK), F32), pltpu.VMEM((SUBLANES, 3 * GDN_W), F32)],
        compiler_params=_params("arbitrary"),
        name="in_proj",
    )(x, w, scale, conv_w)


def _gdn_kernel(q_ref, k_ref, v_ref, z_ref, bd_ref, alog_ref, dtb_ref, nw_ref, y_ref,
                beta_s, g_s, sa_s, sb_s, oq_s, oc_s, gl_s, *, seq):
    C = GDN_CHUNK
    n_chunks = seq // C
    n_heads = q_ref.shape[1] // LANES
    head0 = pl.program_id(1) * n_heads
    lanes_of = lambda h: slice(h * LANES, (h + 1) * LANES)

    bd = bd_ref[...]
    lane = lax.broadcasted_iota(jnp.int32, (seq, LANES), 1)
    beta_all = _sigmoid(bd)
    xg = bd + dtb_ref[...]
    softplus = jnp.maximum(xg, 0.0) + jnp.log(1.0 + jnp.exp(-jnp.abs(xg)))
    g_all = -jnp.exp(alog_ref[...]) * softplus
    for h in range(n_heads):
        beta = jnp.sum(jnp.where(lane == head0 + h, beta_all, 0.0), axis=-1, keepdims=True)
        g = jnp.sum(jnp.where(lane == head0 + h + GDN_HEADS, g_all, 0.0), axis=-1, keepdims=True)
        beta_s[h] = jnp.broadcast_to(beta, (seq, LANES))
        g_s[h] = jnp.broadcast_to(g, (seq, LANES))

    ri = lax.broadcasted_iota(jnp.int32, (C, C), 0)
    ci = lax.broadcasted_iota(jnp.int32, (C, C), 1)
    lower_incl = ri >= ci
    strict = ri > ci
    tri = jnp.where(lower_incl, 1.0, 0.0).astype(BF16)
    tri3 = jnp.concatenate([tri, tri, tri], axis=1)

    def chunk_group(gi, carry):
        items = [(h, gi * GDN_GROUP + j) for j in range(GDN_GROUP) for h in range(n_heads)]
        grp = range(len(items))
        rows = [pl.ds(pl.multiple_of(c * C, C), C) for _, c in items]
        qc = [q_ref[r, lanes_of(h)] for (h, _), r in zip(items, rows)]
        kc = [k_ref[r, lanes_of(h)] for (h, _), r in zip(items, rows)]
        vc = [v_ref[r, lanes_of(h)] for (h, _), r in zip(items, rows)]
        bb = [beta_s[h, r, :] for (h, _), r in zip(items, rows)]
        gg = [g_s[h, r, :] for (h, _), r in zip(items, rows)]
        cum = [jnp.dot(tri3, jnp.concatenate(_split3_bf16(g), axis=0), preferred_element_type=F32)
               for g in gg]
        decay = [jnp.exp(jnp.where(lower_incl, c - c.T, -jnp.inf)) for c in cum]
        kb = [k * b for k, b in zip(kc, bb)]
        scores = [_dot_nt(jnp.concatenate([x, q], axis=0), k) for x, q, k in zip(kb, qc, kc)]
        m = [jnp.where(strict, s[0:C, :] * d, 0.0) for s, d in zip(scores, decay)]
        qk = [jnp.where(lower_incl, s[C:, :] * d, 0.0) for s, d in zip(scores, decay)]
        nmat = [-x for x in m]
        p = [_dot_split(x, x) for x in m]
        span = 2
        while 2 * span < C:
            dot = _dot_split if span < GDN_SPLIT_SPAN else _dot
            both = [dot(jnp.concatenate([n, x], axis=0), x) for n, x in zip(nmat, p)]
            nmat = [n + x + b[0:C, :] for n, x, b in zip(nmat, p, both)]
            p = [b[C:, :] for b in both]
            span *= 2
        nmat = [n + x + _dot(n, x) for n, x in zip(nmat, p)]
        ecum = [jnp.exp(c) for c in cum]
        vb = [v * b for v, b in zip(vc, bb)]
        kbe = [x * e for x, e in zip(kb, ecum)]
        wu = [jnp.concatenate([x, y], axis=1) for x, y in zip(kbe, vb)]
        wu = [x + _dot(n, x) for n, x in zip(nmat, wu)]
        qd = [q * e for q, e in zip(qc, ecum)]
        cum_last = [c[C - 1:C, :] for c in cum]
        ktt = [(k * jnp.exp(cl - c)).T for k, cl, c in zip(kc, cum_last, cum)]
        prod = [_dot(jnp.concatenate([kt, a], axis=0), x) for kt, a, x in zip(ktt, qk, wu)]
        for j, ((h, c), r) in enumerate(zip(items, rows)):
            sa_s[h, r, :] = -prod[j][0:C, 0:C]
            sb_s[h, r, :] = prod[j][0:C, C:]
            oq_s[h, r, :] = qd[j] - prod[j][C:, 0:C]
            oc_s[h, r, :] = prod[j][C:, C:]
            gl_s[h, pl.ds(pl.multiple_of(c * SUBLANES, SUBLANES), SUBLANES), :] = (
                jnp.broadcast_to(jnp.exp(cum_last[j]), (SUBLANES, LANES)))
        return carry

    lax.fori_loop(0, n_chunks // GDN_GROUP, chunk_group, 0)

    nw = nw_ref[...]

    def scan_step(c, states):
        rows = pl.ds(pl.multiple_of(c * C, C), C)
        new = []
        for h, s in enumerate(states):
            gl = gl_s[h, pl.ds(pl.multiple_of(c * SUBLANES, SUBLANES), 1), :]
            new.append(s * gl + (_dot(sa_s[h, rows, :], s) + sb_s[h, rows, :]))
        for h, s in enumerate(states):
            o = _dot(oq_s[h, rows, :], s) + oc_s[h, rows, :]
            z = z_ref[rows, lanes_of(h)]
            o = o * lax.rsqrt(jnp.mean(o * o, axis=-1, keepdims=True) + EPS) * nw
            y_ref[rows, lanes_of(h)] = (o * (z * _sigmoid(z))).astype(y_ref.dtype)
        return tuple(new)

    lax.fori_loop(0, n_chunks, scan_step,
                  tuple(jnp.zeros((GDN_HEAD_DIM, GDN_HEAD_DIM), F32) for _ in range(n_heads)))


def _gdn(a, bd, alog_row, dtb_row, norm_row, layer):
    b, t, _ = a.shape
    width = GDN_STEP_HEADS * GDN_HEAD_DIM
    steps = GDN_HEADS // GDN_STEP_HEADS
    tok = lambda j: pl.BlockSpec((None, t, width), lambda i, p: (i, 0, j * steps + p))
    row = pl.BlockSpec((None, 1, LANES), lambda i, p: (layer, 0, 0))
    head_buf = pltpu.VMEM((GDN_STEP_HEADS, t, LANES), F32)
    return pl.pallas_call(
        functools.partial(_gdn_kernel, seq=t),
        grid=(b, steps),
        in_specs=[tok(0), tok(1), tok(2), tok(3),
                  pl.BlockSpec((None, t, LANES), lambda i, p: (i, 0, 0)), row, row, row],
        out_specs=pl.BlockSpec((None, t, width), lambda i, p: (i, 0, p)),
        out_shape=jax.ShapeDtypeStruct((b, t, GDN_W), BF16),
        scratch_shapes=[head_buf] * 6
        + [pltpu.VMEM((GDN_STEP_HEADS, t // GDN_CHUNK * SUBLANES, LANES), F32)],
        compiler_params=_params("arbitrary", "arbitrary"),
        name="gdn",
    )(a, a, a, a, bd, alog_row, dtb_row, norm_row)


def _stage_values(v_ref, vt_s, seq, dv):
    aug = dv + ATT_AUG
    ones_row = jnp.where(lax.broadcasted_iota(jnp.int32, (ATT_AUG, ATT_K_TILE), 0) == 0, 1.0, 0.0).astype(BF16)
    for g in range(v_ref.shape[1] // LANES):
        for t in range(seq // ATT_K_TILE):
            for h in range(LANES // dv):
                vt_s[g, t, h * aug + dv:(h + 1) * aug, :] = ones_row
        for c in range(seq // LANES):
            t, off = divmod(c * LANES, ATT_K_TILE)
            blk = v_ref[c * LANES:(c + 1) * LANES, g * LANES:(g + 1) * LANES].astype(F32).T.astype(BF16)
            for h in range(LANES // dv):
                vt_s[g, t, h * aug:h * aug + dv, off:off + LANES] = blk[h * dv:(h + 1) * dv, :]


def _stack_masked(q, width):
    lane = lax.broadcasted_iota(jnp.int32, q.shape, 1)
    return jnp.concatenate(
        [jnp.where((lane >= j * width) & (lane < (j + 1) * width), q, 0.0).astype(BF16)
         for j in range(LANES // width)], axis=0)


def _softmax_init(rows, width):
    return (jnp.full((1, width), -jnp.inf, F32), jnp.zeros((rows, width), F32))


def _softmax_tile(state, scores, cnt, vt):
    ps, alphas, maxes = [], [], []
    for (m_old, _), s in zip(state, scores):
        m_new = jnp.maximum(m_old, jnp.max(s, axis=0, keepdims=True))
        p = jnp.exp2(s - m_new)
        if cnt is not None:
            p = p * cnt
        alphas.append(jnp.exp2(m_old - m_new))
        maxes.append(m_new)
        ps.append(p.astype(BF16))
    pv = jnp.dot(vt, jnp.concatenate(ps, axis=1), preferred_element_type=F32)
    width = pv.shape[1] // len(state)
    return [(m, a * acc + pv[:, n * width:(n + 1) * width])
            for n, (m, a, (_, acc)) in enumerate(zip(maxes, alphas, state))]


def _dsw_kernel(q_ref, k_ref, v_ref, o_ref, vt_s, cnt_s, bias_s, *, seq):
    tq, tk = ATT_Q_TILE, ATT_K_TILE
    blocks = q_ref.shape[1] // LANES
    dv = DSW_HEAD_DIM
    heads = LANES // dv
    aug = dv + ATT_AUG
    _stage_values(v_ref, vt_s, seq, dv)

    @pl.when(pl.program_id(0) == 0)
    def _():
        kr = lax.broadcasted_iota(jnp.int32, (tk, tq), 0)
        qc = lax.broadcasted_iota(jnp.int32, (tk, tq), 1)
        for d in range(DSW_FAR_TILE + 1):
            delta = d * tq + qc - kr
            causal = delta >= 0
            cnt = (jnp.where(causal & (delta <= 128), 1.0, 0.0)
                   + jnp.where(causal & (delta <= DSW_MAX_BAND) & ((delta & 3) == 0), 1.0, 0.0)
                   + jnp.where(causal & ((delta & 15) == 0), 1.0, 0.0))
            cnt_s[d] = cnt.astype(F32)
            bias_s[d] = jnp.where(cnt > 0.0, 0.0, -jnp.inf).astype(F32)

    for qi in range(seq // tq):
        qrows = slice(qi * tq, (qi + 1) * tq)
        qstack = [_stack_masked(q_ref[qrows, g * LANES:(g + 1) * LANES].astype(F32), DSW_HEAD_DIM)
                  for g in range(blocks)]
        state = [_softmax_init(aug, tq) for _ in range(blocks * heads)]
        for kj in range((qi * tq) // tk + 1):
            krows = slice(kj * tk, (kj + 1) * tk)
            far = min(qi - kj * (tk // tq), DSW_FAR_TILE)
            cnt = cnt_s[far] if far < DSW_FAR_TILE else None
            bias = bias_s[far]
            s = [_dot_nt(k_ref[krows, g * LANES:(g + 1) * LANES], qstack[g]) for g in range(blocks)]
            for g in range(blocks):
                for h in range(heads):
                    i = g * heads + h
                    state[i:i + 1] = _softmax_tile(state[i:i + 1], [s[g][:, h * tq:(h + 1) * tq] + bias],
                                                   cnt, vt_s[g, kj, h * aug:(h + 1) * aug, :])
        o = jnp.concatenate([acc[0:dv, :] / acc[dv:dv + 1, :] for _, acc in state], axis=0)
        o_ref[qrows, :] = o.T.astype(o_ref.dtype)


def _dsw(bc):
    b, t, _ = bc.shape
    spec = lambda j: pl.BlockSpec((None, t, DSW_W), lambda i: (i, 0, j))
    return pl.pallas_call(
        functools.partial(_dsw_kernel, seq=t),
        grid=(b,),
        in_specs=[spec(0), spec(1), spec(2)],
        out_specs=pl.BlockSpec((None, t, DSW_W), lambda i: (i, 0, 0)),
        out_shape=jax.ShapeDtypeStruct((b, t, DSW_W), BF16),
        scratch_shapes=[pltpu.VMEM((DSW_W // LANES, t // ATT_K_TILE,
                                    LANES // DSW_HEAD_DIM * (DSW_HEAD_DIM + ATT_AUG), ATT_K_TILE), BF16)]
        + [pltpu.VMEM((DSW_FAR_TILE + 1, ATT_K_TILE, ATT_Q_TILE), F32)] * 2,
        compiler_params=_params("arbitrary"),
        name="dsw",
    )(bc, bc, bc)


def _diff_kernel(q_ref, k_ref, v_ref, lam_ref, laminit_ref, nw_ref, o_ref, vt_s, *, seq):
    tq, tk = ATT_Q_TILE, ATT_K_TILE
    blocks = q_ref.shape[1] // LANES
    dv = DIFF_V_DIM
    heads = LANES // dv
    aug = dv + ATT_AUG
    _stage_values(v_ref, vt_s, seq, dv)
    kr = lax.broadcasted_iota(jnp.int32, (tk, tq), 0)
    qc = lax.broadcasted_iota(jnp.int32, (tk, tq), 1)
    lv = lam_ref[...]
    lam_init = laminit_ref[...]
    lam = (jnp.exp(jnp.sum(lv[0:1, :] * lv[1:2, :], keepdims=True))
           - jnp.exp(jnp.sum(lv[2:3, :] * lv[3:4, :], keepdims=True)) + lam_init)
    nw = nw_ref[...]

    for qi in range(seq // tq):
        qrows = slice(qi * tq, (qi + 1) * tq)
        qstack = [_stack_masked(q_ref[qrows, g * LANES:(g + 1) * LANES].astype(F32), DIFF_QK_DIM)
                  for g in range(blocks)]
        state = [_softmax_init(dv + ATT_AUG, tq) for _ in range(2 * heads * blocks)]
        n_full = (qi * tq) // tk
        for kj in range(n_full + 1):
            krows = slice(kj * tk, (kj + 1) * tk)
            valid = None if kj < n_full else kr + (kj * tk - qi * tq) <= qc
            s = [_dot_nt(k_ref[krows, g * LANES:(g + 1) * LANES], qstack[g])
                 for g in range(blocks)]
            for g in range(blocks):
                for h in range(heads):
                    scores = [s[g][:, (2 * h + mp) * tq:(2 * h + mp + 1) * tq] for mp in range(2)]
                    if valid is not None:
                        scores = [jnp.where(valid, x, -jnp.inf) for x in scores]
                    first = 2 * (heads * g + h)
                    state[first:first + 2] = _softmax_tile(state[first:first + 2], scores, None,
                                                           vt_s[g, kj, h * aug:(h + 1) * aug, :])
        att = [acc[0:dv, :] / acc[dv:dv + 1, :] for _, acc in state]
        halves = []
        for h in range(heads * blocks):
            o = att[2 * h] - lam * att[2 * h + 1]
            halves.append(o * lax.rsqrt(jnp.mean(o * o, axis=0, keepdims=True) + EPS))
        o = jnp.concatenate(halves, axis=0).T
        o_ref[qrows, :] = (o * nw * (1.0 - lam_init)).astype(o_ref.dtype)


def _diff(bc, lam_vecs, lam_init, norm_row):
    b, t, _ = bc.shape
    base = 3 * DSW_W // DIFF_W
    spec = lambda j: pl.BlockSpec((None, t, DIFF_W), lambda i: (i, 0, base + j))
    return pl.pallas_call(
        functools.partial(_diff_kernel, seq=t),
        grid=(b,),
        in_specs=[spec(0), spec(1), spec(2),
                  pl.BlockSpec((4, DIFF_QK_DIM), lambda i: (0, 0)),
                  pl.BlockSpec((1, 1), lambda i: (0, 0)),
                  pl.BlockSpec((1, DIFF_W), lambda i: (0, 0))],
        out_specs=pl.BlockSpec((None, t, DIFF_W), lambda i: (i, 0, 0)),
        out_shape=jax.ShapeDtypeStruct((b, t, DIFF_W), BF16),
        scratch_shapes=[pltpu.VMEM((DIFF_W // LANES, t // ATT_K_TILE,
                                    LANES // DIFF_V_DIM * (DIFF_V_DIM + ATT_AUG), ATT_K_TILE), BF16)],
        compiler_params=_params("arbitrary"),
        name="diff",
    )(bc, bc, bc, lam_vecs, lam_init, norm_row)


def _layer_norm(h, g, b):
    mu = jnp.mean(h, axis=-1, keepdims=True)
    hc = h - mu
    var = jnp.mean(hc * hc, axis=-1, keepdims=True)
    return hc * lax.rsqrt(var + EPS) * g + b


def _residual_layer_norm(project, x_ref, g_ref, b_ref, o_ref, alpha):
    tm = x_ref.shape[0]
    ahead = project(slice(0, LN_ROWS))
    for r in range(0, tm, LN_ROWS):
        y = ahead
        if r + LN_ROWS < tm:
            ahead = project(slice(r + LN_ROWS, r + 2 * LN_ROWS))
        o_ref[r:r + LN_ROWS, :] = _layer_norm(alpha * x_ref[r:r + LN_ROWS, :] + y, g_ref[...], b_ref[...])


def _out_ln_kernel(x_ref, ya_ref, yb_ref, yc_ref, w_ref, g_ref, b_ref, o_ref, *, alpha):
    def project(rows):
        mixed = jnp.concatenate([ya_ref[rows, :], yb_ref[rows, :], yc_ref[rows, :]], axis=1)
        return jnp.dot(mixed, w_ref[...], preferred_element_type=F32)

    _residual_layer_norm(project, x_ref, g_ref, b_ref, o_ref, alpha)


def _out_ln(x, ya, yb, yc, w, g, b, layer, alpha):
    n, d = x.shape
    tm = min(LN_ROW_TILE, n)
    rows = lambda width: pl.BlockSpec((tm, width), lambda i: (i, 0))
    vec = pl.BlockSpec((None, 1, d), lambda i: (layer, 0, 0))
    return pl.pallas_call(
        functools.partial(_out_ln_kernel, alpha=alpha),
        grid=(n // tm,),
        in_specs=[rows(d), rows(GDN_W), rows(DSW_W), rows(DIFF_W),
                  pl.BlockSpec((None, d, d), lambda i: (layer, 0, 0)), vec, vec],
        out_specs=rows(d),
        out_shape=jax.ShapeDtypeStruct((n, d), F32),
        compiler_params=_params("arbitrary"),
        name="out_ln",
    )(x, ya, yb, yc, w, g, b)


def _ffn_up_kernel(x_ref, w_ref, cg_ref, cv_ref, h_ref, work_g, work_v, carry_g, carry_v,
                   *, tiles_per_seq, col_chunk):
    tm = x_ref.shape[0]
    width = h_ref.shape[1]

    @pl.when(pl.program_id(0) % tiles_per_seq == 0)
    def _():
        carry_g[...] = jnp.zeros(carry_g.shape, F32)
        carry_v[...] = jnp.zeros(carry_v.shape, F32)

    x = x_ref[...].astype(BF16)
    first = SUBLANES - (FFN_CONV - 1)

    def project(c, slot):
        cols = slice(c, c + col_chunk)
        for first_col, work, carry in ((0, work_g, carry_g), (width, work_v, carry_v)):
            work[slot, 0:SUBLANES, :] = carry[:, cols]
            work[slot, SUBLANES:SUBLANES + tm, :] = jnp.dot(
                x, w_ref[:, first_col + c:first_col + c + col_chunk], preferred_element_type=F32)
            carry[:, cols] = work[slot, tm:tm + SUBLANES, :]

    def conv(work, slot, taps, r):
        acc = work[slot, first + r:first + r + EPI_ROWS, :] * taps[0]
        for j in range(1, FFN_CONV):
            acc = acc + work[slot, first + j + r:first + j + r + EPI_ROWS, :] * taps[j]
        return acc

    project(0, 0)
    for n, c in enumerate(range(0, width, col_chunk)):
        slot = n % 2
        if c + col_chunk < width:
            project(c + col_chunk, 1 - slot)
        cols = slice(c, c + col_chunk)
        taps_g = [jnp.broadcast_to(0.5 * cg_ref[j:j + 1, cols], (EPI_ROWS, col_chunk)) for j in range(FFN_CONV)]
        taps_v = [jnp.broadcast_to(cv_ref[j:j + 1, cols], (EPI_ROWS, col_chunk)) for j in range(FFN_CONV)]
        for r in range(0, tm, EPI_ROWS):
            half = conv(work_g, slot, taps_g, r)
            val = conv(work_v, slot, taps_v, r)
            h_ref[r:r + EPI_ROWS, cols] = ((half + half * jnp.tanh(half)) * val).astype(h_ref.dtype)


def _ffn_up(x, w, cg, cv, layer, seq):
    n, d = x.shape
    width = w.shape[-1] // 2
    tm = min(ROW_TILE, seq)
    col_chunk = 256
    wspec = pl.BlockSpec((None, d, 2 * width), lambda i: (layer, 0, 0))
    cspec = pl.BlockSpec((None, FFN_CONV, width), lambda i: (layer, 0, 0))
    return pl.pallas_call(
        functools.partial(_ffn_up_kernel, tiles_per_seq=seq // tm, col_chunk=col_chunk),
        grid=(n // tm,),
        in_specs=[pl.BlockSpec((tm, d), lambda i: (i, 0)), wspec, cspec, cspec],
        out_specs=pl.BlockSpec((tm, width), lambda i: (i, 0)),
        out_shape=jax.ShapeDtypeStruct((n, width), BF16),
        scratch_shapes=[pltpu.VMEM((2, tm + SUBLANES, col_chunk), F32)] * 2
        + [pltpu.VMEM((SUBLANES, width), F32)] * 2,
        compiler_params=_params("arbitrary"),
        name="ffn_up",
    )(x, w, cg, cv)


def _ffn_down_kernel(x_ref, h_ref, w_ref, g_ref, b_ref, o_ref, *, alpha):
    project = lambda rows: jnp.dot(h_ref[rows, :], w_ref[...], preferred_element_type=F32)
    _residual_layer_norm(project, x_ref, g_ref, b_ref, o_ref, alpha)


def _ffn_down(x, h, w, g, b, layer, alpha):
    n, d = x.shape
    width = h.shape[1]
    tm = min(LN_ROW_TILE, n)
    vec = pl.BlockSpec((None, 1, d), lambda i: (layer, 0, 0))
    return pl.pallas_call(
        functools.partial(_ffn_down_kernel, alpha=alpha),
        grid=(n // tm,),
        in_specs=[pl.BlockSpec((tm, d), lambda i: (i, 0)), pl.BlockSpec((tm, width), lambda i: (i, 0)),
                  pl.BlockSpec((None, width, d), lambda i: (layer, 0, 0)), vec, vec],
        out_specs=pl.BlockSpec((tm, d), lambda i: (i, 0)),
        out_shape=jax.ShapeDtypeStruct((n, d), F32),
        compiler_params=_params("arbitrary"),
        name="ffn_down",
    )(x, h, w, g, b)


def _pad_last(a, width):
    return jnp.pad(a, [(0, 0)] * (a.ndim - 1) + [(0, width - a.shape[-1])])


def kernel(x, w_in, gdn_conv, gdn_a_log, gdn_dt_bias, gdn_norm, diff_lambda, diff_norm, w_out,
           ln1_g, ln1_b, w_up, ffn_conv, w_down, ln2_g, ln2_b):
    batch, seq, d = x.shape
    depth = w_in.shape[0]
    d_ff = w_down.shape[1]
    ff_pad = -(-d_ff // LANES) * LANES
    alpha = (2 * depth) ** 0.25

    bd0 = A_W
    bc0 = A_W + 2 * GDN_HEADS
    w_in_r = jnp.concatenate(
        [w_in[:, :, :A_W], w_in[:, :, bc0:], _pad_last(w_in[:, :, bd0:bc0], LANES)], axis=-1).astype(BF16)
    scale = jnp.concatenate([
        jnp.full((DSW_W,), DSW_HEAD_DIM ** -0.5 * LOG2E, F32), jnp.ones((2 * DSW_W,), F32),
        jnp.full((DIFF_QK_W,), DIFF_QK_DIM ** -0.5 * LOG2E, F32), jnp.ones((DIFF_QK_W + DIFF_W,), F32)])[None, :]
    lane_row = lambda v, off: jnp.pad(v, ((0, 0), (off, LANES - off - v.shape[1])))[:, None, :]
    alog_row = lane_row(gdn_a_log, GDN_HEADS)
    dtb_row = lane_row(gdn_dt_bias, GDN_HEADS)
    gdn_norm_row = gdn_norm[:, None, :]
    diff_norm_row = jnp.tile(diff_norm, (1, DIFF_HEADS))[:, None, :]
    lam_init = jnp.asarray([0.8 - 0.6 * math.exp(-0.3 * l) for l in range(depth)], F32)[:, None, None]
    w_out_b = w_out.astype(BF16)
    w_up_r = jnp.concatenate(
        [_pad_last(w_up[:, :, :d_ff], ff_pad), _pad_last(w_up[:, :, d_ff:], ff_pad)], axis=-1).astype(BF16)
    cg = _pad_last(ffn_conv[:, :, :d_ff], ff_pad)
    cv = _pad_last(ffn_conv[:, :, d_ff:], ff_pad)
    w_down_b = jnp.pad(w_down, ((0, 0), (0, ff_pad - d_ff), (0, 0))).astype(BF16)
    vec3 = lambda v: v[:, None, :]

    xf = x.reshape(batch * seq, d)
    for l in range(depth):
        a, bc, bd = _in_proj(xf, w_in_r, scale, gdn_conv, l, seq)
        bc3 = bc.reshape(batch, seq, BC_W)
        ya = _gdn(a.reshape(batch, seq, A_W), bd.reshape(batch, seq, LANES),
                  alog_row, dtb_row, gdn_norm_row, l)
        yb = _dsw(bc3)
        yc = _diff(bc3, diff_lambda[l], lam_init[l], diff_norm_row[l])
        n = batch * seq
        x1 = _out_ln(xf, ya.reshape(n, GDN_W), yb.reshape(n, DSW_W), yc.reshape(n, DIFF_W),
                     w_out_b, vec3(ln1_g), vec3(ln1_b), l, alpha)
        h = _ffn_up(x1, w_up_r, cg, cv, l, seq)
        xf = _ffn_down(x1, h, w_down_b, vec3(ln2_g), vec3(ln2_b), l, alpha)
    return xf.reshape(batch, seq, d)
```

```python
import functools
import math

import jax
import jax.numpy as jnp
from jax import lax
from jax.experimental import pallas as pl
from jax.experimental.pallas import tpu as pltpu

F32 = jnp.float32
BF16 = jnp.bfloat16

LANES = 128
SUBLANES = 8
VMEM_LIMIT = 48 * 1024 * 1024

GDN_HEADS = 4
GDN_HEAD_DIM = 128
GDN_CONV = 4
DSW_HEADS = 4
DSW_HEAD_DIM = 64
DIFF_HEADS = 4
DIFF_QK_DIM = 32
DIFF_V_DIM = 64
FFN_CONV = 3
EPS = 1e-5

GDN_W = GDN_HEADS * GDN_HEAD_DIM
DSW_W = DSW_HEADS * DSW_HEAD_DIM
DIFF_W = DIFF_HEADS * DIFF_V_DIM
DIFF_QK_W = DIFF_HEADS * 2 * DIFF_QK_DIM
A_W = 4 * GDN_W
BC_W = 3 * DSW_W + 2 * DIFF_QK_W + DIFF_W

ROW_TILE = 512
LN_ROW_TILE = 1024
LN_ROWS = 128
PROJ_CHUNK = 256
EPI_ROWS = 64
ATT_Q_TILE = 512
ATT_K_TILE = 512
ATT_AUG = 16
LOG2E = math.log2(math.e)
GDN_CHUNK = 128
GDN_STEP_HEADS = 4
GDN_GROUP = 2
GDN_SPLIT_SPAN = 16
DSW_MAX_BAND = 512
DSW_FAR_TILE = -(-(DSW_MAX_BAND + ATT_K_TILE) // ATT_Q_TILE)


def _dot(a, b):
    return jnp.dot(a.astype(BF16), b.astype(BF16), preferred_element_type=F32)


def _dot_nt(a, b):
    return lax.dot_general(a.astype(BF16), b.astype(BF16), (((1,), (1,)), ((), ())),
                           preferred_element_type=F32)


def _split_bf16(a):
    hi = a.astype(BF16)
    return hi, (a - hi.astype(F32)).astype(BF16)


def _split3_bf16(a):
    hi, mid = _split_bf16(a)
    return hi, mid, (a - hi.astype(F32) - mid.astype(F32)).astype(BF16)


def _dot_split(a, b):
    ah, al = _split_bf16(a)
    bh, bl = _split_bf16(b)
    return jnp.dot(jnp.concatenate([ah, al, ah], axis=1), jnp.concatenate([bh, bh, bl], axis=0),
                   preferred_element_type=F32)


def _sigmoid(x):
    return 1.0 / (1.0 + jnp.exp(-x))


def _params(*sem):
    return pltpu.CompilerParams(dimension_semantics=sem, vmem_limit_bytes=VMEM_LIMIT)


def _in_proj_kernel(x_ref, w_ref, scale_ref, cw_ref, a_ref, bc_ref, bd_ref,
                    work, carry, *, tiles_per_seq):
    tm = x_ref.shape[0]
    x = x_ref[...].astype(BF16)

    @pl.when(pl.program_id(0) % tiles_per_seq == 0)
    def _():
        carry[...] = jnp.zeros(carry.shape, F32)

    first = SUBLANES - (GDN_CONV - 1)

    def project_conv(cols, slot):
        work[slot, 0:SUBLANES, :] = carry[:, cols]
        work[slot, SUBLANES:SUBLANES + tm, :] = jnp.dot(x, w_ref[:, cols], preferred_element_type=F32)
        carry[:, cols] = work[slot, tm:tm + SUBLANES, :]

    def finish_conv(cols, slot, norm_scale):
        taps = [jnp.broadcast_to(0.5 * cw_ref[j:j + 1, cols], (EPI_ROWS, PROJ_CHUNK)) for j in range(GDN_CONV)]
        for r in range(0, tm, EPI_ROWS):
            y = work[slot, first + r:first + r + EPI_ROWS, :] * taps[0]
            for j in range(1, GDN_CONV):
                y = y + work[slot, first + j + r:first + j + r + EPI_ROWS, :] * taps[j]
            y = y + y * jnp.tanh(y)
            if norm_scale is not None:
                heads = [y[:, h * GDN_HEAD_DIM:(h + 1) * GDN_HEAD_DIM] for h in range(PROJ_CHUNK // GDN_HEAD_DIM)]
                y = jnp.concatenate(
                    [v * (lax.rsqrt(jnp.sum(v * v, axis=-1, keepdims=True) + 1e-6) * norm_scale)
                     for v in heads], axis=1)
            a_ref[r:r + EPI_ROWS, cols] = y.astype(a_ref.dtype)

    def plain(first_col, cols, store):
        wcols = slice(first_col + cols.start, first_col + cols.stop)
        return lambda: store(jnp.dot(x, w_ref[:, wcols], preferred_element_type=F32))

    def store_z(cols):
        def store(u):
            a_ref[:, cols] = u.astype(a_ref.dtype)
        return store

    def store_bc(cols):
        def store(u):
            bc_ref[:, cols] = (u * scale_ref[:, cols]).astype(BF16)
        return store

    def store_bd(u):
        bd_ref[...] = u

    chunks = lambda base, width: [slice(base + c, base + c + PROJ_CHUNK) for c in range(0, width, PROJ_CHUNK)]
    heavy = ([(c, GDN_HEAD_DIM ** -0.5) for c in chunks(0, GDN_W)] + [(c, 1.0) for c in chunks(GDN_W, GDN_W)]
             + [(c, None) for c in chunks(2 * GDN_W, GDN_W)])
    light = ([plain(A_W, c, store_bc(c)) for c in chunks(0, BC_W)]
             + [plain(0, c, store_z(c)) for c in chunks(3 * GDN_W, GDN_W)]
             + [plain(A_W + BC_W, slice(0, LANES), store_bd)])
    project_conv(heavy[0][0], 0)
    for n, (cols, norm_scale) in enumerate(heavy):
        if n + 1 < len(heavy):
            project_conv(heavy[n + 1][0], (n + 1) % 2)
        light[n]()
        finish_conv(cols, n % 2, norm_scale)
    for job in light[len(heavy):]:
        job()


def _in_proj(x, w, scale, conv_w, layer, seq):
    n, d = x.shape
    tm = min(ROW_TILE, seq)
    return pl.pallas_call(
        functools.partial(_in_proj_kernel, tiles_per_seq=seq // tm),
        grid=(n // tm,),
        in_specs=[
            pl.BlockSpec((tm, d), lambda i: (i, 0)),
            pl.BlockSpec((None, d, A_W + BC_W + LANES), lambda i: (layer, 0, 0)),
            pl.BlockSpec((1, BC_W), lambda i: (0, 0)),
            pl.BlockSpec((None, GDN_CONV, 3 * GDN_W), lambda i: (layer, 0, 0)),
        ],
        out_specs=[
            pl.BlockSpec((tm, A_W), lambda i: (i, 0)),
            pl.BlockSpec((tm, BC_W), lambda i: (i, 0)),
            pl.BlockSpec((tm, LANES), lambda i: (i, 0)),
        ],
        out_shape=[
            jax.ShapeDtypeStruct((n, A_W), BF16),
            jax.ShapeDtypeStruct((n, BC_W), BF16),
            jax.ShapeDtypeStruct((n, LANES), F32),
        ],
        scratch_shapes=[pltpu.VMEM((2, tm + SUBLANES, PROJ_CHUNK), F32), pltpu.VMEM((SUBLANES, 3 * GDN_W), F32)],
        compiler_params=_params("arbitrary"),
        name="in_proj",
    )(x, w, scale, conv_w)


def _gdn_kernel(q_ref, k_ref, v_ref, z_ref, bd_ref, alog_ref, dtb_ref, nw_ref, y_ref,
                beta_s, g_s, sa_s, sb_s, oq_s, oc_s, gl_s, *, seq):
    C = GDN_CHUNK
    n_chunks = seq // C
    n_heads = q_ref.shape[1] // LANES
    head0 = pl.program_id(1) * n_heads
    lanes_of = lambda h: slice(h * LANES, (h + 1) * LANES)

    bd = bd_ref[...]
    lane = lax.broadcasted_iota(jnp.int32, (seq, LANES), 1)
    beta_all = _sigmoid(bd)
    xg = bd + dtb_ref[...]
    softplus = jnp.maximum(xg, 0.0) + jnp.log(1.0 + jnp.exp(-jnp.abs(xg)))
    g_all = -jnp.exp(alog_ref[...]) * softplus
    for h in range(n_heads):
        beta = jnp.sum(jnp.where(lane == head0 + h, beta_all, 0.0), axis=-1, keepdims=True)
        g = jnp.sum(jnp.where(lane == head0 + h + GDN_HEADS, g_all, 0.0), axis=-1, keepdims=True)
        beta_s[h] = jnp.broadcast_to(beta, (seq, LANES))
        g_s[h] = jnp.broadcast_to(g, (seq, LANES))

    ri = lax.broadcasted_iota(jnp.int32, (C, C), 0)
    ci = lax.broadcasted_iota(jnp.int32, (C, C), 1)
    lower_incl = ri >= ci
    strict = ri > ci
    tri = jnp.where(lower_incl, 1.0, 0.0).astype(BF16)
    tri3 = jnp.concatenate([tri, tri, tri], axis=1)

    def chunk_group(gi, carry):
        items = [(h, gi * GDN_GROUP + j) for j in range(GDN_GROUP) for h in range(n_heads)]
        grp = range(len(items))
        rows = [pl.ds(pl.multiple_of(c * C, C), C) for _, c in items]
        qc = [q_ref[r, lanes_of(h)].astype(F32) for (h, _), r in zip(items, rows)]
        kc = [k_ref[r, lanes_of(h)].astype(F32) for (h, _), r in zip(items, rows)]
        vc = [v_ref[r, lanes_of(h)].astype(F32) for (h, _), r in zip(items, rows)]
        bb = [beta_s[h, r, :] for (h, _), r in zip(items, rows)]
        gg = [g_s[h, r, :] for (h, _), r in zip(items, rows)]
        cum = [jnp.dot(tri3, jnp.concatenate(_split3_bf16(g), axis=0), preferred_element_type=F32)
               for g in gg]
        decay = [jnp.exp(jnp.where(lower_incl, c - c.T, -jnp.inf)) for c in cum]
        kb = [k * b for k, b in zip(kc, bb)]
        scores = [_dot_nt(jnp.concatenate([x, q], axis=0), k) for x, q, k in zip(kb, qc, kc)]
        m = [jnp.where(strict, s[0:C, :] * d, 0.0) for s, d in zip(scores, decay)]
        qk = [jnp.where(lower_incl, s[C:, :] * d, 0.0) for s, d in zip(scores, decay)]
        nmat = [-x for x in m]
        p = [_dot_split(x, x) for x in m]
        span = 2
        while 2 * span < C:
            dot = _dot_split if span < GDN_SPLIT_SPAN else _dot
            both = [dot(jnp.concatenate([n, x], axis=0), x) for n, x in zip(nmat, p)]
            nmat = [n + x + b[0:C, :] for n, x, b in zip(nmat, p, both)]
            p = [b[C:, :] for b in both]
            span *= 2
        nmat = [n + x + _dot(n, x) for n, x in zip(nmat, p)]
        ecum = [jnp.exp(c) for c in cum]
        vb = [v * b for v, b in zip(vc, bb)]
        kbe = [x * e for x, e in zip(kb, ecum)]
        wu = [jnp.concatenate([x, y], axis=1) for x, y in zip(kbe, vb)]
        wu = [x + _dot(n, x) for n, x in zip(nmat, wu)]
        qd = [q * e for q, e in zip(qc, ecum)]
        cum_last = [c[C - 1:C, :] for c in cum]
        ktt = [(k * jnp.exp(cl - c)).T for k, cl, c in zip(kc, cum_last, cum)]
        prod = [_dot(jnp.concatenate([kt, a], axis=0), x) for kt, a, x in zip(ktt, qk, wu)]
        for j, ((h, c), r) in enumerate(zip(items, rows)):
            sa_s[h, r, :] = (-prod[j][0:C, 0:C]).astype(sa_s.dtype)
            sb_s[h, r, :] = prod[j][0:C, C:]
            oq_s[h, r, :] = (qd[j] - prod[j][C:, 0:C]).astype(oq_s.dtype)
            oc_s[h, r, :] = prod[j][C:, C:]
            gl_s[h, pl.ds(pl.multiple_of(c * SUBLANES, SUBLANES), SUBLANES), :] = (
                jnp.broadcast_to(jnp.exp(cum_last[j]), (SUBLANES, LANES)))
        return carry

    lax.fori_loop(0, n_chunks // GDN_GROUP, chunk_group, 0)

    nw = nw_ref[...]

    def scan_step(c, states):
        rows = pl.ds(pl.multiple_of(c * C, C), C)
        new = []
        for h, s in enumerate(states):
            gl = gl_s[h, pl.ds(pl.multiple_of(c * SUBLANES, SUBLANES), 1), :]
            new.append(s * gl + (_dot(sa_s[h, rows, :], s) + sb_s[h, rows, :]))
        for h, s in enumerate(states):
            o = _dot(oq_s[h, rows, :], s) + oc_s[h, rows, :]
            z = z_ref[rows, lanes_of(h)].astype(F32)
            o = o * lax.rsqrt(jnp.mean(o * o, axis=-1, keepdims=True) + EPS) * nw
            y_ref[rows, lanes_of(h)] = (o * (z * _sigmoid(z))).astype(y_ref.dtype)
        return tuple(new)

    lax.fori_loop(0, n_chunks, scan_step,
                  tuple(jnp.zeros((GDN_HEAD_DIM, GDN_HEAD_DIM), F32) for _ in range(n_heads)))


def _gdn(a, bd, alog_row, dtb_row, norm_row, layer):
    b, t, _ = a.shape
    width = GDN_STEP_HEADS * GDN_HEAD_DIM
    steps = GDN_HEADS // GDN_STEP_HEADS
    tok = lambda j: pl.BlockSpec((None, t, width), lambda i, p: (i, 0, j * steps + p))
    row = pl.BlockSpec((None, 1, LANES), lambda i, p: (layer, 0, 0))
    head_buf = lambda dtype: pltpu.VMEM((GDN_STEP_HEADS, t, LANES), dtype)
    return pl.pallas_call(
        functools.partial(_gdn_kernel, seq=t),
        grid=(b, steps),
        in_specs=[tok(0), tok(1), tok(2), tok(3),
                  pl.BlockSpec((None, t, LANES), lambda i, p: (i, 0, 0)), row, row, row],
        out_specs=pl.BlockSpec((None, t, width), lambda i, p: (i, 0, p)),
        out_shape=jax.ShapeDtypeStruct((b, t, GDN_W), BF16),
        scratch_shapes=[head_buf(F32), head_buf(F32), head_buf(BF16), head_buf(F32), head_buf(BF16),
                        head_buf(F32)]
        + [pltpu.VMEM((GDN_STEP_HEADS, t // GDN_CHUNK * SUBLANES, LANES), F32)],
        compiler_params=_params("arbitrary", "arbitrary"),
        name="gdn",
    )(a, a, a, a, bd, alog_row, dtb_row, norm_row)


def _stage_values(v_ref, vt_s, seq, dv):
    aug = dv + ATT_AUG
    ones_row = jnp.where(lax.broadcasted_iota(jnp.int32, (ATT_AUG, ATT_K_TILE), 0) == 0, 1.0, 0.0).astype(BF16)
    for g in range(v_ref.shape[1] // LANES):
        for t in range(seq // ATT_K_TILE):
            for h in range(LANES // dv):
                vt_s[g, t, h * aug + dv:(h + 1) * aug, :] = ones_row
        for c in range(seq // LANES):
            t, off = divmod(c * LANES, ATT_K_TILE)
            blk = v_ref[c * LANES:(c + 1) * LANES, g * LANES:(g + 1) * LANES].astype(F32).T.astype(BF16)
            for h in range(LANES // dv):
                vt_s[g, t, h * aug:h * aug + dv, off:off + LANES] = blk[h * dv:(h + 1) * dv, :]


def _stack_masked(q, width):
    lane = lax.broadcasted_iota(jnp.int32, q.shape, 1)
    return jnp.concatenate(
        [jnp.where((lane >= j * width) & (lane < (j + 1) * width), q, 0.0).astype(BF16)
         for j in range(LANES // width)], axis=0)


def _softmax_init(rows, width):
    return (jnp.full((1, width), -jnp.inf, F32), jnp.zeros((rows, width), F32))


def _softmax_tile(state, scores, cnt, vt):
    ps, alphas, maxes = [], [], []
    for (m_old, _), s in zip(state, scores):
        m_new = jnp.maximum(m_old, jnp.max(s, axis=0, keepdims=True))
        p = jnp.exp2(s - m_new)
        if cnt is not None:
            p = p * cnt
        alphas.append(jnp.exp2(m_old - m_new))
        maxes.append(m_new)
        ps.append(p.astype(BF16))
    pv = jnp.dot(vt, jnp.concatenate(ps, axis=1), preferred_element_type=F32)
    width = pv.shape[1] // len(state)
    return [(m, a * acc + pv[:, n * width:(n + 1) * width])
            for n, (m, a, (_, acc)) in enumerate(zip(maxes, alphas, state))]


def _dsw_kernel(q_ref, k_ref, v_ref, o_ref, vt_s, cnt_s, bias_s, *, seq):
    tq, tk = ATT_Q_TILE, ATT_K_TILE
    blocks = q_ref.shape[1] // LANES
    dv = DSW_HEAD_DIM
    heads = LANES // dv
    aug = dv + ATT_AUG
    _stage_values(v_ref, vt_s, seq, dv)

    @pl.when(pl.program_id(0) == 0)
    def _():
        kr = lax.broadcasted_iota(jnp.int32, (tk, tq), 0)
        qc = lax.broadcasted_iota(jnp.int32, (tk, tq), 1)
        for d in range(DSW_FAR_TILE + 1):
            delta = d * tq + qc - kr
            causal = delta >= 0
            cnt = (jnp.where(causal & (delta <= 128), 1.0, 0.0)
                   + jnp.where(causal & (delta <= DSW_MAX_BAND) & ((delta & 3) == 0), 1.0, 0.0)
                   + jnp.where(causal & ((delta & 15) == 0), 1.0, 0.0))
            cnt_s[d] = cnt.astype(F32)
            bias_s[d] = jnp.where(cnt > 0.0, 0.0, -jnp.inf).astype(F32)

    for qi in range(seq // tq):
        qrows = slice(qi * tq, (qi + 1) * tq)
        qstack = [_stack_masked(q_ref[qrows, g * LANES:(g + 1) * LANES].astype(F32), DSW_HEAD_DIM)
                  for g in range(blocks)]
        state = [_softmax_init(aug, tq) for _ in range(blocks * heads)]
        for kj in range((qi * tq) // tk + 1):
            krows = slice(kj * tk, (kj + 1) * tk)
            far = min(qi - kj * (tk // tq), DSW_FAR_TILE)
            cnt = cnt_s[far] if far < DSW_FAR_TILE else None
            bias = bias_s[far]
            s = [_dot_nt(k_ref[krows, g * LANES:(g + 1) * LANES], qstack[g]) for g in range(blocks)]
            for g in range(blocks):
                for h in range(heads):
                    i = g * heads + h
                    state[i:i + 1] = _softmax_tile(state[i:i + 1], [s[g][:, h * tq:(h + 1) * tq] + bias],
                                                   cnt, vt_s[g, kj, h * aug:(h + 1) * aug, :])
        o = jnp.concatenate([acc[0:dv, :] / acc[dv:dv + 1, :] for _, acc in state], axis=0)
        o_ref[qrows, :] = o.T.astype(o_ref.dtype)


def _dsw(bc):
    b, t, _ = bc.shape
    spec = lambda j: pl.BlockSpec((None, t, DSW_W), lambda i: (i, 0, j))
    return pl.pallas_call(
        functools.partial(_dsw_kernel, seq=t),
        grid=(b,),
        in_specs=[spec(0), spec(1), spec(2)],
        out_specs=pl.BlockSpec((None, t, DSW_W), lambda i: (i, 0, 0)),
        out_shape=jax.ShapeDtypeStruct((b, t, DSW_W), BF16),
        scratch_shapes=[pltpu.VMEM((DSW_W // LANES, t // ATT_K_TILE,
                                    LANES // DSW_HEAD_DIM * (DSW_HEAD_DIM + ATT_AUG), ATT_K_TILE), BF16)]
        + [pltpu.VMEM((DSW_FAR_TILE + 1, ATT_K_TILE, ATT_Q_TILE), F32)] * 2,
        compiler_params=_params("arbitrary"),
        name="dsw",
    )(bc, bc, bc)


def _diff_kernel(q_ref, k_ref, v_ref, lam_ref, laminit_ref, nw_ref, o_ref, vt_s, *, seq):
    tq, tk = ATT_Q_TILE, ATT_K_TILE
    blocks = q_ref.shape[1] // LANES
    dv = DIFF_V_DIM
    heads = LANES // dv
    aug = dv + ATT_AUG
    _stage_values(v_ref, vt_s, seq, dv)
    kr = lax.broadcasted_iota(jnp.int32, (tk, tq), 0)
    qc = lax.broadcasted_iota(jnp.int32, (tk, tq), 1)
    lv = lam_ref[...]
    lam_init = laminit_ref[...]
    lam = (jnp.exp(jnp.sum(lv[0:1, :] * lv[1:2, :], keepdims=True))
           - jnp.exp(jnp.sum(lv[2:3, :] * lv[3:4, :], keepdims=True)) + lam_init)
    nw = nw_ref[...]

    for qi in range(seq // tq):
        qrows = slice(qi * tq, (qi + 1) * tq)
        qstack = [_stack_masked(q_ref[qrows, g * LANES:(g + 1) * LANES].astype(F32), DIFF_QK_DIM)
                  for g in range(blocks)]
        state = [_softmax_init(dv + ATT_AUG, tq) for _ in range(2 * heads * blocks)]
        n_full = (qi * tq) // tk
        for kj in range(n_full + 1):
            krows = slice(kj * tk, (kj + 1) * tk)
            valid = None if kj < n_full else kr + (kj * tk - qi * tq) <= qc
            s = [_dot_nt(k_ref[krows, g * LANES:(g + 1) * LANES], qstack[g])
                 for g in range(blocks)]
            for g in range(blocks):
                for h in range(heads):
                    scores = [s[g][:, (2 * h + mp) * tq:(2 * h + mp + 1) * tq] for mp in range(2)]
                    if valid is not None:
                        scores = [jnp.where(valid, x, -jnp.inf) for x in scores]
                    first = 2 * (heads * g + h)
                    state[first:first + 2] = _softmax_tile(state[first:first + 2], scores, None,
                                                           vt_s[g, kj, h * aug:(h + 1) * aug, :])
        att = [acc[0:dv, :] / acc[dv:dv + 1, :] for _, acc in state]
        halves = []
        for h in range(heads * blocks):
            o = att[2 * h] - lam * att[2 * h + 1]
            halves.append(o * lax.rsqrt(jnp.mean(o * o, axis=0, keepdims=True) + EPS))
        o = jnp.concatenate(halves, axis=0).T
        o_ref[qrows, :] = (o * nw * (1.0 - lam_init)).astype(o_ref.dtype)


def _diff(bc, lam_vecs, lam_init, norm_row):
    b, t, _ = bc.shape
    base = 3 * DSW_W // DIFF_W
    spec = lambda j: pl.BlockSpec((None, t, DIFF_W), lambda i: (i, 0, base + j))
    return pl.pallas_call(
        functools.partial(_diff_kernel, seq=t),
        grid=(b,),
        in_specs=[spec(0), spec(1), spec(2),
                  pl.BlockSpec((4, DIFF_QK_DIM), lambda i: (0, 0)),
                  pl.BlockSpec((1, 1), lambda i: (0, 0)),
                  pl.BlockSpec((1, DIFF_W), lambda i: (0, 0))],
        out_specs=pl.BlockSpec((None, t, DIFF_W), lambda i: (i, 0, 0)),
        out_shape=jax.ShapeDtypeStruct((b, t, DIFF_W), BF16),
        scratch_shapes=[pltpu.VMEM((DIFF_W // LANES, t // ATT_K_TILE,
                                    LANES // DIFF_V_DIM * (DIFF_V_DIM + ATT_AUG), ATT_K_TILE), BF16)],
        compiler_params=_params("arbitrary"),
        name="diff",
    )(bc, bc, bc, lam_vecs, lam_init, norm_row)


def _layer_norm(h, g, b):
    mu = jnp.mean(h, axis=-1, keepdims=True)
    hc = h - mu
    var = jnp.mean(hc * hc, axis=-1, keepdims=True)
    return hc * lax.rsqrt(var + EPS) * g + b


def _residual_layer_norm(project, x_ref, g_ref, b_ref, o_ref, alpha):
    tm = x_ref.shape[0]
    ahead = project(slice(0, LN_ROWS))
    for r in range(0, tm, LN_ROWS):
        y = ahead
        if r + LN_ROWS < tm:
            ahead = project(slice(r + LN_ROWS, r + 2 * LN_ROWS))
        o_ref[r:r + LN_ROWS, :] = _layer_norm(alpha * x_ref[r:r + LN_ROWS, :] + y, g_ref[...], b_ref[...])


def _out_ln_kernel(x_ref, ya_ref, yb_ref, yc_ref, w_ref, g_ref, b_ref, o_ref, *, alpha):
    def project(rows):
        mixed = jnp.concatenate([ya_ref[rows, :], yb_ref[rows, :], yc_ref[rows, :]], axis=1)
        return jnp.dot(mixed, w_ref[...], preferred_element_type=F32)

    _residual_layer_norm(project, x_ref, g_ref, b_ref, o_ref, alpha)


def _out_ln(x, ya, yb, yc, w, g, b, layer, alpha):
    n, d = x.shape
    tm = min(LN_ROW_TILE, n)
    rows = lambda width: pl.BlockSpec((tm, width), lambda i: (i, 0))
    vec = pl.BlockSpec((None, 1, d), lambda i: (layer, 0, 0))
    return pl.pallas_call(
        functools.partial(_out_ln_kernel, alpha=alpha),
        grid=(n // tm,),
        in_specs=[rows(d), rows(GDN_W), rows(DSW_W), rows(DIFF_W),
                  pl.BlockSpec((None, d, d), lambda i: (layer, 0, 0)), vec, vec],
        out_specs=rows(d),
        out_shape=jax.ShapeDtypeStruct((n, d), F32),
        compiler_params=_params("arbitrary"),
        name="out_ln",
    )(x, ya, yb, yc, w, g, b)


def _ffn_up_kernel(x_ref, w_ref, cg_ref, cv_ref, h_ref, work_g, work_v, carry_g, carry_v,
                   *, tiles_per_seq, col_chunk):
    tm = x_ref.shape[0]
    width = h_ref.shape[1]

    @pl.when(pl.program_id(0) % tiles_per_seq == 0)
    def _():
        carry_g[...] = jnp.zeros(carry_g.shape, F32)
        carry_v[...] = jnp.zeros(carry_v.shape, F32)

    x = x_ref[...].astype(BF16)
    first = SUBLANES - (FFN_CONV - 1)

    def project(c, slot):
        cols = slice(c, c + col_chunk)
        for first_col, work, carry in ((0, work_g, carry_g), (width, work_v, carry_v)):
            work[slot, 0:SUBLANES, :] = carry[:, cols]
            work[slot, SUBLANES:SUBLANES + tm, :] = jnp.dot(
                x, w_ref[:, first_col + c:first_col + c + col_chunk], preferred_element_type=F32)
            carry[:, cols] = work[slot, tm:tm + SUBLANES, :]

    def conv(work, slot, taps, r):
        acc = work[slot, first + r:first + r + EPI_ROWS, :] * taps[0]
        for j in range(1, FFN_CONV):
            acc = acc + work[slot, first + j + r:first + j + r + EPI_ROWS, :] * taps[j]
        return acc

    project(0, 0)
    for n, c in enumerate(range(0, width, col_chunk)):
        slot = n % 2
        if c + col_chunk < width:
            project(c + col_chunk, 1 - slot)
        cols = slice(c, c + col_chunk)
        taps_g = [jnp.broadcast_to(0.5 * cg_ref[j:j + 1, cols], (EPI_ROWS, col_chunk)) for j in range(FFN_CONV)]
        taps_v = [jnp.broadcast_to(cv_ref[j:j + 1, cols], (EPI_ROWS, col_chunk)) for j in range(FFN_CONV)]
        for r in range(0, tm, EPI_ROWS):
            half = conv(work_g, slot, taps_g, r)
            val = conv(work_v, slot, taps_v, r)
            h_ref[r:r + EPI_ROWS, cols] = ((half + half * jnp.tanh(half)) * val).astype(h_ref.dtype)


def _ffn_up(x, w, cg, cv, layer, seq):
    n, d = x.shape
    width = w.shape[-1] // 2
    tm = min(ROW_TILE, seq)
    col_chunk = 256
    wspec = pl.BlockSpec((None, d, 2 * width), lambda i: (layer, 0, 0))
    cspec = pl.BlockSpec((None, FFN_CONV, width), lambda i: (layer, 0, 0))
    return pl.pallas_call(
        functools.partial(_ffn_up_kernel, tiles_per_seq=seq // tm, col_chunk=col_chunk),
        grid=(n // tm,),
        in_specs=[pl.BlockSpec((tm, d), lambda i: (i, 0)), wspec, cspec, cspec],
        out_specs=pl.BlockSpec((tm, width), lambda i: (i, 0)),
        out_shape=jax.ShapeDtypeStruct((n, width), BF16),
        scratch_shapes=[pltpu.VMEM((2, tm + SUBLANES, col_chunk), F32)] * 2
        + [pltpu.VMEM((SUBLANES, width), F32)] * 2,
        compiler_params=_params("arbitrary"),
        name="ffn_up",
    )(x, w, cg, cv)


def _ffn_down_kernel(x_ref, h_ref, w_ref, g_ref, b_ref, o_ref, *, alpha):
    project = lambda rows: jnp.dot(h_ref[rows, :], w_ref[...], preferred_element_type=F32)
    _residual_layer_norm(project, x_ref, g_ref, b_ref, o_ref, alpha)


def _ffn_down(x, h, w, g, b, layer, alpha):
    n, d = x.shape
    width = h.shape[1]
    tm = min(LN_ROW_TILE, n)
    vec = pl.BlockSpec((None, 1, d), lambda i: (layer, 0, 0))
    return pl.pallas_call(
        functools.partial(_ffn_down_kernel, alpha=alpha),
        grid=(n // tm,),
        in_specs=[pl.BlockSpec((tm, d), lambda i: (i, 0)), pl.BlockSpec((tm, width), lambda i: (i, 0)),
                  pl.BlockSpec((None, width, d), lambda i: (layer, 0, 0)), vec, vec],
        out_specs=pl.BlockSpec((tm, d), lambda i: (i, 0)),
        out_shape=jax.ShapeDtypeStruct((n, d), F32),
        compiler_params=_params("arbitrary"),
        name="ffn_down",
    )(x, h, w, g, b)


def _pad_last(a, width):
    return jnp.pad(a, [(0, 0)] * (a.ndim - 1) + [(0, width - a.shape[-1])])


def kernel(x, w_in, gdn_conv, gdn_a_log, gdn_dt_bias, gdn_norm, diff_lambda, diff_norm, w_out,
           ln1_g, ln1_b, w_up, ffn_conv, w_down, ln2_g, ln2_b):
    batch, seq, d = x.shape
    depth = w_in.shape[0]
    d_ff = w_down.shape[1]
    ff_pad = -(-d_ff // LANES) * LANES
    alpha = (2 * depth) ** 0.25

    bd0 = A_W
    bc0 = A_W + 2 * GDN_HEADS
    w_in_r = jnp.concatenate(
        [w_in[:, :, :A_W], w_in[:, :, bc0:], _pad_last(w_in[:, :, bd0:bc0], LANES)], axis=-1).astype(BF16)
    scale = jnp.concatenate([
        jnp.full((DSW_W,), DSW_HEAD_DIM ** -0.5 * LOG2E, F32), jnp.ones((2 * DSW_W,), F32),
        jnp.full((DIFF_QK_W,), DIFF_QK_DIM ** -0.5 * LOG2E, F32), jnp.ones((DIFF_QK_W + DIFF_W,), F32)])[None, :]
    lane_row = lambda v, off: jnp.pad(v, ((0, 0), (off, LANES - off - v.shape[1])))[:, None, :]
    alog_row = lane_row(gdn_a_log, GDN_HEADS)
    dtb_row = lane_row(gdn_dt_bias, GDN_HEADS)
    gdn_norm_row = gdn_norm[:, None, :]
    diff_norm_row = jnp.tile(diff_norm, (1, DIFF_HEADS))[:, None, :]
    lam_init = jnp.asarray([0.8 - 0.6 * math.exp(-0.3 * l) for l in range(depth)], F32)[:, None, None]
    w_out_b = w_out.astype(BF16)
    w_up_r = jnp.concatenate(
        [_pad_last(w_up[:, :, :d_ff], ff_pad), _pad_last(w_up[:, :, d_ff:], ff_pad)], axis=-1).astype(BF16)
    cg = _pad_last(ffn_conv[:, :, :d_ff], ff_pad)
    cv = _pad_last(ffn_conv[:, :, d_ff:], ff_pad)
    w_down_b = jnp.pad(w_down, ((0, 0), (0, ff_pad - d_ff), (0, 0))).astype(BF16)
    vec3 = lambda v: v[:, None, :]

    xf = x.reshape(batch * seq, d)
    for l in range(depth):
        a, bc, bd = _in_proj(xf, w_in_r, scale, gdn_conv, l, seq)
        bc3 = bc.reshape(batch, seq, BC_W)
        ya = _gdn(a.reshape(batch, seq, A_W), bd.reshape(batch, seq, LANES),
                  alog_row, dtb_row, gdn_norm_row, l)
        yb = _dsw(bc3)
        yc = _diff(bc3, diff_lambda[l], lam_init[l], diff_norm_row[l])
        n = batch * seq
        x1 = _out_ln(xf, ya.reshape(n, GDN_W), yb.reshape(n, DSW_W), yc.reshape(n, DIFF_W),
                     w_out_b, vec3(ln1_g), vec3(ln1_b), l, alpha)
        h = _ffn_up(x1, w_up_r, cg, cv, l, seq)
        xf = _ffn_down(x1, h, w_down_b, vec3(ln2_g), vec3(ln2_b), l, alpha)
    return xf.reshape(batch, seq, d)
```

```python
import functools
import math

import jax
import jax.numpy as jnp
from jax import lax
from jax.experimental import pallas as pl
from jax.experimental.pallas import tpu as pltpu

F32 = jnp.float32
BF16 = jnp.bfloat16

LANES = 128
SUBLANES = 8
VMEM_LIMIT = 48 * 1024 * 1024

GDN_HEADS = 4
GDN_HEAD_DIM = 128
GDN_CONV = 4
DSW_HEADS = 4
DSW_HEAD_DIM = 64
DIFF_HEADS = 4
DIFF_QK_DIM = 32
DIFF_V_DIM = 64
FFN_CONV = 3
EPS = 1e-5

GDN_W = GDN_HEADS * GDN_HEAD_DIM
DSW_W = DSW_HEADS * DSW_HEAD_DIM
DIFF_W = DIFF_HEADS * DIFF_V_DIM
DIFF_QK_W = DIFF_HEADS * 2 * DIFF_QK_DIM
A_W = 4 * GDN_W
BC_W = 3 * DSW_W + 2 * DIFF_QK_W + DIFF_W

ROW_TILE = 512
LN_ROW_TILE = 1024
LN_ROWS = 128
PROJ_CHUNK = 256
EPI_ROWS = 64
ATT_Q_TILE = 512
ATT_K_TILE = 512
ATT_AUG = 16
LOG2E = math.log2(math.e)
GDN_CHUNK = 128
GDN_GROUP = 2
GDN_SPLIT_SPAN = 16
DSW_MAX_BAND = 512
DSW_FAR_TILE = -(-(DSW_MAX_BAND + ATT_K_TILE) // ATT_Q_TILE)


def _dot(a, b):
    return jnp.dot(a.astype(BF16), b.astype(BF16), preferred_element_type=F32)


def _dot_nt(a, b):
    return lax.dot_general(a.astype(BF16), b.astype(BF16), (((1,), (1,)), ((), ())),
                           preferred_element_type=F32)


def _split_bf16(a):
    hi = a.astype(BF16)
    return hi, (a - hi.astype(F32)).astype(BF16)


def _split3_bf16(a):
    hi, mid = _split_bf16(a)
    return hi, mid, (a - hi.astype(F32) - mid.astype(F32)).astype(BF16)


def _dot_split(a, b):
    ah, al = _split_bf16(a)
    bh, bl = _split_bf16(b)
    return jnp.dot(jnp.concatenate([ah, al, ah], axis=1), jnp.concatenate([bh, bh, bl], axis=0),
                   preferred_element_type=F32)


def _sigmoid(x):
    return 1.0 / (1.0 + jnp.exp(-x))


def _params(*sem):
    return pltpu.CompilerParams(dimension_semantics=sem, vmem_limit_bytes=VMEM_LIMIT)


def _in_proj_kernel(x_ref, w_ref, scale_ref, cw_ref, a_ref, bc_ref, bd_ref,
                    work, carry, *, tiles_per_seq):
    tm = x_ref.shape[0]
    x = x_ref[...].astype(BF16)

    @pl.when(pl.program_id(0) % tiles_per_seq == 0)
    def _():
        carry[...] = jnp.zeros(carry.shape, F32)

    first = SUBLANES - (GDN_CONV - 1)

    def project_conv(cols, slot):
        work[slot, 0:SUBLANES, :] = carry[:, cols]
        work[slot, SUBLANES:SUBLANES + tm, :] = jnp.dot(x, w_ref[:, cols], preferred_element_type=F32)
        carry[:, cols] = work[slot, tm:tm + SUBLANES, :]

    def finish_conv(cols, slot, norm_scale):
        taps = [jnp.broadcast_to(0.5 * cw_ref[j:j + 1, cols], (EPI_ROWS, PROJ_CHUNK)) for j in range(GDN_CONV)]
        for r in range(0, tm, EPI_ROWS):
            y = work[slot, first + r:first + r + EPI_ROWS, :] * taps[0]
            for j in range(1, GDN_CONV):
                y = y + work[slot, first + j + r:first + j + r + EPI_ROWS, :] * taps[j]
            y = y + y * jnp.tanh(y)
            if norm_scale is not None:
                heads = [y[:, h * GDN_HEAD_DIM:(h + 1) * GDN_HEAD_DIM] for h in range(PROJ_CHUNK // GDN_HEAD_DIM)]
                y = jnp.concatenate(
                    [v * (lax.rsqrt(jnp.sum(v * v, axis=-1, keepdims=True) + 1e-6) * norm_scale)
                     for v in heads], axis=1)
            a_ref[r:r + EPI_ROWS, cols] = y.astype(a_ref.dtype)

    def plain(first_col, cols, store):
        wcols = slice(first_col + cols.start, first_col + cols.stop)
        return lambda: store(jnp.dot(x, w_ref[:, wcols], preferred_element_type=F32))

    def store_z(cols):
        def store(u):
            a_ref[:, cols] = u.astype(a_ref.dtype)
        return store

    def store_bc(cols):
        def store(u):
            bc_ref[:, cols] = (u * scale_ref[:, cols]).astype(BF16)
        return store

    def store_bd(u):
        bd_ref[...] = u

    chunks = lambda base, width: [slice(base + c, base + c + PROJ_CHUNK) for c in range(0, width, PROJ_CHUNK)]
    heavy = ([(c, GDN_HEAD_DIM ** -0.5) for c in chunks(0, GDN_W)] + [(c, 1.0) for c in chunks(GDN_W, GDN_W)]
             + [(c, None) for c in chunks(2 * GDN_W, GDN_W)])
    light = ([plain(A_W, c, store_bc(c)) for c in chunks(0, BC_W)]
             + [plain(0, c, store_z(c)) for c in chunks(3 * GDN_W, GDN_W)]
             + [plain(A_W + BC_W, slice(0, LANES), store_bd)])
    project_conv(heavy[0][0], 0)
    for n, (cols, norm_scale) in enumerate(heavy):
        if n + 1 < len(heavy):
            project_conv(heavy[n + 1][0], (n + 1) % 2)
        light[n]()
        finish_conv(cols, n % 2, norm_scale)
    for job in light[len(heavy):]:
        job()


def _in_proj(x, w, scale, conv_w, layer, seq):
    n, d = x.shape
    tm = min(ROW_TILE, seq)
    return pl.pallas_call(
        functools.partial(_in_proj_kernel, tiles_per_seq=seq // tm),
        grid=(n // tm,),
        in_specs=[
            pl.BlockSpec((tm, d), lambda i: (i, 0)),
            pl.BlockSpec((None, d, A_W + BC_W + LANES), lambda i: (layer, 0, 0)),
            pl.BlockSpec((1, BC_W), lambda i: (0, 0)),
            pl.BlockSpec((None, GDN_CONV, 3 * GDN_W), lambda i: (layer, 0, 0)),
        ],
        out_specs=[
            pl.BlockSpec((tm, A_W), lambda i: (i, 0)),
            pl.BlockSpec((tm, BC_W), lambda i: (i, 0)),
            pl.BlockSpec((tm, LANES), lambda i: (i, 0)),
        ],
        out_shape=[
            jax.ShapeDtypeStruct((n, A_W), BF16),
            jax.ShapeDtypeStruct((n, BC_W), BF16),
            jax.ShapeDtypeStruct((n, LANES), F32),
        ],
        scratch_shapes=[pltpu.VMEM((2, tm + SUBLANES, PROJ_CHUNK), F32), pltpu.VMEM((SUBLANES, 3 * GDN_W), F32)],
        compiler_params=_params("arbitrary"),
        name="in_proj",
    )(x, w, scale, conv_w)


def _gdn_kernel(q_ref, k_ref, v_ref, z_ref, bd_ref, alog_ref, dtb_ref, nw_ref, y_ref,
                beta_s, g_s, sa_s, sb_s, oq_s, oc_s, gl_s, *, seq):
    C = GDN_CHUNK
    n_chunks = seq // C
    n_heads = GDN_HEADS
    lanes_of = lambda h: slice(h * LANES, (h + 1) * LANES)

    bd = bd_ref[...]
    beta_all = _sigmoid(bd)
    xg = bd + dtb_ref[...]
    softplus = jnp.maximum(xg, 0.0) + jnp.log(1.0 + jnp.exp(-jnp.abs(xg)))
    g_all = -jnp.exp(alog_ref[...]) * softplus
    for h in range(n_heads):
        beta_s[h] = jnp.broadcast_to(beta_all[:, h:h + 1], (seq, LANES))
        g_s[h] = jnp.broadcast_to(g_all[:, h + GDN_HEADS:h + GDN_HEADS + 1], (seq, LANES))

    ri = lax.broadcasted_iota(jnp.int32, (C, C), 0)
    ci = lax.broadcasted_iota(jnp.int32, (C, C), 1)
    lower_incl = ri >= ci
    strict = ri > ci
    tri = jnp.where(lower_incl, 1.0, 0.0).astype(BF16)
    tri3 = jnp.concatenate([tri, tri, tri], axis=1)

    def chunk_group(gi, carry):
        items = [(h, gi * GDN_GROUP + j) for j in range(GDN_GROUP) for h in range(n_heads)]
        rows = [pl.ds(pl.multiple_of(c * C, C), C) for _, c in items]
        qc = [q_ref[r, lanes_of(h)].astype(F32) for (h, _), r in zip(items, rows)]
        kc = [k_ref[r, lanes_of(h)].astype(F32) for (h, _), r in zip(items, rows)]
        vc = [v_ref[r, lanes_of(h)].astype(F32) for (h, _), r in zip(items, rows)]
        bb = [beta_s[h, r, :] for (h, _), r in zip(items, rows)]
        gg = [g_s[h, r, :] for (h, _), r in zip(items, rows)]
        cum = [jnp.dot(tri3, jnp.concatenate(_split3_bf16(g), axis=0), preferred_element_type=F32)
               for g in gg]
        decay = [jnp.exp(jnp.where(lower_incl, c - c.T, -jnp.inf)) for c in cum]
        kb = [k * b for k, b in zip(kc, bb)]
        scores = [_dot_nt(jnp.concatenate([x, q], axis=0), k) for x, q, k in zip(kb, qc, kc)]
        m = [jnp.where(strict, s[0:C, :] * d, 0.0) for s, d in zip(scores, decay)]
        qk = [jnp.where(lower_incl, s[C:, :] * d, 0.0) for s, d in zip(scores, decay)]
        nmat = [-x for x in m]
        p = [_dot_split(x, x) for x in m]
        span = 2
        while 2 * span < C:
            dot = _dot_split if span < GDN_SPLIT_SPAN else _dot
            both = [dot(jnp.concatenate([n, x], axis=0), x) for n, x in zip(nmat, p)]
            nmat = [n + x + b[0:C, :] for n, x, b in zip(nmat, p, both)]
            p = [b[C:, :] for b in both]
            span *= 2
        nmat = [n + x + _dot(n, x) for n, x in zip(nmat, p)]
        ecum = [jnp.exp(c) for c in cum]
        vb = [v * b for v, b in zip(vc, bb)]
        kbe = [x * e for x, e in zip(kb, ecum)]
        wu = [jnp.concatenate([x, y], axis=1) for x, y in zip(kbe, vb)]
        wu = [x + _dot(n, x) for n, x in zip(nmat, wu)]
        qd = [q * e for q, e in zip(qc, ecum)]
        cum_last = [c[C - 1:C, :] for c in cum]
        ktt = [(k * jnp.exp(cl - c)).T for k, cl, c in zip(kc, cum_last, cum)]
        prod = [_dot(jnp.concatenate([kt, a], axis=0), x) for kt, a, x in zip(ktt, qk, wu)]
        for j, ((h, c), r) in enumerate(zip(items, rows)):
            sa_s[h, r, :] = (-prod[j][0:C, 0:C]).astype(sa_s.dtype)
            sb_s[h, r, :] = prod[j][0:C, C:]
            oq_s[h, r, :] = (qd[j] - prod[j][C:, 0:C]).astype(oq_s.dtype)
            oc_s[h, r, :] = prod[j][C:, C:]
            gl_s[h, pl.ds(pl.multiple_of(c * SUBLANES, SUBLANES), SUBLANES), :] = (
                jnp.broadcast_to(jnp.exp(cum_last[j]), (SUBLANES, LANES)))
        return carry

    lax.fori_loop(0, n_chunks // GDN_GROUP, chunk_group, 0)

    nw = nw_ref[...]

    def scan_step(c, states):
        rows = pl.ds(pl.multiple_of(c * C, C), C)
        new = []
        for h, s in enumerate(states):
            gl = gl_s[h, pl.ds(pl.multiple_of(c * SUBLANES, SUBLANES), 1), :]
            new.append(s * gl + (_dot(sa_s[h, rows, :], s) + sb_s[h, rows, :]))
        for h, s in enumerate(states):
            o = _dot(oq_s[h, rows, :], s) + oc_s[h, rows, :]
            z = z_ref[rows, lanes_of(h)].astype(F32)
            o = o * lax.rsqrt(jnp.mean(o * o, axis=-1, keepdims=True) + EPS) * nw
            y_ref[rows, lanes_of(h)] = (o * (z * _sigmoid(z))).astype(y_ref.dtype)
        return tuple(new)

    lax.fori_loop(0, n_chunks, scan_step,
                  tuple(jnp.zeros((GDN_HEAD_DIM, GDN_HEAD_DIM), F32) for _ in range(n_heads)))


def _gdn(a, bd, alog_row, dtb_row, norm_row, layer):
    b, t, _ = a.shape
    tok = lambda j: pl.BlockSpec((None, t, GDN_W), lambda i: (i, 0, j))
    row = pl.BlockSpec((None, 1, LANES), lambda i: (layer, 0, 0))
    head_buf = lambda dtype: pltpu.VMEM((GDN_HEADS, t, LANES), dtype)
    return pl.pallas_call(
        functools.partial(_gdn_kernel, seq=t),
        grid=(b,),
        in_specs=[tok(0), tok(1), tok(2), tok(3),
                  pl.BlockSpec((None, t, LANES), lambda i: (i, 0, 0)), row, row, row],
        out_specs=pl.BlockSpec((None, t, GDN_W), lambda i: (i, 0, 0)),
        out_shape=jax.ShapeDtypeStruct((b, t, GDN_W), BF16),
        scratch_shapes=[head_buf(F32), head_buf(F32), head_buf(BF16), head_buf(F32), head_buf(BF16),
                        head_buf(F32)]
        + [pltpu.VMEM((GDN_HEADS, t // GDN_CHUNK * SUBLANES, LANES), F32)],
        compiler_params=_params("arbitrary"),
        name="gdn",
    )(a, a, a, a, bd, alog_row, dtb_row, norm_row)


def _stage_values(v_ref, vt_s, seq, dv):
    aug = dv + ATT_AUG
    ones_row = jnp.where(lax.broadcasted_iota(jnp.int32, (ATT_AUG, ATT_K_TILE), 0) == 0, 1.0, 0.0).astype(BF16)
    for g in range(v_ref.shape[1] // LANES):
        for t in range(seq // ATT_K_TILE):
            for h in range(LANES // dv):
                vt_s[g, t, h * aug + dv:(h + 1) * aug, :] = ones_row
        for c in range(seq // LANES):
            t, off = divmod(c * LANES, ATT_K_TILE)
            blk = v_ref[c * LANES:(c + 1) * LANES, g * LANES:(g + 1) * LANES].astype(F32).T.astype(BF16)
            for h in range(LANES // dv):
                vt_s[g, t, h * aug:h * aug + dv, off:off + LANES] = blk[h * dv:(h + 1) * dv, :]


def _stack_masked(q, width):
    lane = lax.broadcasted_iota(jnp.int32, q.shape, 1)
    return jnp.concatenate(
        [jnp.where((lane >= j * width) & (lane < (j + 1) * width), q, 0.0).astype(BF16)
         for j in range(LANES // width)], axis=0)


def _softmax_init(rows, width):
    return (jnp.full((1, width), -jnp.inf, F32), jnp.zeros((rows, width), F32))


def _softmax_tile(state, scores, cnt, vt):
    ps, alphas, maxes = [], [], []
    for (m_old, _), s in zip(state, scores):
        m_new = jnp.maximum(m_old, jnp.max(s, axis=0, keepdims=True))
        p = jnp.exp2(s - m_new)
        if cnt is not None:
            p = p * cnt
        alphas.append(jnp.exp2(m_old - m_new))
        maxes.append(m_new)
        ps.append(p.astype(BF16))
    pv = jnp.dot(vt, jnp.concatenate(ps, axis=1), preferred_element_type=F32)
    width = pv.shape[1] // len(state)
    return [(m, a * acc + pv[:, n * width:(n + 1) * width])
            for n, (m, a, (_, acc)) in enumerate(zip(maxes, alphas, state))]


def _dsw_kernel(q_ref, k_ref, v_ref, o_ref, vt_s, cnt_s, bias_s, *, seq):
    tq, tk = ATT_Q_TILE, ATT_K_TILE
    blocks = q_ref.shape[1] // LANES
    dv = DSW_HEAD_DIM
    heads = LANES // dv
    aug = dv + ATT_AUG
    _stage_values(v_ref, vt_s, seq, dv)

    @pl.when(pl.program_id(0) == 0)
    def _():
        kr = lax.broadcasted_iota(jnp.int32, (tk, tq), 0)
        qc = lax.broadcasted_iota(jnp.int32, (tk, tq), 1)
        for d in range(DSW_FAR_TILE + 1):
            delta = d * tq + qc - kr
            causal = delta >= 0
            cnt = (jnp.where(causal & (delta <= 128), 1.0, 0.0)
                   + jnp.where(causal & (delta <= DSW_MAX_BAND) & ((delta & 3) == 0), 1.0, 0.0)
                   + jnp.where(causal & ((delta & 15) == 0), 1.0, 0.0))
            cnt_s[d] = cnt.astype(F32)
            bias_s[d] = jnp.where(cnt > 0.0, 0.0, -jnp.inf).astype(F32)

    for qi in range(seq // tq):
        qrows = slice(qi * tq, (qi + 1) * tq)
        qstack = [_stack_masked(q_ref[qrows, g * LANES:(g + 1) * LANES].astype(F32), DSW_HEAD_DIM)
                  for g in range(blocks)]
        state = [_softmax_init(aug, tq) for _ in range(blocks * heads)]
        for kj in range((qi * tq) // tk + 1):
            krows = slice(kj * tk, (kj + 1) * tk)
            far = min(qi - kj * (tk // tq), DSW_FAR_TILE)
            cnt = cnt_s[far] if far < DSW_FAR_TILE else None
            bias = bias_s[far]
            s = [_dot_nt(k_ref[krows, g * LANES:(g + 1) * LANES], qstack[g]) for g in range(blocks)]
            for g in range(blocks):
                for h in range(heads):
                    i = g * heads + h
                    state[i:i + 1] = _softmax_tile(state[i:i + 1], [s[g][:, h * tq:(h + 1) * tq] + bias],
                                                   cnt, vt_s[g, kj, h * aug:(h + 1) * aug, :])
        o = jnp.concatenate([acc[0:dv, :] / acc[dv:dv + 1, :] for _, acc in state], axis=0)
        o_ref[qrows, :] = o.T.astype(o_ref.dtype)


def _dsw(bc):
    b, t, _ = bc.shape
    spec = lambda j: pl.BlockSpec((None, t, DSW_W), lambda i: (i, 0, j))
    return pl.pallas_call(
        functools.partial(_dsw_kernel, seq=t),
        grid=(b,),
        in_specs=[spec(0), spec(1), spec(2)],
        out_specs=pl.BlockSpec((None, t, DSW_W), lambda i: (i, 0, 0)),
        out_shape=jax.ShapeDtypeStruct((b, t, DSW_W), BF16),
        scratch_shapes=[pltpu.VMEM((DSW_W // LANES, t // ATT_K_TILE,
                                    LANES // DSW_HEAD_DIM * (DSW_HEAD_DIM + ATT_AUG), ATT_K_TILE), BF16)]
        + [pltpu.VMEM((DSW_FAR_TILE + 1, ATT_K_TILE, ATT_Q_TILE), F32)] * 2,
        compiler_params=_params("arbitrary"),
        name="dsw",
    )(bc, bc, bc)


def _diff_kernel(q_ref, k_ref, v_ref, lam_ref, laminit_ref, nw_ref, o_ref, vt_s, *, seq):
    tq, tk = ATT_Q_TILE, ATT_K_TILE
    blocks = q_ref.shape[1] // LANES
    dv = DIFF_V_DIM
    heads = LANES // dv
    aug = dv + ATT_AUG
    _stage_values(v_ref, vt_s, seq, dv)
    kr = lax.broadcasted_iota(jnp.int32, (tk, tq), 0)
    qc = lax.broadcasted_iota(jnp.int32, (tk, tq), 1)
    lv = lam_ref[...]
    lam_init = laminit_ref[...]
    lam = (jnp.exp(jnp.sum(lv[0:1, :] * lv[1:2, :], keepdims=True))
           - jnp.exp(jnp.sum(lv[2:3, :] * lv[3:4, :], keepdims=True)) + lam_init)
    nw = nw_ref[...]

    for qi in range(seq // tq):
        qrows = slice(qi * tq, (qi + 1) * tq)
        qstack = [_stack_masked(q_ref[qrows, g * LANES:(g + 1) * LANES].astype(F32), DIFF_QK_DIM)
                  for g in range(blocks)]
        state = [_softmax_init(dv + ATT_AUG, tq) for _ in range(2 * heads * blocks)]
        n_full = (qi * tq) // tk
        for kj in range(n_full + 1):
            krows = slice(kj * tk, (kj + 1) * tk)
            valid = None if kj < n_full else kr + (kj * tk - qi * tq) <= qc
            s = [_dot_nt(k_ref[krows, g * LANES:(g + 1) * LANES], qstack[g])
                 for g in range(blocks)]
            for g in range(blocks):
                for h in range(heads):
                    scores = [s[g][:, (2 * h + mp) * tq:(2 * h + mp + 1) * tq] for mp in range(2)]
                    if valid is not None:
                        scores = [jnp.where(valid, x, -jnp.inf) for x in scores]
                    first = 2 * (heads * g + h)
                    state[first:first + 2] = _softmax_tile(state[first:first + 2], scores, None,
                                                           vt_s[g, kj, h * aug:(h + 1) * aug, :])
        att = [acc[0:dv, :] / acc[dv:dv + 1, :] for _, acc in state]
        halves = []
        for h in range(heads * blocks):
            o = att[2 * h] - lam * att[2 * h + 1]
            halves.append(o * lax.rsqrt(jnp.mean(o * o, axis=0, keepdims=True) + EPS))
        o = jnp.concatenate(halves, axis=0).T
        o_ref[qrows, :] = (o * nw * (1.0 - lam_init)).astype(o_ref.dtype)


def _diff(bc, lam_vecs, lam_init, norm_row):
    b, t, _ = bc.shape
    base = 3 * DSW_W // DIFF_W
    spec = lambda j: pl.BlockSpec((None, t, DIFF_W), lambda i: (i, 0, base + j))
    return pl.pallas_call(
        functools.partial(_diff_kernel, seq=t),
        grid=(b,),
        in_specs=[spec(0), spec(1), spec(2),
                  pl.BlockSpec((4, DIFF_QK_DIM), lambda i: (0, 0)),
                  pl.BlockSpec((1, 1), lambda i: (0, 0)),
                  pl.BlockSpec((1, DIFF_W), lambda i: (0, 0))],
        out_specs=pl.BlockSpec((None, t, DIFF_W), lambda i: (i, 0, 0)),
        out_shape=jax.ShapeDtypeStruct((b, t, DIFF_W), BF16),
        scratch_shapes=[pltpu.VMEM((DIFF_W // LANES, t // ATT_K_TILE,
                                    LANES // DIFF_V_DIM * (DIFF_V_DIM + ATT_AUG), ATT_K_TILE), BF16)],
        compiler_params=_params("arbitrary"),
        name="diff",
    )(bc, bc, bc, lam_vecs, lam_init, norm_row)


def _layer_norm(h, g, b):
    mu = jnp.mean(h, axis=-1, keepdims=True)
    hc = h - mu
    var = jnp.mean(hc * hc, axis=-1, keepdims=True)
    return hc * lax.rsqrt(var + EPS) * g + b


def _residual_layer_norm(project, x_ref, g_ref, b_ref, o_ref, alpha):
    tm = x_ref.shape[0]
    ahead = project(slice(0, LN_ROWS))
    for r in range(0, tm, LN_ROWS):
        y = ahead
        if r + LN_ROWS < tm:
            ahead = project(slice(r + LN_ROWS, r + 2 * LN_ROWS))
        o_ref[r:r + LN_ROWS, :] = _layer_norm(alpha * x_ref[r:r + LN_ROWS, :] + y, g_ref[...], b_ref[...])


def _out_ln_kernel(x_ref, ya_ref, yb_ref, yc_ref, w_ref, g_ref, b_ref, o_ref, *, alpha):
    def project(rows):
        mixed = jnp.concatenate([ya_ref[rows, :], yb_ref[rows, :], yc_ref[rows, :]], axis=1)
        return jnp.dot(mixed, w_ref[...], preferred_element_type=F32)

    _residual_layer_norm(project, x_ref, g_ref, b_ref, o_ref, alpha)


def _out_ln(x, ya, yb, yc, w, g, b, layer, alpha):
    n, d = x.shape
    tm = min(LN_ROW_TILE, n)
    rows = lambda width: pl.BlockSpec((tm, width), lambda i: (i, 0))
    vec = pl.BlockSpec((None, 1, d), lambda i: (layer, 0, 0))
    return pl.pallas_call(
        functools.partial(_out_ln_kernel, alpha=alpha),
        grid=(n // tm,),
        in_specs=[rows(d), rows(GDN_W), rows(DSW_W), rows(DIFF_W),
                  pl.BlockSpec((None, d, d), lambda i: (layer, 0, 0)), vec, vec],
        out_specs=rows(d),
        out_shape=jax.ShapeDtypeStruct((n, d), F32),
        compiler_params=_params("arbitrary"),
        name="out_ln",
    )(x, ya, yb, yc, w, g, b)


def _ffn_up_kernel(x_ref, w_ref, cg_ref, cv_ref, h_ref, work_g, work_v, carry_g, carry_v,
                   *, tiles_per_seq, col_chunk):
    tm = x_ref.shape[0]
    width = h_ref.shape[1]

    @pl.when(pl.program_id(0) % tiles_per_seq == 0)
    def _():
        carry_g[...] = jnp.zeros(carry_g.shape, F32)
        carry_v[...] = jnp.zeros(carry_v.shape, F32)

    x = x_ref[...].astype(BF16)
    first = SUBLANES - (FFN_CONV - 1)

    def project(c, slot):
        cols = slice(c, c + col_chunk)
        for first_col, work, carry in ((0, work_g, carry_g), (width, work_v, carry_v)):
            work[slot, 0:SUBLANES, :] = carry[:, cols]
            work[slot, SUBLANES:SUBLANES + tm, :] = jnp.dot(
                x, w_ref[:, first_col + c:first_col + c + col_chunk], preferred_element_type=F32)
            carry[:, cols] = work[slot, tm:tm + SUBLANES, :]

    def conv(work, slot, taps, r):
        acc = work[slot, first + r:first + r + EPI_ROWS, :] * taps[0]
        for j in range(1, FFN_CONV):
            acc = acc + work[slot, first + j + r:first + j + r + EPI_ROWS, :] * taps[j]
        return acc

    project(0, 0)
    for n, c in enumerate(range(0, width, col_chunk)):
        slot = n % 2
        if c + col_chunk < width:
            project(c + col_chunk, 1 - slot)
        cols = slice(c, c + col_chunk)
        taps_g = [jnp.broadcast_to(0.5 * cg_ref[j:j + 1, cols], (EPI_ROWS, col_chunk)) for j in range(FFN_CONV)]
        taps_v = [jnp.broadcast_to(cv_ref[j:j + 1, cols], (EPI_ROWS, col_chunk)) for j in range(FFN_CONV)]
        for r in range(0, tm, EPI_ROWS):
            half = conv(work_g, slot, taps_g, r)
            val = conv(work_v, slot, taps_v, r)
            h_ref[r:r + EPI_ROWS, cols] = ((half + half * jnp.tanh(half)) * val).astype(h_ref.dtype)


def _ffn_up(x, w, cg, cv, layer, seq):
    n, d = x.shape
    width = w.shape[-1] // 2
    tm = min(ROW_TILE, seq)
    col_chunk = 256
    wspec = pl.BlockSpec((None, d, 2 * width), lambda i: (layer, 0, 0))
    cspec = pl.BlockSpec((None, FFN_CONV, width), lambda i: (layer, 0, 0))
    return pl.pallas_call(
        functools.partial(_ffn_up_kernel, tiles_per_seq=seq // tm, col_chunk=col_chunk),
        grid=(n // tm,),
        in_specs=[pl.BlockSpec((tm, d), lambda i: (i, 0)), wspec, cspec, cspec],
        out_specs=pl.BlockSpec((tm, width), lambda i: (i, 0)),
        out_shape=jax.ShapeDtypeStruct((n, width), BF16),
        scratch_shapes=[pltpu.VMEM((2, tm + SUBLANES, col_chunk), F32)] * 2
        + [pltpu.VMEM((SUBLANES, width), F32)] * 2,
        compiler_params=_params("arbitrary"),
        name="ffn_up",
    )(x, w, cg, cv)


def _ffn_down_kernel(x_ref, h_ref, w_ref, g_ref, b_ref, o_ref, *, alpha):
    project = lambda rows: jnp.dot(h_ref[rows, :], w_ref[...], preferred_element_type=F32)
    _residual_layer_norm(project, x_ref, g_ref, b_ref, o_ref, alpha)


def _ffn_down(x, h, w, g, b, layer, alpha):
    n, d = x.shape
    width = h.shape[1]
    tm = min(LN_ROW_TILE, n)
    vec = pl.BlockSpec((None, 1, d), lambda i: (layer, 0, 0))
    return pl.pallas_call(
        functools.partial(_ffn_down_kernel, alpha=alpha),
        grid=(n // tm,),
        in_specs=[pl.BlockSpec((tm, d), lambda i: (i, 0)), pl.BlockSpec((tm, width), lambda i: (i, 0)),
                  pl.BlockSpec((None, width, d), lambda i: (layer, 0, 0)), vec, vec],
        out_specs=pl.BlockSpec((tm, d), lambda i: (i, 0)),
        out_shape=jax.ShapeDtypeStruct((n, d), F32),
        compiler_params=_params("arbitrary"),
        name="ffn_down",
    )(x, h, w, g, b)


def _pad_last(a, width):
    return jnp.pad(a, [(0, 0)] * (a.ndim - 1) + [(0, width - a.shape[-1])])


def kernel(x, w_in, gdn_conv, gdn_a_log, gdn_dt_bias, gdn_norm, diff_lambda, diff_norm, w_out,
           ln1_g, ln1_b, w_up, ffn_conv, w_down, ln2_g, ln2_b):
    batch, seq, d = x.shape
    depth = w_in.shape[0]
    d_ff = w_down.shape[1]
    ff_pad = -(-d_ff // LANES) * LANES
    alpha = (2 * depth) ** 0.25
    assert w_in.shape[2] == A_W + 2 * GDN_HEADS + BC_W and w_out.shape[1] == GDN_W + DSW_W + DIFF_W
    assert seq % ROW_TILE == 0 and (batch * seq) % LN_ROW_TILE == 0
    assert seq % ATT_Q_TILE == 0 and seq % ATT_K_TILE == 0 and seq % (GDN_GROUP * GDN_CHUNK) == 0
    assert seq <= 2048

    bd0 = A_W
    bc0 = A_W + 2 * GDN_HEADS
    w_in_r = jnp.concatenate(
        [w_in[:, :, :A_W], w_in[:, :, bc0:], _pad_last(w_in[:, :, bd0:bc0], LANES)], axis=-1).astype(BF16)
    scale = jnp.concatenate([
        jnp.full((DSW_W,), DSW_HEAD_DIM ** -0.5 * LOG2E, F32), jnp.ones((2 * DSW_W,), F32),
        jnp.full((DIFF_QK_W,), DIFF_QK_DIM ** -0.5 * LOG2E, F32), jnp.ones((DIFF_QK_W + DIFF_W,), F32)])[None, :]
    lane_row = lambda v, off: jnp.pad(v, ((0, 0), (off, LANES - off - v.shape[1])))[:, None, :]
    alog_row = lane_row(gdn_a_log, GDN_HEADS)
    dtb_row = lane_row(gdn_dt_bias, GDN_HEADS)
    gdn_norm_row = gdn_norm[:, None, :]
    diff_norm_row = jnp.tile(diff_norm, (1, DIFF_HEADS))[:, None, :]
    lam_init = jnp.asarray([0.8 - 0.6 * math.exp(-0.3 * l) for l in range(depth)], F32)[:, None, None]
    w_out_b = w_out.astype(BF16)
    w_up_r = jnp.concatenate(
        [_pad_last(w_up[:, :, :d_ff], ff_pad), _pad_last(w_up[:, :, d_ff:], ff_pad)], axis=-1).astype(BF16)
    cg = _pad_last(ffn_conv[:, :, :d_ff], ff_pad)
    cv = _pad_last(ffn_conv[:, :, d_ff:], ff_pad)
    w_down_b = jnp.pad(w_down, ((0, 0), (0, ff_pad - d_ff), (0, 0))).astype(BF16)
    vec3 = lambda v: v[:, None, :]

    xf = x.reshape(batch * seq, d)
    for l in range(depth):
        a, bc, bd = _in_proj(xf, w_in_r, scale, gdn_conv, l, seq)
        bc3 = bc.reshape(batch, seq, BC_W)
        ya = _gdn(a.reshape(batch, seq, A_W), bd.reshape(batch, seq, LANES),
                  alog_row, dtb_row, gdn_norm_row, l)
        yb = _dsw(bc3)
        yc = _diff(bc3, diff_lambda[l], lam_init[l], diff_norm_row[l])
        n = batch * seq
        x1 = _out_ln(xf, ya.reshape(n, GDN_W), yb.reshape(n, DSW_W), yc.reshape(n, DIFF_W),
                     w_out_b, vec3(ln1_g), vec3(ln1_b), l, alpha)
        h = _ffn_up(x1, w_up_r, cg, cv, l, seq)
        xf = _ffn_down(x1, h, w_down_b, vec3(ln2_g), vec3(ln2_b), l, alpha)
    return xf.reshape(batch, seq, d)
```

```python
import functools
import math

import jax
import jax.numpy as jnp
from jax import lax
from jax.experimental import pallas as pl
from jax.experimental.pallas import tpu as pltpu

F32 = jnp.float32
BF16 = jnp.bfloat16

LANES = 128
SUBLANES = 8
VMEM_LIMIT = 48 * 1024 * 1024

GDN_HEADS = 4
GDN_HEAD_DIM = 128
GDN_CONV = 4
DSW_HEADS = 4
DSW_HEAD_DIM = 64
DIFF_HEADS = 4
DIFF_QK_DIM = 32
DIFF_V_DIM = 64
FFN_CONV = 3
EPS = 1e-5

GDN_W = GDN_HEADS * GDN_HEAD_DIM
DSW_W = DSW_HEADS * DSW_HEAD_DIM
DIFF_W = DIFF_HEADS * DIFF_V_DIM
DIFF_QK_W = DIFF_HEADS * 2 * DIFF_QK_DIM
A_W = 4 * GDN_W
BC_W = 3 * DSW_W + 2 * DIFF_QK_W + DIFF_W

ROW_TILE = 512
LN_ROW_TILE = 1024
LN_ROWS = 128
PROJ_CHUNK = 256
EPI_ROWS = 64
ATT_Q_TILE = 512
ATT_K_TILE = 512
ATT_AUG = 16
LOG2E = math.log2(math.e)
GDN_CHUNK = 128
GDN_GROUP = 2
GDN_SPLIT_SPAN = 16
DSW_MAX_BAND = 512
DSW_FAR_TILE = -(-(DSW_MAX_BAND + ATT_K_TILE) // ATT_Q_TILE)


def _dot(a, b):
    return jnp.dot(a.astype(BF16), b.astype(BF16), preferred_element_type=F32)


def _dot_nt(a, b):
    return lax.dot_general(a.astype(BF16), b.astype(BF16), (((1,), (1,)), ((), ())),
                           preferred_element_type=F32)


def _split_bf16(a):
    hi = a.astype(BF16)
    return hi, (a - hi.astype(F32)).astype(BF16)


def _split3_bf16(a):
    hi, mid = _split_bf16(a)
    return hi, mid, (a - hi.astype(F32) - mid.astype(F32)).astype(BF16)


def _dot_split(a, b):
    ah, al = _split_bf16(a)
    bh, bl = _split_bf16(b)
    return jnp.dot(jnp.concatenate([ah, al, ah], axis=1), jnp.concatenate([bh, bh, bl], axis=0),
                   preferred_element_type=F32)


def _sigmoid(x):
    return 1.0 / (1.0 + jnp.exp(-x))


def _params(*sem):
    return pltpu.CompilerParams(dimension_semantics=sem, vmem_limit_bytes=VMEM_LIMIT)


def _in_proj_kernel(x_ref, w_ref, scale_ref, cw_ref, a_ref, bc_ref, bd_ref,
                    work, carry, *, tiles_per_seq):
    tm = x_ref.shape[0]
    x = x_ref[...].astype(BF16)

    @pl.when(pl.program_id(0) % tiles_per_seq == 0)
    def _():
        carry[...] = jnp.zeros(carry.shape, F32)

    first = SUBLANES - (GDN_CONV - 1)

    def project_conv(cols, slot):
        work[slot, 0:SUBLANES, :] = carry[:, cols]
        work[slot, SUBLANES:SUBLANES + tm, :] = jnp.dot(x, w_ref[:, cols], preferred_element_type=F32)
        carry[:, cols] = work[slot, tm:tm + SUBLANES, :]

    def finish_conv(cols, slot, norm_scale):
        taps = [jnp.broadcast_to(0.5 * cw_ref[j:j + 1, cols], (EPI_ROWS, PROJ_CHUNK)) for j in range(GDN_CONV)]
        for r in range(0, tm, EPI_ROWS):
            y = work[slot, first + r:first + r + EPI_ROWS, :] * taps[0]
            for j in range(1, GDN_CONV):
                y = y + work[slot, first + j + r:first + j + r + EPI_ROWS, :] * taps[j]
            y = y + y * jnp.tanh(y)
            if norm_scale is not None:
                heads = [y[:, h * GDN_HEAD_DIM:(h + 1) * GDN_HEAD_DIM] for h in range(PROJ_CHUNK // GDN_HEAD_DIM)]
                y = jnp.concatenate(
                    [v * (lax.rsqrt(jnp.sum(v * v, axis=-1, keepdims=True) + 1e-6) * norm_scale)
                     for v in heads], axis=1)
            a_ref[r:r + EPI_ROWS, cols] = y.astype(a_ref.dtype)

    def plain(first_col, cols, store):
        wcols = slice(first_col + cols.start, first_col + cols.stop)
        return lambda: store(jnp.dot(x, w_ref[:, wcols], preferred_element_type=F32))

    def store_z(cols):
        def store(u):
            a_ref[:, cols] = u.astype(a_ref.dtype)
        return store

    def store_bc(cols):
        def store(u):
            bc_ref[:, cols] = (u * scale_ref[:, cols]).astype(BF16)
        return store

    def store_bd(u):
        bd_ref[...] = u

    chunks = lambda base, width: [slice(base + c, base + c + PROJ_CHUNK) for c in range(0, width, PROJ_CHUNK)]
    heavy = ([(c, GDN_HEAD_DIM ** -0.5) for c in chunks(0, GDN_W)] + [(c, 1.0) for c in chunks(GDN_W, GDN_W)]
             + [(c, None) for c in chunks(2 * GDN_W, GDN_W)])
    light = ([plain(A_W, c, store_bc(c)) for c in chunks(0, BC_W)]
             + [plain(0, c, store_z(c)) for c in chunks(3 * GDN_W, GDN_W)]
             + [plain(A_W + BC_W, slice(0, LANES), store_bd)])
    project_conv(heavy[0][0], 0)
    for n, (cols, norm_scale) in enumerate(heavy):
        if n + 1 < len(heavy):
            project_conv(heavy[n + 1][0], (n + 1) % 2)
        light[n]()
        finish_conv(cols, n % 2, norm_scale)
    for job in light[len(heavy):]:
        job()


def _in_proj(x, w, scale, conv_w, layer, seq):
    n, d = x.shape
    tm = min(ROW_TILE, seq)
    return pl.pallas_call(
        functools.partial(_in_proj_kernel, tiles_per_seq=seq // tm),
        grid=(n // tm,),
        in_specs=[
            pl.BlockSpec((tm, d), lambda i: (i, 0)),
            pl.BlockSpec((None, d, A_W + BC_W + LANES), lambda i: (layer, 0, 0)),
            pl.BlockSpec((1, BC_W), lambda i: (0, 0)),
            pl.BlockSpec((None, GDN_CONV, 3 * GDN_W), lambda i: (layer, 0, 0)),
        ],
        out_specs=[
            pl.BlockSpec((tm, A_W), lambda i: (i, 0)),
            pl.BlockSpec((tm, BC_W), lambda i: (i, 0)),
            pl.BlockSpec((tm, LANES), lambda i: (i, 0)),
        ],
        out_shape=[
            jax.ShapeDtypeStruct((n, A_W), BF16),
            jax.ShapeDtypeStruct((n, BC_W), BF16),
            jax.ShapeDtypeStruct((n, LANES), F32),
        ],
        scratch_shapes=[pltpu.VMEM((2, tm + SUBLANES, PROJ_CHUNK), F32), pltpu.VMEM((SUBLANES, 3 * GDN_W), F32)],
        compiler_params=_params("arbitrary"),
        name="in_proj",
    )(x, w, scale, conv_w)


def _gdn_kernel(q_ref, k_ref, v_ref, z_ref, bd_ref, alog_ref, dtb_ref, nw_ref, y_ref,
                beta_s, g_s, sa_s, sb_s, oq_s, oc_s, gl_s, *, seq):
    C = GDN_CHUNK
    n_chunks = seq // C
    n_heads = GDN_HEADS
    lanes_of = lambda h: slice(h * LANES, (h + 1) * LANES)

    bd = bd_ref[...]
    beta_all = _sigmoid(bd)
    xg = bd + dtb_ref[...]
    softplus = jnp.maximum(xg, 0.0) + jnp.log(1.0 + jnp.exp(-jnp.abs(xg)))
    g_all = -jnp.exp(alog_ref[...]) * softplus
    for h in range(n_heads):
        beta_s[h] = jnp.broadcast_to(beta_all[:, h:h + 1], (seq, LANES))
        g_s[h] = jnp.broadcast_to(g_all[:, h + GDN_HEADS:h + GDN_HEADS + 1], (seq, LANES))

    ri = lax.broadcasted_iota(jnp.int32, (C, C), 0)
    ci = lax.broadcasted_iota(jnp.int32, (C, C), 1)
    lower_incl = ri >= ci
    strict = ri > ci
    tri = jnp.where(lower_incl, 1.0, 0.0).astype(BF16)
    tri3 = jnp.concatenate([tri, tri, tri], axis=1)

    def chunk_group(gi, carry):
        items = [(h, gi * GDN_GROUP + j) for j in range(GDN_GROUP) for h in range(n_heads)]
        rows = [pl.ds(pl.multiple_of(c * C, C), C) for _, c in items]
        qc = [q_ref[r, lanes_of(h)].astype(F32) for (h, _), r in zip(items, rows)]
        kc = [k_ref[r, lanes_of(h)].astype(F32) for (h, _), r in zip(items, rows)]
        vc = [v_ref[r, lanes_of(h)].astype(F32) for (h, _), r in zip(items, rows)]
        bb = [beta_s[h, r, :] for (h, _), r in zip(items, rows)]
        gg = [g_s[h, r, :] for (h, _), r in zip(items, rows)]
        cum = [jnp.dot(tri3, jnp.concatenate(_split3_bf16(g), axis=0), preferred_element_type=F32)
               for g in gg]
        decay = [jnp.exp(jnp.where(lower_incl, c - c.T, -jnp.inf)) for c in cum]
        kb = [k * b for k, b in zip(kc, bb)]
        scores = [_dot_nt(jnp.concatenate([x, q], axis=0), k) for x, q, k in zip(kb, qc, kc)]
        m = [jnp.where(strict, s[0:C, :] * d, 0.0) for s, d in zip(scores, decay)]
        qk = [jnp.where(lower_incl, s[C:, :] * d, 0.0) for s, d in zip(scores, decay)]
        nmat = [-x for x in m]
        p = [_dot_split(x, x) for x in m]
        span = 2
        while 2 * span < C:
            dot = _dot_split if span < GDN_SPLIT_SPAN else _dot
            both = [dot(jnp.concatenate([n, x], axis=0), x) for n, x in zip(nmat, p)]
            nmat = [n + x + b[0:C, :] for n, x, b in zip(nmat, p, both)]
            p = [b[C:, :] for b in both]
            span *= 2
        nmat = [n + x + _dot(n, x) for n, x in zip(nmat, p)]
        ecum = [jnp.exp(c) for c in cum]
        vb = [v * b for v, b in zip(vc, bb)]
        kbe = [x * e for x, e in zip(kb, ecum)]
        wu = [jnp.concatenate([x, y], axis=1) for x, y in zip(kbe, vb)]
        wu = [x + _dot(n, x) for n, x in zip(nmat, wu)]
        qd = [q * e for q, e in zip(qc, ecum)]
        cum_last = [c[C - 1:C, :] for c in cum]
        ktt = [(k * jnp.exp(cl - c)).T for k, cl, c in zip(kc, cum_last, cum)]
        prod = [_dot(jnp.concatenate([kt, a], axis=0), x) for kt, a, x in zip(ktt, qk, wu)]
        for j, ((h, c), r) in enumerate(zip(items, rows)):
            sa_s[h, r, :] = (-prod[j][0:C, 0:C]).astype(sa_s.dtype)
            sb_s[h, r, :] = prod[j][0:C, C:]
            oq_s[h, r, :] = (qd[j] - prod[j][C:, 0:C]).astype(oq_s.dtype)
            oc_s[h, r, :] = prod[j][C:, C:]
            gl_s[h, pl.ds(pl.multiple_of(c * SUBLANES, SUBLANES), SUBLANES), :] = (
                jnp.broadcast_to(jnp.exp(cum_last[j]), (SUBLANES, LANES)))
        return carry

    lax.fori_loop(0, n_chunks // GDN_GROUP, chunk_group, 0)

    nw = nw_ref[...]

    def scan_step(c, states):
        rows = pl.ds(pl.multiple_of(c * C, C), C)
        new = []
        for h, s in enumerate(states):
            gl = gl_s[h, pl.ds(pl.multiple_of(c * SUBLANES, SUBLANES), 1), :]
            new.append(s * gl + (_dot(sa_s[h, rows, :], s) + sb_s[h, rows, :]))
        for h, s in enumerate(states):
            o = _dot(oq_s[h, rows, :], s) + oc_s[h, rows, :]
            z = z_ref[rows, lanes_of(h)].astype(F32)
            o = o * lax.rsqrt(jnp.mean(o * o, axis=-1, keepdims=True) + EPS) * nw
            y_ref[rows, lanes_of(h)] = (o * (z * _sigmoid(z))).astype(y_ref.dtype)
        return tuple(new)

    lax.fori_loop(0, n_chunks, scan_step,
                  tuple(jnp.zeros((GDN_HEAD_DIM, GDN_HEAD_DIM), F32) for _ in range(n_heads)))


def _gdn(a, bd, alog_row, dtb_row, norm_row, layer):
    b, t, _ = a.shape
    tok = lambda j: pl.BlockSpec((None, t, GDN_W), lambda i: (i, 0, j))
    row = pl.BlockSpec((None, 1, LANES), lambda i: (layer, 0, 0))
    head_buf = lambda dtype: pltpu.VMEM((GDN_HEADS, t, LANES), dtype)
    return pl.pallas_call(
        functools.partial(_gdn_kernel, seq=t),
        grid=(b,),
        in_specs=[tok(0), tok(1), tok(2), tok(3),
                  pl.BlockSpec((None, t, LANES), lambda i: (i, 0, 0)), row, row, row],
        out_specs=pl.BlockSpec((None, t, GDN_W), lambda i: (i, 0, 0)),
        out_shape=jax.ShapeDtypeStruct((b, t, GDN_W), BF16),
        scratch_shapes=[head_buf(F32), head_buf(F32), head_buf(BF16), head_buf(F32), head_buf(BF16),
                        head_buf(F32)]
        + [pltpu.VMEM((GDN_HEADS, t // GDN_CHUNK * SUBLANES, LANES), F32)],
        compiler_params=_params("arbitrary"),
        name="gdn",
    )(a, a, a, a, bd, alog_row, dtb_row, norm_row)


def _stage_values(v_ref, vt_s, seq, dv):
    aug = dv + ATT_AUG
    ones_row = jnp.where(lax.broadcasted_iota(jnp.int32, (ATT_AUG, ATT_K_TILE), 0) == 0, 1.0, 0.0).astype(BF16)
    for g in range(v_ref.shape[1] // LANES):
        for t in range(seq // ATT_K_TILE):
            for h in range(LANES // dv):
                vt_s[g, t, h * aug + dv:(h + 1) * aug, :] = ones_row
        for c in range(seq // LANES):
            t, off = divmod(c * LANES, ATT_K_TILE)
            blk = v_ref[c * LANES:(c + 1) * LANES, g * LANES:(g + 1) * LANES].astype(F32).T.astype(BF16)
            for h in range(LANES // dv):
                vt_s[g, t, h * aug:h * aug + dv, off:off + LANES] = blk[h * dv:(h + 1) * dv, :]


def _stack_masked(q, width):
    lane = lax.broadcasted_iota(jnp.int32, q.shape, 1)
    return jnp.concatenate(
        [jnp.where((lane >= j * width) & (lane < (j + 1) * width), q, 0.0).astype(BF16)
         for j in range(LANES // width)], axis=0)


def _softmax_init(rows, width):
    return (jnp.full((1, width), -jnp.inf, F32), jnp.zeros((rows, width), F32))


def _softmax_tile(state, scores, cnt, vt):
    ps, alphas, maxes = [], [], []
    for (m_old, _), s in zip(state, scores):
        m_new = jnp.maximum(m_old, jnp.max(s, axis=0, keepdims=True))
        p = jnp.exp2(s - m_new)
        if cnt is not None:
            p = p * cnt
        alphas.append(jnp.exp2(m_old - m_new))
        maxes.append(m_new)
        ps.append(p.astype(BF16))
    pv = jnp.dot(vt, jnp.concatenate(ps, axis=1), preferred_element_type=F32)
    width = pv.shape[1] // len(state)
    return [(m, a * acc + pv[:, n * width:(n + 1) * width])
            for n, (m, a, (_, acc)) in enumerate(zip(maxes, alphas, state))]


def _dsw_kernel(q_ref, k_ref, v_ref, o_ref, vt_s, cnt_s, bias_s, *, seq):
    tq, tk = ATT_Q_TILE, ATT_K_TILE
    blocks = q_ref.shape[1] // LANES
    dv = DSW_HEAD_DIM
    heads = LANES // dv
    aug = dv + ATT_AUG
    _stage_values(v_ref, vt_s, seq, dv)

    @pl.when(pl.program_id(0) == 0)
    def _():
        kr = lax.broadcasted_iota(jnp.int32, (tk, tq), 0)
        qc = lax.broadcasted_iota(jnp.int32, (tk, tq), 1)
        for d in range(DSW_FAR_TILE + 1):
            delta = d * tq + qc - kr
            causal = delta >= 0
            cnt = (jnp.where(causal & (delta <= 128), 1.0, 0.0)
                   + jnp.where(causal & (delta <= DSW_MAX_BAND) & ((delta & 3) == 0), 1.0, 0.0)
                   + jnp.where(causal & ((delta & 15) == 0), 1.0, 0.0))
            cnt_s[d] = cnt.astype(F32)
            bias_s[d] = jnp.where(cnt > 0.0, 0.0, -jnp.inf).astype(F32)

    for qi in range(seq // tq):
        qrows = slice(qi * tq, (qi + 1) * tq)
        qstack = [_stack_masked(q_ref[qrows, g * LANES:(g + 1) * LANES].astype(F32), DSW_HEAD_DIM)
                  for g in range(blocks)]
        state = [_softmax_init(aug, tq) for _ in range(blocks * heads)]
        for kj in range((qi * tq) // tk + 1):
            krows = slice(kj * tk, (kj + 1) * tk)
            far = min(qi - kj * (tk // tq), DSW_FAR_TILE)
            cnt = cnt_s[far] if far < DSW_FAR_TILE else None
            bias = bias_s[far]
            s = [_dot_nt(k_ref[krows, g * LANES:(g + 1) * LANES], qstack[g]) for g in range(blocks)]
            for g in range(blocks):
                for h in range(heads):
                    i = g * heads + h
                    state[i:i + 1] = _softmax_tile(state[i:i + 1], [s[g][:, h * tq:(h + 1) * tq] + bias],
                                                   cnt, vt_s[g, kj, h * aug:(h + 1) * aug, :])
        o = jnp.concatenate([acc[0:dv, :] / acc[dv:dv + 1, :] for _, acc in state], axis=0)
        o_ref[qrows, :] = o.T.astype(o_ref.dtype)


def _dsw(bc):
    b, t, _ = bc.shape
    spec = lambda j: pl.BlockSpec((None, t, DSW_W), lambda i: (i, 0, j))
    return pl.pallas_call(
        functools.partial(_dsw_kernel, seq=t),
        grid=(b,),
        in_specs=[spec(0), spec(1), spec(2)],
        out_specs=pl.BlockSpec((None, t, DSW_W), lambda i: (i, 0, 0)),
        out_shape=jax.ShapeDtypeStruct((b, t, DSW_W), BF16),
        scratch_shapes=[pltpu.VMEM((DSW_W // LANES, t // ATT_K_TILE,
                                    LANES // DSW_HEAD_DIM * (DSW_HEAD_DIM + ATT_AUG), ATT_K_TILE), BF16)]
        + [pltpu.VMEM((DSW_FAR_TILE + 1, ATT_K_TILE, ATT_Q_TILE), F32)] * 2,
        compiler_params=_params("arbitrary"),
        name="dsw",
    )(bc, bc, bc)


def _diff_kernel(q_ref, k_ref, v_ref, lam_ref, laminit_ref, nw_ref, o_ref, vt_s, *, seq):
    tq, tk = ATT_Q_TILE, ATT_K_TILE
    blocks = q_ref.shape[1] // LANES
    dv = DIFF_V_DIM
    heads = LANES // dv
    aug = dv + ATT_AUG
    _stage_values(v_ref, vt_s, seq, dv)
    kr = lax.broadcasted_iota(jnp.int32, (tk, tq), 0)
    qc = lax.broadcasted_iota(jnp.int32, (tk, tq), 1)
    lv = lam_ref[...]
    lam_init = laminit_ref[...]
    lam = (jnp.exp(jnp.sum(lv[0:1, :] * lv[1:2, :], keepdims=True))
           - jnp.exp(jnp.sum(lv[2:3, :] * lv[3:4, :], keepdims=True)) + lam_init)
    nw = nw_ref[...]

    for qi in range(seq // tq):
        qrows = slice(qi * tq, (qi + 1) * tq)
        qstack = [_stack_masked(q_ref[qrows, g * LANES:(g + 1) * LANES].astype(F32), DIFF_QK_DIM)
                  for g in range(blocks)]
        state = [_softmax_init(dv + ATT_AUG, tq) for _ in range(2 * heads * blocks)]
        n_full = (qi * tq) // tk
        for kj in range(n_full + 1):
            krows = slice(kj * tk, (kj + 1) * tk)
            valid = None if kj < n_full else kr + (kj * tk - qi * tq) <= qc
            s = [_dot_nt(k_ref[krows, g * LANES:(g + 1) * LANES], qstack[g])
                 for g in range(blocks)]
            for g in range(blocks):
                for h in range(heads):
                    scores = [s[g][:, (2 * h + mp) * tq:(2 * h + mp + 1) * tq] for mp in range(2)]
                    if valid is not None:
                        scores = [jnp.where(valid, x, -jnp.inf) for x in scores]
                    first = 2 * (heads * g + h)
                    state[first:first + 2] = _softmax_tile(state[first:first + 2], scores, None,
                                                           vt_s[g, kj, h * aug:(h + 1) * aug, :])
        att = [acc[0:dv, :] / acc[dv:dv + 1, :] for _, acc in state]
        halves = []
        for h in range(heads * blocks):
            o = att[2 * h] - lam * att[2 * h + 1]
            halves.append(o * lax.rsqrt(jnp.mean(o * o, axis=0, keepdims=True) + EPS))
        o = jnp.concatenate(halves, axis=0).T
        o_ref[qrows, :] = (o * nw * (1.0 - lam_init)).astype(o_ref.dtype)


def _diff(bc, lam_vecs, lam_init, norm_row):
    b, t, _ = bc.shape
    blocks = DIFF_W // LANES
    base = 3 * DSW_W // LANES
    spec = lambda j: pl.BlockSpec((None, t, LANES), lambda i, p: (i, 0, base + j * blocks + p))
    return pl.pallas_call(
        functools.partial(_diff_kernel, seq=t),
        grid=(b, blocks),
        in_specs=[spec(0), spec(1), spec(2),
                  pl.BlockSpec((4, DIFF_QK_DIM), lambda i, p: (0, 0)),
                  pl.BlockSpec((1, 1), lambda i, p: (0, 0)),
                  pl.BlockSpec((1, LANES), lambda i, p: (0, p))],
        out_specs=pl.BlockSpec((None, t, LANES), lambda i, p: (i, 0, p)),
        out_shape=jax.ShapeDtypeStruct((b, t, DIFF_W), BF16),
        scratch_shapes=[pltpu.VMEM((1, t // ATT_K_TILE,
                                    LANES // DIFF_V_DIM * (DIFF_V_DIM + ATT_AUG), ATT_K_TILE), BF16)],
        compiler_params=_params("arbitrary", "arbitrary"),
        name="diff",
    )(bc, bc, bc, lam_vecs, lam_init, norm_row)


def _layer_norm(h, g, b):
    mu = jnp.mean(h, axis=-1, keepdims=True)
    hc = h - mu
    var = jnp.mean(hc * hc, axis=-1, keepdims=True)
    return hc * lax.rsqrt(var + EPS) * g + b


def _residual_layer_norm(project, x_ref, g_ref, b_ref, o_ref, alpha):
    tm = x_ref.shape[0]
    ahead = project(slice(0, LN_ROWS))
    for r in range(0, tm, LN_ROWS):
        y = ahead
        if r + LN_ROWS < tm:
            ahead = project(slice(r + LN_ROWS, r + 2 * LN_ROWS))
        o_ref[r:r + LN_ROWS, :] = _layer_norm(alpha * x_ref[r:r + LN_ROWS, :] + y, g_ref[...], b_ref[...])


def _out_ln_kernel(x_ref, ya_ref, yb_ref, yc_ref, w_ref, g_ref, b_ref, o_ref, *, alpha):
    def project(rows):
        mixed = jnp.concatenate([ya_ref[rows, :], yb_ref[rows, :], yc_ref[rows, :]], axis=1)
        return jnp.dot(mixed, w_ref[...], preferred_element_type=F32)

    _residual_layer_norm(project, x_ref, g_ref, b_ref, o_ref, alpha)


def _out_ln(x, ya, yb, yc, w, g, b, layer, alpha):
    n, d = x.shape
    tm = min(LN_ROW_TILE, n)
    rows = lambda width: pl.BlockSpec((tm, width), lambda i: (i, 0))
    vec = pl.BlockSpec((None, 1, d), lambda i: (layer, 0, 0))
    return pl.pallas_call(
        functools.partial(_out_ln_kernel, alpha=alpha),
        grid=(n // tm,),
        in_specs=[rows(d), rows(GDN_W), rows(DSW_W), rows(DIFF_W),
                  pl.BlockSpec((None, d, d), lambda i: (layer, 0, 0)), vec, vec],
        out_specs=rows(d),
        out_shape=jax.ShapeDtypeStruct((n, d), F32),
        compiler_params=_params("arbitrary"),
        name="out_ln",
    )(x, ya, yb, yc, w, g, b)


def _ffn_up_kernel(x_ref, w_ref, cg_ref, cv_ref, h_ref, work_g, work_v, carry_g, carry_v,
                   *, tiles_per_seq, col_chunk):
    tm = x_ref.shape[0]
    width = h_ref.shape[1]

    @pl.when(pl.program_id(0) % tiles_per_seq == 0)
    def _():
        carry_g[...] = jnp.zeros(carry_g.shape, F32)
        carry_v[...] = jnp.zeros(carry_v.shape, F32)

    x = x_ref[...].astype(BF16)
    first = SUBLANES - (FFN_CONV - 1)

    def project(c, slot):
        cols = slice(c, c + col_chunk)
        for first_col, work, carry in ((0, work_g, carry_g), (width, work_v, carry_v)):
            work[slot, 0:SUBLANES, :] = carry[:, cols]
            work[slot, SUBLANES:SUBLANES + tm, :] = jnp.dot(
                x, w_ref[:, first_col + c:first_col + c + col_chunk], preferred_element_type=F32)
            carry[:, cols] = work[slot, tm:tm + SUBLANES, :]

    def conv(work, slot, taps, r):
        acc = work[slot, first + r:first + r + EPI_ROWS, :] * taps[0]
        for j in range(1, FFN_CONV):
            acc = acc + work[slot, first + j + r:first + j + r + EPI_ROWS, :] * taps[j]
        return acc

    project(0, 0)
    for n, c in enumerate(range(0, width, col_chunk)):
        slot = n % 2
        if c + col_chunk < width:
            project(c + col_chunk, 1 - slot)
        cols = slice(c, c + col_chunk)
        taps_g = [jnp.broadcast_to(0.5 * cg_ref[j:j + 1, cols], (EPI_ROWS, col_chunk)) for j in range(FFN_CONV)]
        taps_v = [jnp.broadcast_to(cv_ref[j:j + 1, cols], (EPI_ROWS, col_chunk)) for j in range(FFN_CONV)]
        for r in range(0, tm, EPI_ROWS):
            half = conv(work_g, slot, taps_g, r)
            val = conv(work_v, slot, taps_v, r)
            h_ref[r:r + EPI_ROWS, cols] = ((half + half * jnp.tanh(half)) * val).astype(h_ref.dtype)


def _ffn_up(x, w, cg, cv, layer, seq):
    n, d = x.shape
    width = w.shape[-1] // 2
    tm = min(ROW_TILE, seq)
    col_chunk = 256
    wspec = pl.BlockSpec((None, d, 2 * width), lambda i: (layer, 0, 0))
    cspec = pl.BlockSpec((None, FFN_CONV, width), lambda i: (layer, 0, 0))
    return pl.pallas_call(
        functools.partial(_ffn_up_kernel, tiles_per_seq=seq // tm, col_chunk=col_chunk),
        grid=(n // tm,),
        in_specs=[pl.BlockSpec((tm, d), lambda i: (i, 0)), wspec, cspec, cspec],
        out_specs=pl.BlockSpec((tm, width), lambda i: (i, 0)),
        out_shape=jax.ShapeDtypeStruct((n, width), BF16),
        scratch_shapes=[pltpu.VMEM((2, tm + SUBLANES, col_chunk), F32)] * 2
        + [pltpu.VMEM((SUBLANES, width), F32)] * 2,
        compiler_params=_params("arbitrary"),
        name="ffn_up",
    )(x, w, cg, cv)


def _ffn_down_kernel(x_ref, h_ref, w_ref, g_ref, b_ref, o_ref, *, alpha):
    project = lambda rows: jnp.dot(h_ref[rows, :], w_ref[...], preferred_element_type=F32)
    _residual_layer_norm(project, x_ref, g_ref, b_ref, o_ref, alpha)


def _ffn_down(x, h, w, g, b, layer, alpha):
    n, d = x.shape
    width = h.shape[1]
    tm = min(LN_ROW_TILE, n)
    vec = pl.BlockSpec((None, 1, d), lambda i: (layer, 0, 0))
    return pl.pallas_call(
        functools.partial(_ffn_down_kernel, alpha=alpha),
        grid=(n // tm,),
        in_specs=[pl.BlockSpec((tm, d), lambda i: (i, 0)), pl.BlockSpec((tm, width), lambda i: (i, 0)),
                  pl.BlockSpec((None, width, d), lambda i: (layer, 0, 0)), vec, vec],
        out_specs=pl.BlockSpec((tm, d), lambda i: (i, 0)),
        out_shape=jax.ShapeDtypeStruct((n, d), F32),
        compiler_params=_params("arbitrary"),
        name="ffn_down",
    )(x, h, w, g, b)


def _pad_last(a, width):
    return jnp.pad(a, [(0, 0)] * (a.ndim - 1) + [(0, width - a.shape[-1])])


def kernel(x, w_in, gdn_conv, gdn_a_log, gdn_dt_bias, gdn_norm, diff_lambda, diff_norm, w_out,
           ln1_g, ln1_b, w_up, ffn_conv, w_down, ln2_g, ln2_b):
    batch, seq, d = x.shape
    depth = w_in.shape[0]
    d_ff = w_down.shape[1]
    ff_pad = -(-d_ff // LANES) * LANES
    alpha = (2 * depth) ** 0.25
    assert w_in.shape[2] == A_W + 2 * GDN_HEADS + BC_W and w_out.shape[1] == GDN_W + DSW_W + DIFF_W
    assert seq % ROW_TILE == 0 and (batch * seq) % LN_ROW_TILE == 0
    assert seq % ATT_Q_TILE == 0 and seq % ATT_K_TILE == 0 and seq % (GDN_GROUP * GDN_CHUNK) == 0
    assert seq <= 2048

    bd0 = A_W
    bc0 = A_W + 2 * GDN_HEADS
    w_in_r = jnp.concatenate(
        [w_in[:, :, :A_W], w_in[:, :, bc0:], _pad_last(w_in[:, :, bd0:bc0], LANES)], axis=-1).astype(BF16)
    scale = jnp.concatenate([
        jnp.full((DSW_W,), DSW_HEAD_DIM ** -0.5 * LOG2E, F32), jnp.ones((2 * DSW_W,), F32),
        jnp.full((DIFF_QK_W,), DIFF_QK_DIM ** -0.5 * LOG2E, F32), jnp.ones((DIFF_QK_W + DIFF_W,), F32)])[None, :]
    lane_row = lambda v, off: jnp.pad(v, ((0, 0), (off, LANES - off - v.shape[1])))[:, None, :]
    alog_row = lane_row(gdn_a_log, GDN_HEADS)
    dtb_row = lane_row(gdn_dt_bias, GDN_HEADS)
    gdn_norm_row = gdn_norm[:, None, :]
    diff_norm_row = jnp.tile(diff_norm, (1, DIFF_HEADS))[:, None, :]
    lam_init = jnp.asarray([0.8 - 0.6 * math.exp(-0.3 * l) for l in range(depth)], F32)[:, None, None]
    w_out_b = w_out.astype(BF16)
    w_up_r = jnp.concatenate(
        [_pad_last(w_up[:, :, :d_ff], ff_pad), _pad_last(w_up[:, :, d_ff:], ff_pad)], axis=-1).astype(BF16)
    cg = _pad_last(ffn_conv[:, :, :d_ff], ff_pad)
    cv = _pad_last(ffn_conv[:, :, d_ff:], ff_pad)
    w_down_b = jnp.pad(w_down, ((0, 0), (0, ff_pad - d_ff), (0, 0))).astype(BF16)
    vec3 = lambda v: v[:, None, :]

    xf = x.reshape(batch * seq, d)
    for l in range(depth):
        a, bc, bd = _in_proj(xf, w_in_r, scale, gdn_conv, l, seq)
        bc3 = bc.reshape(batch, seq, BC_W)
        ya = _gdn(a.reshape(batch, seq, A_W), bd.reshape(batch, seq, LANES),
                  alog_row, dtb_row, gdn_norm_row, l)
        yb = _dsw(bc3)
        yc = _diff(bc3, diff_lambda[l], lam_init[l], diff_norm_row[l])
        n = batch * seq
        x1 = _out_ln(xf, ya.reshape(n, GDN_W), yb.reshape(n, DSW_W), yc.reshape(n, DIFF_W),
                     w_out_b, vec3(ln1_g), vec3(ln1_b), l, alpha)
        h = _ffn_up(x1, w_up_r, cg, cv, l, seq)
        xf = _ffn_down(x1, h, w_down_b, vec3(ln2_g), vec3(ln2_b), l, alpha)
    return xf.reshape(batch, seq, d)
```

```python
import functools
import math

import jax
import jax.numpy as jnp
from jax import lax
from jax.experimental import pallas as pl
from jax.experimental.pallas import tpu as pltpu

F32 = jnp.float32
BF16 = jnp.bfloat16

LANES = 128
SUBLANES = 8
VMEM_LIMIT = 48 * 1024 * 1024

GDN_HEADS = 4
GDN_HEAD_DIM = 128
GDN_CONV = 4
DSW_HEADS = 4
DSW_HEAD_DIM = 64
DIFF_HEADS = 4
DIFF_QK_DIM = 32
DIFF_V_DIM = 64
FFN_CONV = 3
EPS = 1e-5

GDN_W = GDN_HEADS * GDN_HEAD_DIM
DSW_W = DSW_HEADS * DSW_HEAD_DIM
DIFF_W = DIFF_HEADS * DIFF_V_DIM
DIFF_QK_W = DIFF_HEADS * 2 * DIFF_QK_DIM
A_W = 4 * GDN_W
BC_W = 3 * DSW_W + 2 * DIFF_QK_W + DIFF_W

ROW_TILE = 512
LN_ROW_TILE = 1024
LN_ROWS = 128
PROJ_CHUNK = 256
EPI_ROWS = 64
ATT_Q_TILE = 512
ATT_K_TILE = 512
ATT_AUG = 16
LOG2E = math.log2(math.e)
GDN_CHUNK = 128
GDN_GROUP = 2
GDN_SPLIT_SPAN = 16
DSW_MAX_BAND = 512
DSW_FAR_TILE = -(-(DSW_MAX_BAND + ATT_K_TILE) // ATT_Q_TILE)


def _dot(a, b):
    return jnp.dot(a.astype(BF16), b.astype(BF16), preferred_element_type=F32)


def _dot_nt(a, b):
    return lax.dot_general(a.astype(BF16), b.astype(BF16), (((1,), (1,)), ((), ())),
                           preferred_element_type=F32)


def _split_bf16(a):
    hi = a.astype(BF16)
    return hi, (a - hi.astype(F32)).astype(BF16)


def _split3_bf16(a):
    hi, mid = _split_bf16(a)
    return hi, mid, (a - hi.astype(F32) - mid.astype(F32)).astype(BF16)


def _dot_split(a, b):
    ah, al = _split_bf16(a)
    bh, bl = _split_bf16(b)
    return jnp.dot(jnp.concatenate([ah, al, ah], axis=1), jnp.concatenate([bh, bh, bl], axis=0),
                   preferred_element_type=F32)


def _sigmoid(x):
    return 1.0 / (1.0 + jnp.exp(-x))


def _params(*sem):
    return pltpu.CompilerParams(dimension_semantics=sem, vmem_limit_bytes=VMEM_LIMIT)


def _in_proj_kernel(x_ref, w_ref, scale_ref, cw_ref, a_ref, bc_ref, bd_ref,
                    work, carry, *, tiles_per_seq):
    tm = x_ref.shape[0]
    x = x_ref[...].astype(BF16)

    @pl.when(pl.program_id(0) % tiles_per_seq == 0)
    def _():
        carry[...] = jnp.zeros(carry.shape, F32)

    first = SUBLANES - (GDN_CONV - 1)

    def project_conv(cols, slot):
        work[slot, 0:SUBLANES, :] = carry[:, cols]
        work[slot, SUBLANES:SUBLANES + tm, :] = jnp.dot(x, w_ref[:, cols], preferred_element_type=F32)
        carry[:, cols] = work[slot, tm:tm + SUBLANES, :]

    def finish_conv(cols, slot, norm_scale):
        taps = [jnp.broadcast_to(0.5 * cw_ref[j:j + 1, cols], (EPI_ROWS, PROJ_CHUNK)) for j in range(GDN_CONV)]
        for r in range(0, tm, EPI_ROWS):
            y = work[slot, first + r:first + r + EPI_ROWS, :] * taps[0]
            for j in range(1, GDN_CONV):
                y = y + work[slot, first + j + r:first + j + r + EPI_ROWS, :] * taps[j]
            y = y + y * jnp.tanh(y)
            if norm_scale is not None:
                heads = [y[:, h * GDN_HEAD_DIM:(h + 1) * GDN_HEAD_DIM] for h in range(PROJ_CHUNK // GDN_HEAD_DIM)]
                y = jnp.concatenate(
                    [v * (lax.rsqrt(jnp.sum(v * v, axis=-1, keepdims=True) + 1e-6) * norm_scale)
                     for v in heads], axis=1)
            a_ref[r:r + EPI_ROWS, cols] = y.astype(a_ref.dtype)

    def plain(first_col, cols, store):
        wcols = slice(first_col + cols.start, first_col + cols.stop)
        return lambda: store(jnp.dot(x, w_ref[:, wcols], preferred_element_type=F32))

    def store_z(cols):
        def store(u):
            a_ref[:, cols] = u.astype(a_ref.dtype)
        return store

    def store_bc(cols):
        def store(u):
            bc_ref[:, cols] = (u * scale_ref[:, cols]).astype(BF16)
        return store

    def store_bd(u):
        bd_ref[...] = u

    chunks = lambda base, width: [slice(base + c, base + c + PROJ_CHUNK) for c in range(0, width, PROJ_CHUNK)]
    heavy = ([(c, GDN_HEAD_DIM ** -0.5) for c in chunks(0, GDN_W)] + [(c, 1.0) for c in chunks(GDN_W, GDN_W)]
             + [(c, None) for c in chunks(2 * GDN_W, GDN_W)])
    light = ([plain(A_W, c, store_bc(c)) for c in chunks(0, BC_W)]
             + [plain(0, c, store_z(c)) for c in chunks(3 * GDN_W, GDN_W)]
             + [plain(A_W + BC_W, slice(0, LANES), store_bd)])
    project_conv(heavy[0][0], 0)
    for n, (cols, norm_scale) in enumerate(heavy):
        if n + 1 < len(heavy):
            project_conv(heavy[n + 1][0], (n + 1) % 2)
        light[n]()
        finish_conv(cols, n % 2, norm_scale)
    for job in light[len(heavy):]:
        job()


def _in_proj(x, w, scale, conv_w, layer, seq):
    n, d = x.shape
    tm = min(ROW_TILE, seq)
    return pl.pallas_call(
        functools.partial(_in_proj_kernel, tiles_per_seq=seq // tm),
        grid=(n // tm,),
        in_specs=[
            pl.BlockSpec((tm, d), lambda i: (i, 0)),
            pl.BlockSpec((None, d, A_W + BC_W + LANES), lambda i: (layer, 0, 0)),
            pl.BlockSpec((1, BC_W), lambda i: (0, 0)),
            pl.BlockSpec((None, GDN_CONV, 3 * GDN_W), lambda i: (layer, 0, 0)),
        ],
        out_specs=[
            pl.BlockSpec((tm, A_W), lambda i: (i, 0)),
            pl.BlockSpec((tm, BC_W), lambda i: (i, 0)),
            pl.BlockSpec((tm, LANES), lambda i: (i, 0)),
        ],
        out_shape=[
            jax.ShapeDtypeStruct((n, A_W), BF16),
            jax.ShapeDtypeStruct((n, BC_W), BF16),
            jax.ShapeDtypeStruct((n, LANES), F32),
        ],
        scratch_shapes=[pltpu.VMEM((2, tm + SUBLANES, PROJ_CHUNK), F32), pltpu.VMEM((SUBLANES, 3 * GDN_W), F32)],
        compiler_params=_params("arbitrary"),
        name="in_proj",
    )(x, w, scale, conv_w)


def _gdn_kernel(q_ref, k_ref, v_ref, z_ref, bd_ref, alog_ref, dtb_ref, nw_ref, y_ref,
                beta_s, g_s, sa_s, sb_s, oq_s, oc_s, gl_s, *, seq):
    C = GDN_CHUNK
    n_chunks = seq // C
    n_heads = GDN_HEADS
    lanes_of = lambda h: slice(h * LANES, (h + 1) * LANES)

    bd = bd_ref[...]
    beta_all = _sigmoid(bd)
    xg = bd + dtb_ref[...]
    softplus = jnp.maximum(xg, 0.0) + jnp.log(1.0 + jnp.exp(-jnp.abs(xg)))
    g_all = -jnp.exp(alog_ref[...]) * softplus
    for h in range(n_heads):
        beta_s[h] = jnp.broadcast_to(beta_all[:, h:h + 1], (seq, LANES))
        g_s[h] = jnp.broadcast_to(g_all[:, h + GDN_HEADS:h + GDN_HEADS + 1], (seq, LANES))

    ri = lax.broadcasted_iota(jnp.int32, (C, C), 0)
    ci = lax.broadcasted_iota(jnp.int32, (C, C), 1)
    lower_incl = ri >= ci
    strict = ri > ci
    tri = jnp.where(lower_incl, 1.0, 0.0).astype(BF16)
    tri3 = jnp.concatenate([tri, tri, tri], axis=1)

    def chunk_group(gi, carry):
        items = [(h, gi * GDN_GROUP + j) for j in range(GDN_GROUP) for h in range(n_heads)]
        rows = [pl.ds(pl.multiple_of(c * C, C), C) for _, c in items]
        qc = [q_ref[r, lanes_of(h)].astype(F32) for (h, _), r in zip(items, rows)]
        kc = [k_ref[r, lanes_of(h)].astype(F32) for (h, _), r in zip(items, rows)]
        vc = [v_ref[r, lanes_of(h)].astype(F32) for (h, _), r in zip(items, rows)]
        bb = [beta_s[h, r, :] for (h, _), r in zip(items, rows)]
        gg = [g_s[h, r, :] for (h, _), r in zip(items, rows)]
        cum = [jnp.dot(tri3, jnp.concatenate(_split3_bf16(g), axis=0), preferred_element_type=F32)
               for g in gg]
        decay = [jnp.exp(jnp.where(lower_incl, c - c.T, -jnp.inf)) for c in cum]
        kb = [k * b for k, b in zip(kc, bb)]
        scores = [_dot_nt(jnp.concatenate([x, q], axis=0), k) for x, q, k in zip(kb, qc, kc)]
        m = [jnp.where(strict, s[0:C, :] * d, 0.0) for s, d in zip(scores, decay)]
        qk = [jnp.where(lower_incl, s[C:, :] * d, 0.0) for s, d in zip(scores, decay)]
        nmat = [-x for x in m]
        p = [_dot_split(x, x) for x in m]
        span = 2
        while 2 * span < C:
            dot = _dot_split if span < GDN_SPLIT_SPAN else _dot
            both = [dot(jnp.concatenate([n, x], axis=0), x) for n, x in zip(nmat, p)]
            nmat = [n + x + b[0:C, :] for n, x, b in zip(nmat, p, both)]
            p = [b[C:, :] for b in both]
            span *= 2
        nmat = [n + x + _dot(n, x) for n, x in zip(nmat, p)]
        ecum = [jnp.exp(c) for c in cum]
        vb = [v * b for v, b in zip(vc, bb)]
        kbe = [x * e for x, e in zip(kb, ecum)]
        wu = [jnp.concatenate([x, y], axis=1) for x, y in zip(kbe, vb)]
        wu = [x + _dot(n, x) for n, x in zip(nmat, wu)]
        qd = [q * e for q, e in zip(qc, ecum)]
        cum_last = [c[C - 1:C, :] for c in cum]
        ktt = [(k * jnp.exp(cl - c)).T for k, cl, c in zip(kc, cum_last, cum)]
        prod = [_dot(jnp.concatenate([kt, a], axis=0), x) for kt, a, x in zip(ktt, qk, wu)]
        for j, ((h, c), r) in enumerate(zip(items, rows)):
            sa_s[h, r, :] = (-prod[j][0:C, 0:C]).astype(sa_s.dtype)
            sb_s[h, r, :] = prod[j][0:C, C:]
            oq_s[h, r, :] = (qd[j] - prod[j][C:, 0:C]).astype(oq_s.dtype)
            oc_s[h, r, :] = prod[j][C:, C:]
            gl_s[h, pl.ds(pl.multiple_of(c * SUBLANES, SUBLANES), SUBLANES), :] = (
                jnp.broadcast_to(jnp.exp(cum_last[j]), (SUBLANES, LANES)))
        return carry

    lax.fori_loop(0, n_chunks // GDN_GROUP, chunk_group, 0)

    nw = nw_ref[...]

    def scan_step(c, states):
        rows = pl.ds(pl.multiple_of(c * C, C), C)
        new = []
        for h, s in enumerate(states):
            gl = gl_s[h, pl.ds(pl.multiple_of(c * SUBLANES, SUBLANES), 1), :]
            new.append(s * gl + (_dot(sa_s[h, rows, :], s) + sb_s[h, rows, :]))
        for h, s in enumerate(states):
            o = _dot(oq_s[h, rows, :], s) + oc_s[h, rows, :]
            z = z_ref[rows, lanes_of(h)].astype(F32)
            o = o * lax.rsqrt(jnp.mean(o * o, axis=-1, keepdims=True) + EPS) * nw
            y_ref[rows, lanes_of(h)] = (o * (z * _sigmoid(z))).astype(y_ref.dtype)
        return tuple(new)

    lax.fori_loop(0, n_chunks, scan_step,
                  tuple(jnp.zeros((GDN_HEAD_DIM, GDN_HEAD_DIM), F32) for _ in range(n_heads)))


def _gdn(a, bd, alog_row, dtb_row, norm_row, layer):
    b, t, _ = a.shape
    tok = lambda j: pl.BlockSpec((None, t, GDN_W), lambda i: (i, 0, j))
    row = pl.BlockSpec((None, 1, LANES), lambda i: (layer, 0, 0))
    head_buf = lambda dtype: pltpu.VMEM((GDN_HEADS, t, LANES), dtype)
    return pl.pallas_call(
        functools.partial(_gdn_kernel, seq=t),
        grid=(b,),
        in_specs=[tok(0), tok(1), tok(2), tok(3),
                  pl.BlockSpec((None, t, LANES), lambda i: (i, 0, 0)), row, row, row],
        out_specs=pl.BlockSpec((None, t, GDN_W), lambda i: (i, 0, 0)),
        out_shape=jax.ShapeDtypeStruct((b, t, GDN_W), BF16),
        scratch_shapes=[head_buf(F32), head_buf(F32), head_buf(BF16), head_buf(F32), head_buf(BF16),
                        head_buf(F32)]
        + [pltpu.VMEM((GDN_HEADS, t // GDN_CHUNK * SUBLANES, LANES), F32)],
        compiler_params=_params("arbitrary"),
        name="gdn",
    )(a, a, a, a, bd, alog_row, dtb_row, norm_row)


def _stage_values(v_ref, vt_s, seq, dv):
    aug = dv + ATT_AUG
    ones_row = jnp.where(lax.broadcasted_iota(jnp.int32, (ATT_AUG, ATT_K_TILE), 0) == 0, 1.0, 0.0).astype(BF16)
    for g in range(v_ref.shape[1] // LANES):
        for t in range(seq // ATT_K_TILE):
            for h in range(LANES // dv):
                vt_s[g, t, h * aug + dv:(h + 1) * aug, :] = ones_row
        for c in range(seq // LANES):
            t, off = divmod(c * LANES, ATT_K_TILE)
            blk = v_ref[c * LANES:(c + 1) * LANES, g * LANES:(g + 1) * LANES].astype(F32).T.astype(BF16)
            for h in range(LANES // dv):
                vt_s[g, t, h * aug:h * aug + dv, off:off + LANES] = blk[h * dv:(h + 1) * dv, :]


def _stack_masked(q, width):
    lane = lax.broadcasted_iota(jnp.int32, q.shape, 1)
    return jnp.concatenate(
        [jnp.where((lane >= j * width) & (lane < (j + 1) * width), q, 0.0).astype(BF16)
         for j in range(LANES // width)], axis=0)


def _softmax_init(rows, width):
    return (jnp.full((1, width), -jnp.inf, F32), jnp.zeros((rows, width), F32))


def _softmax_tile(state, scores, cnt, vt):
    ps, alphas, maxes = [], [], []
    for (m_old, _), s in zip(state, scores):
        m_new = jnp.maximum(m_old, jnp.max(s, axis=0, keepdims=True))
        p = jnp.exp2(s - m_new)
        if cnt is not None:
            p = p * cnt
        alphas.append(jnp.exp2(m_old - m_new))
        maxes.append(m_new)
        ps.append(p.astype(BF16))
    pv = jnp.dot(vt, jnp.concatenate(ps, axis=1), preferred_element_type=F32)
    width = pv.shape[1] // len(state)
    return [(m, a * acc + pv[:, n * width:(n + 1) * width])
            for n, (m, a, (_, acc)) in enumerate(zip(maxes, alphas, state))]


def _dsw_kernel(q_ref, k_ref, v_ref, o_ref, vt_s, cnt_s, bias_s, *, seq):
    tq, tk = ATT_Q_TILE, ATT_K_TILE
    blocks = q_ref.shape[1] // LANES
    dv = DSW_HEAD_DIM
    heads = LANES // dv
    aug = dv + ATT_AUG
    _stage_values(v_ref, vt_s, seq, dv)

    @pl.when((pl.program_id(0) == 0) & (pl.program_id(1) == 0))
    def _():
        kr = lax.broadcasted_iota(jnp.int32, (tk, tq), 0)
        qc = lax.broadcasted_iota(jnp.int32, (tk, tq), 1)
        for d in range(DSW_FAR_TILE + 1):
            delta = d * tq + qc - kr
            causal = delta >= 0
            cnt = (jnp.where(causal & (delta <= 128), 1.0, 0.0)
                   + jnp.where(causal & (delta <= DSW_MAX_BAND) & ((delta & 3) == 0), 1.0, 0.0)
                   + jnp.where(causal & ((delta & 15) == 0), 1.0, 0.0))
            cnt_s[d] = cnt.astype(F32)
            bias_s[d] = jnp.where(cnt > 0.0, 0.0, -jnp.inf).astype(F32)

    for qi in range(seq // tq):
        qrows = slice(qi * tq, (qi + 1) * tq)
        qstack = [_stack_masked(q_ref[qrows, g * LANES:(g + 1) * LANES].astype(F32), DSW_HEAD_DIM)
                  for g in range(blocks)]
        state = [_softmax_init(aug, tq) for _ in range(blocks * heads)]
        for kj in range((qi * tq) // tk + 1):
            krows = slice(kj * tk, (kj + 1) * tk)
            far = min(qi - kj * (tk // tq), DSW_FAR_TILE)
            cnt = cnt_s[far] if far < DSW_FAR_TILE else None
            bias = bias_s[far]
            s = [_dot_nt(k_ref[krows, g * LANES:(g + 1) * LANES], qstack[g]) for g in range(blocks)]
            for g in range(blocks):
                for h in range(heads):
                    i = g * heads + h
                    state[i:i + 1] = _softmax_tile(state[i:i + 1], [s[g][:, h * tq:(h + 1) * tq] + bias],
                                                   cnt, vt_s[g, kj, h * aug:(h + 1) * aug, :])
        o = jnp.concatenate([acc[0:dv, :] / acc[dv:dv + 1, :] for _, acc in state], axis=0)
        o_ref[qrows, :] = o.T.astype(o_ref.dtype)


def _dsw(bc):
    b, t, _ = bc.shape
    blocks = DSW_W // LANES
    spec = lambda j: pl.BlockSpec((None, t, LANES), lambda i, p: (i, 0, j * blocks + p))
    return pl.pallas_call(
        functools.partial(_dsw_kernel, seq=t),
        grid=(b, blocks),
        in_specs=[spec(0), spec(1), spec(2)],
        out_specs=pl.BlockSpec((None, t, LANES), lambda i, p: (i, 0, p)),
        out_shape=jax.ShapeDtypeStruct((b, t, DSW_W), BF16),
        scratch_shapes=[pltpu.VMEM((1, t // ATT_K_TILE,
                                    LANES // DSW_HEAD_DIM * (DSW_HEAD_DIM + ATT_AUG), ATT_K_TILE), BF16)]
        + [pltpu.VMEM((DSW_FAR_TILE + 1, ATT_K_TILE, ATT_Q_TILE), F32)] * 2,
        compiler_params=_params("arbitrary", "arbitrary"),
        name="dsw",
    )(bc, bc, bc)


def _diff_kernel(q_ref, k_ref, v_ref, lam_ref, laminit_ref, nw_ref, o_ref, vt_s, *, seq):
    tq, tk = ATT_Q_TILE, ATT_K_TILE
    blocks = q_ref.shape[1] // LANES
    dv = DIFF_V_DIM
    heads = LANES // dv
    aug = dv + ATT_AUG
    _stage_values(v_ref, vt_s, seq, dv)
    kr = lax.broadcasted_iota(jnp.int32, (tk, tq), 0)
    qc = lax.broadcasted_iota(jnp.int32, (tk, tq), 1)
    lv = lam_ref[...]
    lam_init = laminit_ref[...]
    lam = (jnp.exp(jnp.sum(lv[0:1, :] * lv[1:2, :], keepdims=True))
           - jnp.exp(jnp.sum(lv[2:3, :] * lv[3:4, :], keepdims=True)) + lam_init)
    nw = nw_ref[...]

    for qi in range(seq // tq):
        qrows = slice(qi * tq, (qi + 1) * tq)
        qstack = [_stack_masked(q_ref[qrows, g * LANES:(g + 1) * LANES].astype(F32), DIFF_QK_DIM)
                  for g in range(blocks)]
        state = [_softmax_init(dv + ATT_AUG, tq) for _ in range(2 * heads * blocks)]
        n_full = (qi * tq) // tk
        for kj in range(n_full + 1):
            krows = slice(kj * tk, (kj + 1) * tk)
            valid = None if kj < n_full else kr + (kj * tk - qi * tq) <= qc
            s = [_dot_nt(k_ref[krows, g * LANES:(g + 1) * LANES], qstack[g])
                 for g in range(blocks)]
            for g in range(blocks):
                for h in range(heads):
                    scores = [s[g][:, (2 * h + mp) * tq:(2 * h + mp + 1) * tq] for mp in range(2)]
                    if valid is not None:
                        scores = [jnp.where(valid, x, -jnp.inf) for x in scores]
                    first = 2 * (heads * g + h)
                    state[first:first + 2] = _softmax_tile(state[first:first + 2], scores, None,
                                                           vt_s[g, kj, h * aug:(h + 1) * aug, :])
        att = [acc[0:dv, :] / acc[dv:dv + 1, :] for _, acc in state]
        halves = []
        for h in range(heads * blocks):
            o = att[2 * h] - lam * att[2 * h + 1]
            halves.append(o * lax.rsqrt(jnp.mean(o * o, axis=0, keepdims=True) + EPS))
        o = jnp.concatenate(halves, axis=0).T
        o_ref[qrows, :] = (o * nw * (1.0 - lam_init)).astype(o_ref.dtype)


def _diff(bc, lam_vecs, lam_init, norm_row):
    b, t, _ = bc.shape
    blocks = DIFF_W // LANES
    base = 3 * DSW_W // LANES
    spec = lambda j: pl.BlockSpec((None, t, LANES), lambda i, p: (i, 0, base + j * blocks + p))
    return pl.pallas_call(
        functools.partial(_diff_kernel, seq=t),
        grid=(b, blocks),
        in_specs=[spec(0), spec(1), spec(2),
                  pl.BlockSpec((4, DIFF_QK_DIM), lambda i, p: (0, 0)),
                  pl.BlockSpec((1, 1), lambda i, p: (0, 0)),
                  pl.BlockSpec((1, LANES), lambda i, p: (0, p))],
        out_specs=pl.BlockSpec((None, t, LANES), lambda i, p: (i, 0, p)),
        out_shape=jax.ShapeDtypeStruct((b, t, DIFF_W), BF16),
        scratch_shapes=[pltpu.VMEM((1, t // ATT_K_TILE,
                                    LANES // DIFF_V_DIM * (DIFF_V_DIM + ATT_AUG), ATT_K_TILE), BF16)],
        compiler_params=_params("arbitrary", "arbitrary"),
        name="diff",
    )(bc, bc, bc, lam_vecs, lam_init, norm_row)


def _layer_norm(h, g, b):
    mu = jnp.mean(h, axis=-1, keepdims=True)
    hc = h - mu
    var = jnp.mean(hc * hc, axis=-1, keepdims=True)
    return hc * lax.rsqrt(var + EPS) * g + b


def _residual_layer_norm(project, x_ref, g_ref, b_ref, o_ref, alpha):
    tm = x_ref.shape[0]
    ahead = project(slice(0, LN_ROWS))
    for r in range(0, tm, LN_ROWS):
        y = ahead
        if r + LN_ROWS < tm:
            ahead = project(slice(r + LN_ROWS, r + 2 * LN_ROWS))
        o_ref[r:r + LN_ROWS, :] = _layer_norm(alpha * x_ref[r:r + LN_ROWS, :] + y, g_ref[...], b_ref[...])


def _out_ln_kernel(x_ref, ya_ref, yb_ref, yc_ref, w_ref, g_ref, b_ref, o_ref, *, alpha):
    def project(rows):
        mixed = jnp.concatenate([ya_ref[rows, :], yb_ref[rows, :], yc_ref[rows, :]], axis=1)
        return jnp.dot(mixed, w_ref[...], preferred_element_type=F32)

    _residual_layer_norm(project, x_ref, g_ref, b_ref, o_ref, alpha)


def _out_ln(x, ya, yb, yc, w, g, b, layer, alpha):
    n, d = x.shape
    tm = min(LN_ROW_TILE, n)
    rows = lambda width: pl.BlockSpec((tm, width), lambda i: (i, 0))
    vec = pl.BlockSpec((None, 1, d), lambda i: (layer, 0, 0))
    return pl.pallas_call(
        functools.partial(_out_ln_kernel, alpha=alpha),
        grid=(n // tm,),
        in_specs=[rows(d), rows(GDN_W), rows(DSW_W), rows(DIFF_W),
                  pl.BlockSpec((None, d, d), lambda i: (layer, 0, 0)), vec, vec],
        out_specs=rows(d),
        out_shape=jax.ShapeDtypeStruct((n, d), F32),
        compiler_params=_params("arbitrary"),
        name="out_ln",
    )(x, ya, yb, yc, w, g, b)


def _ffn_up_kernel(x_ref, w_ref, cg_ref, cv_ref, h_ref, work_g, work_v, carry_g, carry_v,
                   *, tiles_per_seq, col_chunk):
    tm = x_ref.shape[0]
    width = h_ref.shape[1]

    @pl.when(pl.program_id(0) % tiles_per_seq == 0)
    def _():
        carry_g[...] = jnp.zeros(carry_g.shape, F32)
        carry_v[...] = jnp.zeros(carry_v.shape, F32)

    x = x_ref[...].astype(BF16)
    first = SUBLANES - (FFN_CONV - 1)

    def project(c, slot):
        cols = slice(c, c + col_chunk)
        for first_col, work, carry in ((0, work_g, carry_g), (width, work_v, carry_v)):
            work[slot, 0:SUBLANES, :] = carry[:, cols]
            work[slot, SUBLANES:SUBLANES + tm, :] = jnp.dot(
                x, w_ref[:, first_col + c:first_col + c + col_chunk], preferred_element_type=F32)
            carry[:, cols] = work[slot, tm:tm + SUBLANES, :]

    def conv(work, slot, taps, r):
        acc = work[slot, first + r:first + r + EPI_ROWS, :] * taps[0]
        for j in range(1, FFN_CONV):
            acc = acc + work[slot, first + j + r:first + j + r + EPI_ROWS, :] * taps[j]
        return acc

    project(0, 0)
    for n, c in enumerate(range(0, width, col_chunk)):
        slot = n % 2
        if c + col_chunk < width:
            project(c + col_chunk, 1 - slot)
        cols = slice(c, c + col_chunk)
        taps_g = [jnp.broadcast_to(0.5 * cg_ref[j:j + 1, cols], (EPI_ROWS, col_chunk)) for j in range(FFN_CONV)]
        taps_v = [jnp.broadcast_to(cv_ref[j:j + 1, cols], (EPI_ROWS, col_chunk)) for j in range(FFN_CONV)]
        for r in range(0, tm, EPI_ROWS):
            half = conv(work_g, slot, taps_g, r)
            val = conv(work_v, slot, taps_v, r)
            h_ref[r:r + EPI_ROWS, cols] = ((half + half * jnp.tanh(half)) * val).astype(h_ref.dtype)


def _ffn_up(x, w, cg, cv, layer, seq):
    n, d = x.shape
    width = w.shape[-1] // 2
    tm = min(ROW_TILE, seq)
    col_chunk = 256
    wspec = pl.BlockSpec((None, d, 2 * width), lambda i: (layer, 0, 0))
    cspec = pl.BlockSpec((None, FFN_CONV, width), lambda i: (layer, 0, 0))
    return pl.pallas_call(
        functools.partial(_ffn_up_kernel, tiles_per_seq=seq // tm, col_chunk=col_chunk),
        grid=(n // tm,),
        in_specs=[pl.BlockSpec((tm, d), lambda i: (i, 0)), wspec, cspec, cspec],
        out_specs=pl.BlockSpec((tm, width), lambda i: (i, 0)),
        out_shape=jax.ShapeDtypeStruct((n, width), BF16),
        scratch_shapes=[pltpu.VMEM((2, tm + SUBLANES, col_chunk), F32)] * 2
        + [pltpu.VMEM((SUBLANES, width), F32)] * 2,
        compiler_params=_params("arbitrary"),
        name="ffn_up",
    )(x, w, cg, cv)


def _ffn_down_kernel(x_ref, h_ref, w_ref, g_ref, b_ref, o_ref, *, alpha):
    project = lambda rows: jnp.dot(h_ref[rows, :], w_ref[...], preferred_element_type=F32)
    _residual_layer_norm(project, x_ref, g_ref, b_ref, o_ref, alpha)


def _ffn_down(x, h, w, g, b, layer, alpha):
    n, d = x.shape
    width = h.shape[1]
    tm = min(LN_ROW_TILE, n)
    vec = pl.BlockSpec((None, 1, d), lambda i: (layer, 0, 0))
    return pl.pallas_call(
        functools.partial(_ffn_down_kernel, alpha=alpha),
        grid=(n // tm,),
        in_specs=[pl.BlockSpec((tm, d), lambda i: (i, 0)), pl.BlockSpec((tm, width), lambda i: (i, 0)),
                  pl.BlockSpec((None, width, d), lambda i: (layer, 0, 0)), vec, vec],
        out_specs=pl.BlockSpec((tm, d), lambda i: (i, 0)),
        out_shape=jax.ShapeDtypeStruct((n, d), F32),
        compiler_params=_params("arbitrary"),
        name="ffn_down",
    )(x, h, w, g, b)


def _pad_last(a, width):
    return jnp.pad(a, [(0, 0)] * (a.ndim - 1) + [(0, width - a.shape[-1])])


def kernel(x, w_in, gdn_conv, gdn_a_log, gdn_dt_bias, gdn_norm, diff_lambda, diff_norm, w_out,
           ln1_g, ln1_b, w_up, ffn_conv, w_down, ln2_g, ln2_b):
    batch, seq, d = x.shape
    depth = w_in.shape[0]
    d_ff = w_down.shape[1]
    ff_pad = -(-d_ff // LANES) * LANES
    alpha = (2 * depth) ** 0.25
    assert w_in.shape[2] == A_W + 2 * GDN_HEADS + BC_W and w_out.shape[1] == GDN_W + DSW_W + DIFF_W
    assert seq % ROW_TILE == 0 and (batch * seq) % LN_ROW_TILE == 0
    assert seq % ATT_Q_TILE == 0 and seq % ATT_K_TILE == 0 and seq % (GDN_GROUP * GDN_CHUNK) == 0
    assert seq <= 2048

    bd0 = A_W
    bc0 = A_W + 2 * GDN_HEADS
    w_in_r = jnp.concatenate(
        [w_in[:, :, :A_W], w_in[:, :, bc0:], _pad_last(w_in[:, :, bd0:bc0], LANES)], axis=-1).astype(BF16)
    scale = jnp.concatenate([
        jnp.full((DSW_W,), DSW_HEAD_DIM ** -0.5 * LOG2E, F32), jnp.ones((2 * DSW_W,), F32),
        jnp.full((DIFF_QK_W,), DIFF_QK_DIM ** -0.5 * LOG2E, F32), jnp.ones((DIFF_QK_W + DIFF_W,), F32)])[None, :]
    lane_row = lambda v, off: jnp.pad(v, ((0, 0), (off, LANES - off - v.shape[1])))[:, None, :]
    alog_row = lane_row(gdn_a_log, GDN_HEADS)
    dtb_row = lane_row(gdn_dt_bias, GDN_HEADS)
    gdn_norm_row = gdn_norm[:, None, :]
    diff_norm_row = jnp.tile(diff_norm, (1, DIFF_HEADS))[:, None, :]
    lam_init = jnp.asarray([0.8 - 0.6 * math.exp(-0.3 * l) for l in range(depth)], F32)[:, None, None]
    w_out_b = w_out.astype(BF16)
    w_up_r = jnp.concatenate(
        [_pad_last(w_up[:, :, :d_ff], ff_pad), _pad_last(w_up[:, :, d_ff:], ff_pad)], axis=-1).astype(BF16)
    cg = _pad_last(ffn_conv[:, :, :d_ff], ff_pad)
    cv = _pad_last(ffn_conv[:, :, d_ff:], ff_pad)
    w_down_b = jnp.pad(w_down, ((0, 0), (0, ff_pad - d_ff), (0, 0))).astype(BF16)
    vec3 = lambda v: v[:, None, :]

    xf = x.reshape(batch * seq, d)
    for l in range(depth):
        a, bc, bd = _in_proj(xf, w_in_r, scale, gdn_conv, l, seq)
        bc3 = bc.reshape(batch, seq, BC_W)
        ya = _gdn(a.reshape(batch, seq, A_W), bd.reshape(batch, seq, LANES),
                  alog_row, dtb_row, gdn_norm_row, l)
        yb = _dsw(bc3)
        yc = _diff(bc3, diff_lambda[l], lam_init[l], diff_norm_row[l])
        n = batch * seq
        x1 = _out_ln(xf, ya.reshape(n, GDN_W), yb.reshape(n, DSW_W), yc.reshape(n, DIFF_W),
                     w_out_b, vec3(ln1_g), vec3(ln1_b), l, alpha)
        h = _ffn_up(x1, w_up_r, cg, cv, l, seq)
        xf = _ffn_down(x1, h, w_down_b, vec3(ln2_g), vec3(ln2_b), l, alpha)
    return xf.reshape(batch, seq, d)
```
